```python
import jax, jax.numpy as jnp
from jax import lax
import numpy as np

D_MODEL = 1024
BATCH = 4
SEQ = 4096
DEPTH = 2

CTX_LEN = 256
GRID_W = 64
EPS = 1e-6
HALF = 0.5
D_FF = 2816
N_ADA = 9
CONV_CH = 384
CONV_K = 31
N_HEADS = 8
Q_LORA = 384
KV_LORA = 256
QK_NOPE = 64
QK_ROPE = 32
V_DIM = 64
ROPE_AXIS = QK_ROPE // 2
ROPE_BASE = 10000.0
ATTN_SCALE = (QK_NOPE + QK_ROPE) ** -0.5
Q_BLOCK = 128
FOURIER_GROUPS = 4
FOURIER_GROUP_CH = 128
FOURIER_WIDTH = FOURIER_GROUPS * FOURIER_GROUP_CH
MIX_WIDTH = 2 * CONV_CH + Q_LORA + KV_LORA + QK_ROPE + FOURIER_WIDTH
N_BRANCH = 3

kernel_name = 'hybrid_conv_mla_fourier_dit_block'


def rms_norm(x, g):
    xf = x.astype(jnp.float32)
    y = xf * lax.rsqrt(jnp.mean(xf * xf, axis=-1, keepdims=True) + EPS)
    return (y * g.astype(jnp.float32)).astype(x.dtype)


def layer_norm(x, g, b):
    xf = x.astype(jnp.float32)
    mu = jnp.mean(xf, axis=-1, keepdims=True)
    var = jnp.mean(jnp.square(xf - mu), axis=-1, keepdims=True)
    y = (xf - mu) * lax.rsqrt(var + EPS)
    return (y * g.astype(jnp.float32) + b.astype(jnp.float32)).astype(x.dtype)


def ada_norm(h, g, shift, scale):
    return rms_norm(h, g) * (1 + scale) + shift


def swiglu(x, w1, w3, w2):
    return (jax.nn.silu(x @ w1) * (x @ w3)) @ w2


def axial_rope_tables(n_tok, dtype):
    rows = n_tok // GRID_W
    row = jnp.broadcast_to(jnp.arange(rows, dtype=jnp.float32)[:, None], (rows, GRID_W)).reshape(-1)
    col = jnp.broadcast_to(jnp.arange(GRID_W, dtype=jnp.float32)[None, :], (rows, GRID_W)).reshape(-1)
    inv = 1.0 / (ROPE_BASE ** (jnp.arange(ROPE_AXIS // 2, dtype=jnp.float32) * 2.0 / ROPE_AXIS))
    ar = row[:, None] * inv
    ac = col[:, None] * inv
    ang = jnp.concatenate([ar, ar, ac, ac], axis=-1)
    return jnp.cos(ang).astype(dtype), jnp.sin(ang).astype(dtype)


def rotate_half_axial(x):
    xr = x.reshape(x.shape[:-1] + (2, 2, ROPE_AXIS // 2))
    xr = jnp.concatenate([-xr[..., 1:2, :], xr[..., 0:1, :]], axis=-2)
    return xr.reshape(x.shape)


def apply_rope(x, cos, sin):
    return x * cos + rotate_half_axial(x) * sin


def split_mixing(z):
    o1 = 2 * CONV_CH
    o2 = o1 + Q_LORA
    o3 = o2 + KV_LORA
    o4 = o3 + QK_ROPE
    return z[..., :o1], z[..., o1:o2], z[..., o2:o3], z[..., o3:o4], z[..., o4:]


def conv_module(zc, w_dw, b_dw, ln_g, ln_b, w_pw, b_pw):
    a, gt = jnp.split(zc, 2, axis=-1)
    v = a * jax.nn.sigmoid(gt)
    v = lax.conv_general_dilated(v, w_dw[:, None, :].astype(v.dtype), window_strides=(1,),
                                 padding=[(CONV_K // 2, CONV_K // 2)],
                                 dimension_numbers=('NWC', 'WIO', 'NWC'),
                                 feature_group_count=CONV_CH) + b_dw
    v = layer_norm(v, ln_g, ln_b)
    return jax.nn.silu(v) @ w_pw + b_pw


def fourier_mix(zf, w, b):
    bsz, t, _ = zf.shape
    g = zf.reshape(bsz, t, FOURIER_GROUPS, FOURIER_GROUP_CH).astype(jnp.float32)
    f = jnp.fft.fftn(g, axes=(1, 3), norm='ortho').real.astype(zf.dtype)
    return f.reshape(bsz, t, FOURIER_WIDTH) @ w + b


def mla_q(cq, g_qn, w_uq, cos, sin):
    bsz, t, _ = cq.shape
    q = (rms_norm(cq, g_qn) @ w_uq).reshape(bsz, t, N_HEADS, QK_NOPE + QK_ROPE)
    if cos is None:
        return q
    q_nope, q_rope = q[..., :QK_NOPE], q[..., QK_NOPE:]
    return jnp.concatenate([q_nope, apply_rope(q_rope, cos[:, None, :], sin[:, None, :])], axis=-1)


def mla_kv(ckv, kr, g_kvn, w_ukv, cos, sin):
    bsz, t, _ = ckv.shape
    kv = (rms_norm(ckv, g_kvn) @ w_ukv).reshape(bsz, t, N_HEADS, QK_NOPE + V_DIM)
    k_nope, v = kv[..., :QK_NOPE], kv[..., QK_NOPE:]
    if cos is not None:
        kr = apply_rope(kr, cos, sin)
    k = jnp.concatenate([k_nope, jnp.broadcast_to(kr[:, :, None, :], (bsz, t, N_HEADS, QK_ROPE))], axis=-1)
    return k, v


def attend_dense(q, k, v):
    s = jnp.einsum('bqhd,bkhd->bhqk', q, k, preferred_element_type=jnp.float32) * ATTN_SCALE
    p = jax.nn.softmax(s, axis=-1).astype(v.dtype)
    o = jnp.einsum('bhqk,bkhd->bqhd', p, v)
    return o.reshape(o.shape[0], o.shape[1], N_HEADS * V_DIM)


def attend_blocked(q, k, v):
    bsz, s, _, dk = q.shape
    nb = s // Q_BLOCK
    qb = q.reshape(bsz, nb, Q_BLOCK, N_HEADS, dk).transpose(1, 0, 2, 3, 4)
    o = lax.map(lambda blk: attend_dense(blk, k, v), qb)
    return o.transpose(1, 0, 2, 3).reshape(bsz, s, N_HEADS * V_DIM)


def merge_branches(u, y_conv, y_mla, y_four, w_bg, b_bg, w_out):
    g = jax.nn.sigmoid(u @ w_bg + b_bg)
    g0, g1, g2 = jnp.split(g, N_BRANCH, axis=-1)
    return (g0 * y_conv + g1 * y_mla + g2 * y_four) @ w_out


def setup_inputs(seed: int = 0) -> dict:
    key = jax.random.key(seed)
    ks = jax.random.split(key, 40)
    L, D = DEPTH, D_MODEL

    def nrm(k, shape, std):
        return jax.random.normal(k, shape, jnp.float32) * std

    def gain(k, shape):
        return 1.0 + nrm(k, shape, 0.02)

    return {
        'x': nrm(ks[0], (BATCH, SEQ, D), 1.0),
        'c': nrm(ks[1], (BATCH, D), 1.0),
        'ctx': nrm(ks[2], (BATCH, CTX_LEN, D), 1.0),
        'c_ctx': nrm(ks[3], (D,), 1.0),
        'w_ada': nrm(ks[4], (L, D, N_ADA * D), 0.5 * D ** -0.5),
        'b_ada': nrm(ks[5], (L, N_ADA * D), 0.02),
        'g_ffn1': gain(ks[6], (L, D)),
        'w1_ffn1': nrm(ks[7], (L, D, D_FF), D ** -0.5),
        'w3_ffn1': nrm(ks[8], (L, D, D_FF), D ** -0.5),
        'w2_ffn1': nrm(ks[9], (L, D_FF, D), D_FF ** -0.5),
        'g_mix': gain(ks[10], (L, D)),
        'w_in': nrm(ks[11], (L, D, MIX_WIDTH), D ** -0.5),
        'w_dw': nrm(ks[12], (L, CONV_K, CONV_CH), CONV_K ** -0.5),
        'b_dw': nrm(ks[13], (L, CONV_CH), 0.02),
        'ln_g_conv': gain(ks[14], (L, CONV_CH)),
        'ln_b_conv': nrm(ks[15], (L, CONV_CH), 0.02),
        'w_pw_conv': nrm(ks[16], (L, CONV_CH, D), CONV_CH ** -0.5),
        'b_pw_conv': nrm(ks[17], (L, D), 0.02),
        'g_qnorm': gain(ks[18], (L, Q_LORA)),
        'w_uq': nrm(ks[19], (L, Q_LORA, N_HEADS * (QK_NOPE + QK_ROPE)), Q_LORA ** -0.5),
        'g_kvnorm': gain(ks[20], (L, KV_LORA)),
        'w_ukv': nrm(ks[21], (L, KV_LORA, N_HEADS * (QK_NOPE + V_DIM)), KV_LORA ** -0.5),
        'w_o_mla': nrm(ks[22], (L, N_HEADS * V_DIM, D), (N_HEADS * V_DIM) ** -0.5),
        'w_fourier': nrm(ks[23], (L, FOURIER_WIDTH, D), FOURIER_WIDTH ** -0.5),
        'b_fourier': nrm(ks[24], (L, D), 0.02),
        'w_bgate': nrm(ks[25], (L, D, N_BRANCH * D), D ** -0.5),
        'b_bgate': nrm(ks[26], (L, N_BRANCH * D), 0.02),
        'w_out': nrm(ks[27], (L, D, D), D ** -0.5),
        'g_ffn2': gain(ks[28], (L, D)),
        'w1_ffn2': nrm(ks[29], (L, D, D_FF), D ** -0.5),
        'w3_ffn2': nrm(ks[30], (L, D, D_FF), D ** -0.5),
        'w2_ffn2': nrm(ks[31], (L, D_FF, D), D_FF ** -0.5),
        'g_final': gain(ks[32], (D,)),
    }


def reference(x, c, ctx, c_ctx, w_ada, b_ada, g_ffn1, w1_ffn1, w3_ffn1, w2_ffn1, g_mix, w_in,
              w_dw, b_dw, ln_g_conv, ln_b_conv, w_pw_conv, b_pw_conv, g_qnorm, w_uq, g_kvnorm,
              w_ukv, w_o_mla, w_fourier, b_fourier, w_bgate, b_bgate, w_out, g_ffn2, w1_ffn2,
              w3_ffn2, w2_ffn2, g_final):
    cos, sin = axial_rope_tables(x.shape[1], x.dtype)
    h, hc = x, ctx
    for l in range(DEPTH):
        last = l == DEPTH - 1
        mod = jax.nn.silu(c) @ w_ada[l] + b_ada[l]
        sh1, sc1, gt1, sh2, sc2, gt2, sh3, sc3, gt3 = [m[:, None, :] for m in jnp.split(mod, N_ADA, axis=-1)]
        modc = jax.nn.silu(c_ctx) @ w_ada[l] + b_ada[l]
        csh1, csc1, cgt1, csh2, csc2, cgt2, csh3, csc3, cgt3 = jnp.split(modc, N_ADA, axis=-1)

        h = h + HALF * gt1 * swiglu(ada_norm(h, g_ffn1[l], sh1, sc1), w1_ffn1[l], w3_ffn1[l], w2_ffn1[l])
        hc = hc + HALF * cgt1 * swiglu(ada_norm(hc, g_ffn1[l], csh1, csc1), w1_ffn1[l], w3_ffn1[l], w2_ffn1[l])

        u = ada_norm(h, g_mix[l], sh2, sc2)
        uc = ada_norm(hc, g_mix[l], csh2, csc2)
        z_conv, z_cq, z_ckv, z_kr, z_four = split_mixing(u @ w_in[l])
        zc_conv, zc_cq, zc_ckv, zc_kr, zc_four = split_mixing(uc @ w_in[l])

        k_lat, v_lat = mla_kv(z_ckv, z_kr, g_kvnorm[l], w_ukv[l], cos, sin)
        k_ctx, v_ctx = mla_kv(zc_ckv, zc_kr, g_kvnorm[l], w_ukv[l], None, None)
        q_lat = mla_q(z_cq, g_qnorm[l], w_uq[l], cos, sin)
        attn = attend_blocked(q_lat, jnp.concatenate([k_ctx, k_lat], axis=1),
                              jnp.concatenate([v_ctx, v_lat], axis=1))
        y = merge_branches(u,
                           conv_module(z_conv, w_dw[l], b_dw[l], ln_g_conv[l], ln_b_conv[l], w_pw_conv[l], b_pw_conv[l]),
                           attn @ w_o_mla[l],
                           fourier_mix(z_four, w_fourier[l], b_fourier[l]),
                           w_bgate[l], b_bgate[l], w_out[l])
        h = h + gt2 * y
        if not last:
            q_ctx = mla_q(zc_cq, g_qnorm[l], w_uq[l], None, None)
            attn_c = attend_dense(q_ctx, k_ctx, v_ctx)
            yc = merge_branches(uc,
                                conv_module(zc_conv, w_dw[l], b_dw[l], ln_g_conv[l], ln_b_conv[l], w_pw_conv[l], b_pw_conv[l]),
                                attn_c @ w_o_mla[l],
                                fourier_mix(zc_four, w_fourier[l], b_fourier[l]),
                                w_bgate[l], b_bgate[l], w_out[l])
            hc = hc + cgt2 * yc

        h = h + HALF * gt3 * swiglu(ada_norm(h, g_ffn2[l], sh3, sc3), w1_ffn2[l], w3_ffn2[l], w2_ffn2[l])
        if not last:
            hc = hc + HALF * cgt3 * swiglu(ada_norm(hc, g_ffn2[l], csh3, csc3), w1_ffn2[l], w3_ffn2[l], w2_ffn2[l])
    return rms_norm(h, g_final)
```

```python
import functools

import numpy as np
import jax
import jax.numpy as jnp
from jax import lax
from jax.experimental import pallas as pl
from jax.experimental.pallas import tpu as pltpu

D_MODEL = 1024
D_FF = 2816
N_ADA = 9
CONV_CH = 384
CONV_K = 31
N_HEADS = 8
Q_LORA = 384
KV_LORA = 256
QK_NOPE = 64
QK_ROPE = 32
V_DIM = 64
ROPE_AXIS = QK_ROPE // 2
ROPE_BASE = 10000.0
GRID_W = 64
EPS = 1e-6
HALF = 0.5
ATTN_SCALE = (QK_NOPE + QK_ROPE) ** -0.5
FOURIER_GROUPS = 4
FOURIER_GROUP_CH = 128
FOURIER_WIDTH = FOURIER_GROUPS * FOURIER_GROUP_CH
N_BRANCH = 3

LANES = 128
HEAD_PAD = 128
MOD_ROWS = 8
HALO = 16
FFT_GROUP = 8
VMEM_LIMIT = 56 * 1024 * 1024
W_IN_ALL = 2 * CONV_CH + Q_LORA + KV_LORA + 2 * HEAD_PAD + FOURIER_WIDTH

BF16 = jnp.bfloat16
F32 = jnp.float32
NT_DIMS = (((1,), (1,)), ((), ()))
TN_DIMS = (((0,), (0,)), ((), ()))


def _sigmoid(x):
    return 1.0 / (1.0 + jnp.exp(-x))


def _rms(x, g):
    return x * lax.rsqrt(jnp.mean(x * x, axis=-1, keepdims=True) + EPS) * g


def _dot(a, b):
    return jnp.dot(a, b, preferred_element_type=F32)


def _params(n_axes=1):
    return pltpu.CompilerParams(dimension_semantics=("arbitrary",) * n_axes,
                                vmem_limit_bytes=VMEM_LIMIT)


def _const_spec(shape):
    zeros = (0,) * len(shape)
    return pl.BlockSpec(shape, lambda *_: zeros, pipeline_mode=pl.Buffered(1))


def _mod_kernel(cs_ref, w_ref, b_ref, o_ref):
    cs = cs_ref[...]
    a = (cs * _sigmoid(cs)).astype(BF16)
    o_ref[...] = _dot(a, w_ref[...].astype(BF16)) + b_ref[...]


def _mod_call(cs, w_ada, b_ada):
    n_layers = w_ada.shape[0]
    return pl.pallas_call(
        _mod_kernel,
        grid=(n_layers, N_ADA),
        in_specs=[pl.BlockSpec((MOD_ROWS, D_MODEL), lambda l, k: (0, 0)),
                  pl.BlockSpec((None, D_MODEL, D_MODEL), lambda l, k: (l, 0, k)),
                  pl.BlockSpec((None, 1, D_MODEL), lambda l, k: (l, 0, k))],
        out_specs=pl.BlockSpec((None, MOD_ROWS, D_MODEL), lambda l, k: (l, 0, k)),
        out_shape=jax.ShapeDtypeStruct((n_layers, MOD_ROWS, N_ADA * D_MODEL), F32),
        compiler_params=_params(2),
        name="ada_map",
    )(cs, w_ada, b_ada.reshape(n_layers, 1, N_ADA * D_MODEL))


def _mod_spec(layer, k, row_fn):
    base = (layer * N_ADA + k) * MOD_ROWS
    return pl.BlockSpec((None, 1, D_MODEL), lambda i: (base + row_fn(i), 0, 0))


def _ffn_kernel(*refs, chunks, final):
    if final:
        h_ref, sh_ref, sc_ref, gt_ref, g_ref, w1_ref, w3_ref, w2_ref, gf_ref, o_ref = refs
    else:
        h_ref, sh_ref, sc_ref, gt_ref, g_ref, w1_ref, w3_ref, w2_ref, o_ref = refs
    h = h_ref[...]
    xb = (_rms(h, g_ref[...]) * (1.0 + sc_ref[...]) + sh_ref[...]).astype(BF16)
    acc = None
    off = 0
    for c in chunks:
        a = _dot(xb, w1_ref[:, off:off + c])
        b = _dot(xb, w3_ref[:, off:off + c])
        act = (a * _sigmoid(a) * b).astype(BF16)
        y = _dot(act, w2_ref[off:off + c, :])
        acc = y if acc is None else acc + y
        off += c
    out = h + (HALF * gt_ref[...]) * acc
    if final:
        out = _rms(out, gf_ref[...])
    o_ref[...] = out


def _ffn_call(hs, modr, layer, kbase, g, w1, w3, w2, *, n_tiles, tm, row_fn, final_g=None):
    chunks = (512,) * (D_FF // 512) + ((D_FF % 512,) if D_FF % 512 else ())
    final = final_g is not None
    in_specs = [pl.BlockSpec((tm, D_MODEL), lambda i: (i, 0)),
                _mod_spec(layer, kbase, row_fn), _mod_spec(layer, kbase + 1, row_fn),
                _mod_spec(layer, kbase + 2, row_fn),
                _const_spec((1, D_MODEL)),
                _const_spec((D_MODEL, D_FF)), _const_spec((D_MODEL, D_FF)),
                _const_spec((D_FF, D_MODEL))]
    args = [hs, modr, modr, modr, g.reshape(1, D_MODEL), w1, w3, w2]
    if final:
        in_specs.append(_const_spec((1, D_MODEL)))
        args.append(final_g.reshape(1, D_MODEL))
    return pl.pallas_call(
        functools.partial(_ffn_kernel, chunks=chunks, final=final),
        grid=(n_tiles,),
        in_specs=in_specs,
        out_specs=pl.BlockSpec((tm, D_MODEL), lambda i: (i, 0)),
        out_shape=jax.ShapeDtypeStruct((n_tiles * tm, D_MODEL), F32),
        compiler_params=_params(),
        name="ffn",
    )(*args)


def _mix_in_kernel(h_ref, sh_ref, sc_ref, g_ref, win_ref, gq_ref, wqa_ref, wqb_ref, gkv_ref,
                   wuk_ref, wuvt_ref, cos_ref, sin_ref, dftc_ref,
                   q_ref, k_ref, vt_ref, vc_ref, gc_ref, gs_ref):
    h = h_ref[...]
    u = (_rms(h, g_ref[...]) * (1.0 + sc_ref[...]) + sh_ref[...]).astype(BF16)
    z = _dot(u, win_ref[...])
    o_cq = 2 * CONV_CH
    o_ckv = o_cq + Q_LORA
    o_kr = o_ckv + KV_LORA
    o_four = o_kr + 2 * HEAD_PAD
    vc_ref[...] = z[:, :CONV_CH] * _sigmoid(z[:, CONV_CH:o_cq])
    cos = cos_ref[...]
    sin = sin_ref[...]
    cqn = _rms(z[:, o_cq:o_ckv], gq_ref[...]).astype(BF16)
    qa = _dot(cqn, wqa_ref[...])
    qb = _dot(cqn, wqb_ref[...])
    for hh in range(N_HEADS):
        s = slice(hh * HEAD_PAD, (hh + 1) * HEAD_PAD)
        q_ref[hh] = ((qa[:, s] * cos + qb[:, s] * sin) * ATTN_SCALE).astype(BF16)
    ckvn = _rms(z[:, o_ckv:o_kr], gkv_ref[...]).astype(BF16)
    kn = _dot(ckvn, wuk_ref[...])
    kr = z[:, o_kr:o_kr + HEAD_PAD] * cos + z[:, o_kr + HEAD_PAD:o_four] * sin
    for hh in range(N_HEADS):
        s = slice(hh * HEAD_PAD, (hh + 1) * HEAD_PAD)
        k_ref[hh] = (kn[:, s] + kr).astype(BF16)
    vt_ref[...] = lax.dot_general(wuvt_ref[...], ckvn, NT_DIMS,
                                  preferred_element_type=F32).astype(BF16)
    zf = z[:, o_four:].astype(BF16)
    for gi in range(FOURIER_GROUPS):
        s = slice(gi * FOURIER_GROUP_CH, (gi + 1) * FOURIER_GROUP_CH)
        r = _dot(zf[:, s], dftc_ref[...])
        gc_ref[:, s] = r[:, :FOURIER_GROUP_CH]
        gs_ref[:, s] = r[:, FOURIER_GROUP_CH:]


def _mix_in_call(hs, modr, layer, g_mix, wl, tabs, dims):
    bsz, seq, lc, tm = dims["B"], dims["S"], dims["Lc"], dims["tm"]
    n_rows = bsz * (seq + lc)
    tpb = seq // tm
    n_lat = bsz * tpb
    n_tiles = n_rows // tm
    row_fn = dims["row_fn"]

    def tab_idx(i):
        return (jnp.where(i < n_lat, i % tpb, tpb), 0)

    def g_idx(i):
        return (jnp.where(i < n_lat, i % tpb, 0), jnp.where(i < n_lat, i // tpb, bsz + i - n_lat))

    in_specs = [pl.BlockSpec((tm, D_MODEL), lambda i: (i, 0)),
                _mod_spec(layer, 3, row_fn), _mod_spec(layer, 4, row_fn),
                _const_spec((1, D_MODEL)),
                _const_spec((D_MODEL, W_IN_ALL)),
                _const_spec((1, Q_LORA)),
                _const_spec((Q_LORA, N_HEADS * HEAD_PAD)), _const_spec((Q_LORA, N_HEADS * HEAD_PAD)),
                _const_spec((1, KV_LORA)),
                _const_spec((KV_LORA, N_HEADS * HEAD_PAD)),
                _const_spec((N_HEADS * V_DIM, KV_LORA)),
                pl.BlockSpec((tm, HEAD_PAD), tab_idx), pl.BlockSpec((tm, HEAD_PAD), tab_idx),
                _const_spec((FOURIER_GROUP_CH, 2 * FOURIER_GROUP_CH))]
    out_specs = [pl.BlockSpec((N_HEADS, tm, HEAD_PAD), lambda i: (0, i, 0)),
                 pl.BlockSpec((N_HEADS, tm, HEAD_PAD), lambda i: (0, i, 0)),
                 pl.BlockSpec((N_HEADS * V_DIM, tm), lambda i: (0, i)),
                 pl.BlockSpec((tm, CONV_CH), lambda i: (i, 0)),
                 pl.BlockSpec((tm, FOURIER_WIDTH), g_idx),
                 pl.BlockSpec((tm, FOURIER_WIDTH), g_idx)]
    out_shape = [jax.ShapeDtypeStruct((N_HEADS, n_rows, HEAD_PAD), BF16),
                 jax.ShapeDtypeStruct((N_HEADS, n_rows, HEAD_PAD), BF16),
                 jax.ShapeDtypeStruct((N_HEADS * V_DIM, n_rows), BF16),
                 jax.ShapeDtypeStruct((n_rows, CONV_CH), F32),
                 jax.ShapeDtypeStruct((seq, 2 * bsz * FOURIER_WIDTH), F32),
                 jax.ShapeDtypeStruct((seq, 2 * bsz * FOURIER_WIDTH), F32)]
    return pl.pallas_call(
        _mix_in_kernel,
        grid=(n_tiles,),
        in_specs=in_specs,
        out_specs=out_specs,
        out_shape=out_shape,
        compiler_params=_params(),
        name="mix_in",
    )(hs, modr, modr, g_mix.reshape(1, D_MODEL), wl["w_in_all"], wl["g_q"], wl["w_uq_a"],
      wl["w_uq_b"], wl["g_kv"], wl["w_uk"], wl["w_uvt"], tabs["cos"], tabs["sin"], tabs["dftc"])


def _attn_kernel(*refs, with_lat, lc, seq, tk):
    if with_lat:
        q_ref, kc_ref, vc_ref, kl_ref, vl_ref, o_ref = refs
    else:
        q_ref, kc_ref, vc_ref, o_ref = refs
    q = q_ref[...]
    segs = [(kc_ref, vc_ref, 0, lc)]
    if with_lat:
        segs += [(kl_ref, vl_ref, s0, tk) for s0 in range(0, seq, tk)]
    m = l = acc = None
    for kref, vref, s0, size in segs:
        s = lax.dot_general(kref[s0:s0 + size, :], q, NT_DIMS,
                            preferred_element_type=F32)
        smax = jnp.max(s, axis=0, keepdims=True)
        if m is None:
            m_new = smax
            p = jnp.exp(s - m_new)
            l = jnp.sum(p, axis=0, keepdims=True)
            acc = _dot(vref[:, s0:s0 + size], p.astype(BF16))
        else:
            m_new = jnp.maximum(m, smax)
            alpha = jnp.exp(m - m_new)
            p = jnp.exp(s - m_new)
            l = alpha * l + jnp.sum(p, axis=0, keepdims=True)
            acc = alpha * acc + _dot(vref[:, s0:s0 + size], p.astype(BF16))
        m = m_new
    o_ref[...] = (acc / l).astype(BF16)


def _attn_call(q, k, vt, dims, *, latent_queries):
    bsz, seq, lc = dims["B"], dims["S"], dims["Lc"]
    ctx_blk0 = (bsz * seq) // lc
    if latent_queries:
        tq = dims["tq"]
        nq = seq // tq
        q_idx = lambda b, h, i: (h, b * nq + i, 0)
        o_idx = lambda b, h, i: (h, b * nq + i)
        n_q = bsz * seq
    else:
        tq, nq = lc, 1
        q_idx = lambda b, h, i: (h, ctx_blk0 + b, 0)
        o_idx = lambda b, h, i: (h, b)
        n_q = bsz * lc
    in_specs = [pl.BlockSpec((None, tq, HEAD_PAD), q_idx),
                pl.BlockSpec((None, lc, HEAD_PAD), lambda b, h, i: (h, ctx_blk0 + b, 0)),
                pl.BlockSpec((V_DIM, lc), lambda b, h, i: (h, ctx_blk0 + b))]
    args = [q, k, vt]
    if latent_queries:
        in_specs += [pl.BlockSpec((None, seq, HEAD_PAD), lambda b, h, i: (h, b, 0)),
                     pl.BlockSpec((V_DIM, seq), lambda b, h, i: (h, b))]
        args += [k, vt]
    return pl.pallas_call(
        functools.partial(_attn_kernel, with_lat=latent_queries, lc=lc, seq=seq, tk=dims["tk"]),
        grid=(bsz, N_HEADS, nq),
        in_specs=in_specs,
        out_specs=pl.BlockSpec((V_DIM, tq), o_idx),
        out_shape=jax.ShapeDtypeStruct((N_HEADS * V_DIM, n_q), BF16),
        compiler_params=_params(3),
        name="attn_lat" if latent_queries else "attn_ctx",
    )(*args)


def _fft1_kernel(gc_ref, gs_ref, m_ref, zr_ref, zi_ref):
    n1 = gc_ref.shape[0]
    for j in range(FFT_GROUP):
        x = jnp.concatenate([gc_ref[:, j, :], gs_ref[:, j, :]], axis=0).astype(BF16)
        z = _dot(m_ref[j], x)
        zr_ref[:, j, :] = z[:n1]
        zi_ref[:, j, :] = z[n1:]


def _fft2_kernel(zr_ref, zi_ref, w_ref, f_ref):
    for j in range(FFT_GROUP):
        z = jnp.concatenate([zr_ref[j], zi_ref[j]], axis=0).astype(BF16)
        f_ref[:, j, :] = _dot(w_ref[...], z)


def _fft_tables(seq):
    n1 = int(round(seq ** 0.5))
    n2 = seq // n1
    assert n1 * n2 == seq and n1 % FFT_GROUP == 0 and n2 % FFT_GROUP == 0
    k1 = np.arange(n1)[None, :, None]
    t1 = np.arange(n1)[None, None, :]
    t2 = np.arange(n2)[:, None, None]
    ang = -2.0 * np.pi * (((t1 * k1) % n1) / n1 + ((t2 * k1) % seq) / seq)
    ar = np.cos(ang) / np.sqrt(n1)
    ai = np.sin(ang) / np.sqrt(n1)
    m1 = np.concatenate([np.concatenate([ar, ai], axis=2),
                         np.concatenate([ai, -ar], axis=2)], axis=1)
    k2 = np.arange(n2)[:, None]
    tt = np.arange(n2)[None, :]
    a2 = 2.0 * np.pi * ((k2 * tt) % n2) / n2
    w2 = np.concatenate([np.cos(a2), np.sin(a2)], axis=1) / np.sqrt(n2)
    return n1, n2, jnp.asarray(m1, F32).astype(BF16), jnp.asarray(w2, F32).astype(BF16)


def _fourier_lat_call(gc, gs, dims):
    bsz, seq = dims["B"], dims["S"]
    n1, n2, m1, w2 = _fft_tables(seq)
    ncol = bsz * FOURIER_WIDTH
    gc3 = gc.reshape(n1, n2, gc.shape[1])
    gs3 = gs.reshape(n1, n2, gs.shape[1])
    zshape = jax.ShapeDtypeStruct((n1, n2, ncol), F32)
    strided = pl.BlockSpec((n1, FFT_GROUP, ncol), lambda j: (0, j, 0))
    zr, zi = pl.pallas_call(
        _fft1_kernel,
        grid=(n2 // FFT_GROUP,),
        in_specs=[strided, strided,
                  pl.BlockSpec((FFT_GROUP, 2 * n1, 2 * n1), lambda j: (j, 0, 0))],
        out_specs=[strided, strided],
        out_shape=[zshape, zshape],
        compiler_params=_params(),
        name="fft_stage1",
    )(gc3, gs3, m1)
    slab = pl.BlockSpec((FFT_GROUP, n2, ncol), lambda j: (j, 0, 0))
    f3 = pl.pallas_call(
        _fft2_kernel,
        grid=(n1 // FFT_GROUP,),
        in_specs=[slab, slab, _const_spec((n2, 2 * n2))],
        out_specs=pl.BlockSpec((n2, FFT_GROUP, ncol), lambda j: (0, j, 0)),
        out_shape=jax.ShapeDtypeStruct((n2, n1, ncol), F32),
        compiler_params=_params(),
        name="fft_stage2",
    )(zr, zi, w2)
    return f3.reshape(seq, ncol)


def _dft_ctx_kernel(gc_ref, gs_ref, c_ref, s_ref, f_ref):
    f_ref[...] = (_dot(c_ref[...], gc_ref[...].astype(BF16))
                  - _dot(s_ref[...], gs_ref[...].astype(BF16)))


def _fourier_ctx_call(gc, gs, dims):
    bsz, lc = dims["B"], dims["Lc"]
    ncol = bsz * FOURIER_WIDTH
    kt = (np.arange(lc)[:, None] * np.arange(lc)[None, :]) % lc
    ang = 2.0 * np.pi * kt / lc
    cm = jnp.asarray(np.cos(ang) / np.sqrt(lc), F32).astype(BF16)
    sm = jnp.asarray(np.sin(ang) / np.sqrt(lc), F32).astype(BF16)
    ctx_cols = pl.BlockSpec((lc, ncol), lambda j: (0, 1))
    return pl.pallas_call(
        _dft_ctx_kernel,
        grid=(1,),
        in_specs=[ctx_cols, ctx_cols, _const_spec((lc, lc)), _const_spec((lc, lc))],
        out_specs=pl.BlockSpec((lc, ncol), lambda j: (0, 0)),
        out_shape=jax.ShapeDtypeStruct((lc, ncol), F32),
        compiler_params=_params(),
        name="dft_ctx",
    )(gc, gs, cm, sm)


def _merge_kernel(*refs, with_ctx, n_lat, tpb, tm):
    refs = list(refs)
    xpad_ref = refs.pop()
    o_ref = refs.pop()
    it = iter(refs)
    h_ref, sh_ref, sc_ref, gt_ref, g_ref, wbg_ref, bbg_ref = [next(it) for _ in range(7)]
    vc_ref, vp_ref, vn_ref, wdw_ref, bdw_ref, lng_ref, lnb_ref, wpw_ref, bpw_ref = [
        next(it) for _ in range(9)]
    otl_ref = next(it)
    otc_ref = next(it) if with_ctx else None
    wo_ref = next(it)
    fl_ref = next(it)
    fc_ref = next(it) if with_ctx else None
    wf_ref, bf_ref, wout_ref = [next(it) for _ in range(3)]

    i = pl.program_id(0)
    pos = i % tpb
    first = pos == 0
    last = pos == tpb - 1
    if with_ctx:
        is_lat = i < n_lat
        first = jnp.logical_or(first, jnp.logical_not(is_lat))
        last = jnp.logical_or(last, jnp.logical_not(is_lat))

    h = h_ref[...]
    u = (_rms(h, g_ref[...]) * (1.0 + sc_ref[...]) + sh_ref[...]).astype(BF16)

    xpad_ref[0:HALO, :] = jnp.where(first, 0.0, vp_ref[...])
    xpad_ref[HALO:HALO + tm, :] = vc_ref[...]
    xpad_ref[HALO + tm:, :] = jnp.where(last, 0.0, vn_ref[...])
    base = HALO - CONV_K // 2
    conv = None
    for kk in range(CONV_K):
        t = xpad_ref[pl.ds(base + kk, tm), :] * wdw_ref[kk:kk + 1, :]
        conv = t if conv is None else conv + t
    conv = conv + bdw_ref[...]
    mu = jnp.mean(conv, axis=-1, keepdims=True)
    cen = conv - mu
    var = jnp.mean(cen * cen, axis=-1, keepdims=True)
    ln = cen * lax.rsqrt(var + EPS) * lng_ref[...] + lnb_ref[...]
    y_conv = _dot((ln * _sigmoid(ln)).astype(BF16), wpw_ref[...]) + bpw_ref[...]

    ot = otl_ref[...]
    fo = fl_ref[...]
    if with_ctx:
        ot = jnp.where(is_lat, ot, otc_ref[...])
        fo = jnp.where(is_lat, fo, fc_ref[...])
    y_mla = lax.dot_general(ot, wo_ref[...], TN_DIMS, preferred_element_type=F32)
    y_four = _dot(fo.astype(BF16), wf_ref[...]) + bf_ref[...]

    gates = _sigmoid(_dot(u, wbg_ref[...]) + bbg_ref[...])
    mix = (gates[:, :D_MODEL] * y_conv + gates[:, D_MODEL:2 * D_MODEL] * y_mla
           + gates[:, 2 * D_MODEL:] * y_four)
    y = _dot(mix.astype(BF16), wout_ref[...])
    o_ref[...] = h + gt_ref[...] * y


def _merge_call(hs, modr, layer, g_mix, wl, vconv, ot_lat, ot_ctx, f_lat, f_ctx, dims, *, with_ctx):
    bsz, seq, lc, tm = dims["B"], dims["S"], dims["Lc"], dims["tm"]
    tpb = seq // tm
    n_lat = bsz * tpb
    n_tiles = n_lat + (bsz * lc // tm if with_ctx else 0)
    row_fn = dims["row_fn"]
    hpt = tm // HALO
    n_halo = vconv.shape[0] // HALO

    in_specs = [pl.BlockSpec((tm, D_MODEL), lambda i: (i, 0)),
                _mod_spec(layer, 3, row_fn), _mod_spec(layer, 4, row_fn),
                _mod_spec(layer, 5, row_fn),
                _const_spec((1, D_MODEL)),
                _const_spec((D_MODEL, N_BRANCH * D_MODEL)), _const_spec((1, N_BRANCH * D_MODEL)),
                pl.BlockSpec((tm, CONV_CH), lambda i: (i, 0)),
                pl.BlockSpec((HALO, CONV_CH), lambda i: (jnp.maximum(i * hpt - 1, 0), 0)),
                pl.BlockSpec((HALO, CONV_CH), lambda i: (jnp.minimum((i + 1) * hpt, n_halo - 1), 0)),
                _const_spec((CONV_K + 1, CONV_CH)),
                _const_spec((1, CONV_CH)), _const_spec((1, CONV_CH)), _const_spec((1, CONV_CH)),
                _const_spec((CONV_CH, D_MODEL)), _const_spec((1, D_MODEL)),
                pl.BlockSpec((N_HEADS * V_DIM, tm), lambda i: (0, jnp.minimum(i, n_lat - 1)))]
    args = [hs, modr, modr, modr, g_mix.reshape(1, D_MODEL), wl["w_bg"], wl["b_bg"],
            vconv, vconv, vconv, wl["w_dw"], wl["b_dw"], wl["ln_g"], wl["ln_b"], wl["w_pw"],
            wl["b_pw"], ot_lat]
    if with_ctx:
        in_specs.append(pl.BlockSpec((N_HEADS * V_DIM, tm), lambda i: (0, jnp.maximum(i - n_lat, 0))))
        args.append(ot_ctx)
    in_specs.append(_const_spec((N_HEADS * V_DIM, D_MODEL)))
    args.append(wl["w_o"])
    in_specs.append(pl.BlockSpec(
        (tm, FOURIER_WIDTH),
        lambda i: (jnp.where(i < n_lat, i % tpb, 0), jnp.where(i < n_lat, i // tpb, 0))))
    args.append(f_lat)
    if with_ctx:
        in_specs.append(pl.BlockSpec((tm, FOURIER_WIDTH), lambda i: (0, jnp.maximum(i - n_lat, 0))))
        args.append(f_ctx)
    in_specs += [_const_spec((FOURIER_WIDTH, D_MODEL)), _const_spec((1, D_MODEL)),
                 _const_spec((D_MODEL, D_MODEL))]
    args += [wl["w_f"], wl["b_f"], wl["w_out"]]
    return pl.pallas_call(
        functools.partial(_merge_kernel, with_ctx=with_ctx, n_lat=n_lat, tpb=tpb, tm=tm),
        grid=(n_tiles,),
        in_specs=in_specs,
        out_specs=pl.BlockSpec((tm, D_MODEL), lambda i: (i, 0)),
        out_shape=jax.ShapeDtypeStruct((n_tiles * tm, D_MODEL), F32),
        scratch_shapes=[pltpu.VMEM((tm + 2 * HALO, CONV_CH), F32)],
        compiler_params=_params(),
        name="merge",
    )(*args)


def _rotate_half_cols(w):
    wr = w.reshape(w.shape[:-1] + (2, 2, ROPE_AXIS // 2))
    wr = jnp.concatenate([-wr[..., 1:2, :], wr[..., 0:1, :]], axis=-2)
    return wr.reshape(w.shape)


def _layer_weights(l, p):
    o1 = 2 * CONV_CH
    o2 = o1 + Q_LORA
    o3 = o2 + KV_LORA
    o4 = o3 + QK_ROPE
    w_in = p["w_in"][l]
    w_kr = w_in[:, o3:o4]
    zeros = lambda n: jnp.zeros((D_MODEL, n), F32)
    pad_kr = lambda w: jnp.concatenate([zeros(QK_NOPE), w, zeros(HEAD_PAD - QK_NOPE - QK_ROPE)], axis=1)
    w_in_all = jnp.concatenate(
        [w_in[:, :o3], pad_kr(w_kr), pad_kr(_rotate_half_cols(w_kr)), w_in[:, o4:]], axis=1)
    wq = p["w_uq"][l].reshape(Q_LORA, N_HEADS, QK_NOPE + QK_ROPE)
    zq = lambda n: jnp.zeros((Q_LORA, N_HEADS, n), F32)
    w_uq_a = jnp.concatenate([wq, zq(HEAD_PAD - QK_NOPE - QK_ROPE)], axis=-1)
    w_uq_b = jnp.concatenate([zq(QK_NOPE), _rotate_half_cols(wq[..., QK_NOPE:]),
                              zq(HEAD_PAD - QK_NOPE - QK_ROPE)], axis=-1)
    wkv = p["w_ukv"][l].reshape(KV_LORA, N_HEADS, QK_NOPE + V_DIM)
    w_uk = jnp.concatenate([wkv[..., :QK_NOPE], jnp.zeros((KV_LORA, N_HEADS, HEAD_PAD - QK_NOPE), F32)],
                           axis=-1)
    w_uvt = wkv[..., QK_NOPE:].reshape(KV_LORA, N_HEADS * V_DIM).T
    row = lambda v: v.reshape(1, -1)
    return {
        "w_in_all": w_in_all.astype(BF16),
        "g_q": row(p["g_qnorm"][l]), "g_kv": row(p["g_kvnorm"][l]),
        "w_uq_a": w_uq_a.reshape(Q_LORA, N_HEADS * HEAD_PAD).astype(BF16),
        "w_uq_b": w_uq_b.reshape(Q_LORA, N_HEADS * HEAD_PAD).astype(BF16),
        "w_uk": w_uk.reshape(KV_LORA, N_HEADS * HEAD_PAD).astype(BF16),
        "w_uvt": w_uvt.astype(BF16),
        "w_bg": p["w_bgate"][l].astype(BF16), "b_bg": row(p["b_bgate"][l]),
        "w_dw": jnp.concatenate([p["w_dw"][l], jnp.zeros((1, CONV_CH), F32)], axis=0),
        "b_dw": row(p["b_dw"][l]), "ln_g": row(p["ln_g_conv"][l]), "ln_b": row(p["ln_b_conv"][l]),
        "w_pw": p["w_pw_conv"][l].astype(BF16), "b_pw": row(p["b_pw_conv"][l]),
        "w_o": p["w_o_mla"][l].astype(BF16),
        "w_f": p["w_fourier"][l].astype(BF16), "b_f": row(p["b_fourier"][l]),
        "w_out": p["w_out"][l].astype(BF16),
    }


def _tables(seq, lc):
    rows = seq // GRID_W
    row = jnp.broadcast_to(jnp.arange(rows, dtype=F32)[:, None], (rows, GRID_W)).reshape(-1)
    col = jnp.broadcast_to(jnp.arange(GRID_W, dtype=F32)[None, :], (rows, GRID_W)).reshape(-1)
    inv = 1.0 / (ROPE_BASE ** (jnp.arange(ROPE_AXIS // 2, dtype=F32) * 2.0 / ROPE_AXIS))
    ar = row[:, None] * inv
    ac = col[:, None] * inv
    ang = jnp.concatenate([ar, ar, ac, ac], axis=-1)
    tail = HEAD_PAD - QK_NOPE - QK_ROPE
    cos = jnp.concatenate([jnp.ones((seq, QK_NOPE), F32), jnp.cos(ang), jnp.ones((seq, tail), F32)], axis=1)
    sin = jnp.concatenate([jnp.zeros((seq, QK_NOPE), F32), jnp.sin(ang), jnp.zeros((seq, tail), F32)], axis=1)
    cos = jnp.concatenate([cos, jnp.ones((lc, HEAD_PAD), F32)], axis=0)
    sin = jnp.concatenate([sin, jnp.zeros((lc, HEAD_PAD), F32)], axis=0)
    c = np.arange(FOURIER_GROUP_CH)
    a = 2.0 * np.pi * ((c[:, None] * c[None, :]) % FOURIER_GROUP_CH) / FOURIER_GROUP_CH
    dftc = np.concatenate([np.cos(a), np.sin(a)], axis=1) / np.sqrt(FOURIER_GROUP_CH)
    return {"cos": cos, "sin": sin, "dftc": jnp.asarray(dftc, F32).astype(BF16)}


def kernel(x, c, ctx, c_ctx, w_ada, b_ada, g_ffn1, w1_ffn1, w3_ffn1, w2_ffn1, g_mix, w_in, w_dw, b_dw, ln_g_conv, ln_b_conv, w_pw_conv, b_pw_conv, g_qnorm, w_uq, g_kvnorm, w_ukv, w_o_mla, w_fourier, b_fourier, w_bgate, b_bgate, w_out, g_ffn2, w1_ffn2, w3_ffn2, w2_ffn2, g_final):
    bsz, seq, _ = x.shape
    lc = ctx.shape[1]
    depth = w_ada.shape[0]
    tm = lc
    tf = 2 * tm
    assert bsz + 1 <= MOD_ROWS and seq % tf == 0 and (bsz * lc) % tf == 0 and tm % HALO == 0
    n_lat_rows = bsz * seq
    p = dict(w_in=w_in, w_dw=w_dw, b_dw=b_dw, ln_g_conv=ln_g_conv, ln_b_conv=ln_b_conv,
             w_pw_conv=w_pw_conv, b_pw_conv=b_pw_conv, g_qnorm=g_qnorm, w_uq=w_uq,
             g_kvnorm=g_kvnorm, w_ukv=w_ukv, w_o_mla=w_o_mla, w_fourier=w_fourier,
             b_fourier=b_fourier, w_bgate=w_bgate, b_bgate=b_bgate, w_out=w_out)

    def make_row_fn(tile):
        n_lat_tiles, tiles_per_seq = n_lat_rows // tile, seq // tile
        return lambda i: jnp.where(i < n_lat_tiles, i // tiles_per_seq, bsz)

    dims = {"B": bsz, "S": seq, "Lc": lc, "tm": tm, "row_fn": make_row_fn(tm),
            "tq": min(512, seq), "tk": min(512, seq)}
    ffn_row_fn = make_row_fn(tf)
    n_ffn_all = (n_lat_rows + bsz * lc) // tf
    n_ffn_lat = n_lat_rows // tf

    cs = jnp.concatenate([c, c_ctx[None, :], jnp.zeros((MOD_ROWS - bsz - 1, D_MODEL), F32)], axis=0)
    mod = _mod_call(cs, w_ada, b_ada)
    modr = mod.reshape(depth, MOD_ROWS, N_ADA, D_MODEL).transpose(0, 2, 1, 3)
    modr = modr.reshape(depth * N_ADA * MOD_ROWS, 1, D_MODEL)
    tabs = _tables(seq, lc)

    hs = jnp.concatenate([x.reshape(n_lat_rows, D_MODEL), ctx.reshape(bsz * lc, D_MODEL)], axis=0)
    for l in range(depth):
        last = l == depth - 1
        wl = _layer_weights(l, p)
        hs = _ffn_call(hs, modr, l, 0, g_ffn1[l], w1_ffn1[l].astype(BF16), w3_ffn1[l].astype(BF16),
                       w2_ffn1[l].astype(BF16), n_tiles=n_ffn_all, tm=tf, row_fn=ffn_row_fn)
        q, k, vt, vconv, gc, gs = _mix_in_call(hs, modr, l, g_mix[l], wl, tabs, dims)
        ot_lat = _attn_call(q, k, vt, dims, latent_queries=True)
        f_lat = _fourier_lat_call(gc, gs, dims)
        if last:
            ot_ctx = f_ctx = None
        else:
            ot_ctx = _attn_call(q, k, vt, dims, latent_queries=False)
            f_ctx = _fourier_ctx_call(gc, gs, dims)
        hs = _merge_call(hs, modr, l, g_mix[l], wl, vconv, ot_lat, ot_ctx, f_lat, f_ctx, dims,
                         with_ctx=not last)
        hs = _ffn_call(hs, modr, l, 6, g_ffn2[l], w1_ffn2[l].astype(BF16), w3_ffn2[l].astype(BF16),
                       w2_ffn2[l].astype(BF16), n_tiles=n_ffn_lat if last else n_ffn_all, tm=tf,
                       row_fn=ffn_row_fn, final_g=g_final if last else None)
    return hs.reshape(bsz, seq, D_MODEL)
```

```python
import functools

import numpy as np
import jax
import jax.numpy as jnp
from jax import lax
from jax.experimental import pallas as pl
from jax.experimental.pallas import tpu as pltpu

D_MODEL = 1024
D_FF = 2816
N_ADA = 9
CONV_CH = 384
CONV_K = 31
N_HEADS = 8
Q_LORA = 384
KV_LORA = 256
QK_NOPE = 64
QK_ROPE = 32
V_DIM = 64
ROPE_AXIS = QK_ROPE // 2
ROPE_BASE = 10000.0
GRID_W = 64
EPS = 1e-6
HALF = 0.5
ATTN_SCALE = (QK_NOPE + QK_ROPE) ** -0.5
FOURIER_GROUPS = 4
FOURIER_GROUP_CH = 128
FOURIER_WIDTH = FOURIER_GROUPS * FOURIER_GROUP_CH
N_BRANCH = 3

LANES = 128
SUBLANES = 8
HEAD_PAD = 128
MOD_ROWS = 8
HALO = 16
FFT_GROUP = 8
DENOM_ROWS = 16
Q_SCALE = ATTN_SCALE * float(np.log2(np.e))
VMEM_LIMIT = 56 * 1024 * 1024
W_IN_ALL = 2 * CONV_CH + Q_LORA + KV_LORA + 2 * HEAD_PAD + FOURIER_WIDTH

BF16 = jnp.bfloat16
F32 = jnp.float32
NT_DIMS = (((1,), (1,)), ((), ()))
TN_DIMS = (((0,), (0,)), ((), ()))


def _sigmoid(x):
    return 1.0 / (1.0 + jnp.exp(-x))


def _rms(x, g):
    return x * lax.rsqrt(jnp.mean(x * x, axis=-1, keepdims=True) + EPS) * g


def _dot(a, b):
    return jnp.dot(a, b, preferred_element_type=F32)


def _params(n_axes=1):
    return pltpu.CompilerParams(dimension_semantics=("arbitrary",) * n_axes,
                                vmem_limit_bytes=VMEM_LIMIT)


def _const_spec(shape):
    zeros = (0,) * len(shape)
    return pl.BlockSpec(shape, lambda *_: zeros, pipeline_mode=pl.Buffered(1))


def _mod_kernel(cs_ref, w_ref, b_ref, o_ref):
    cs = cs_ref[...]
    a = (cs * _sigmoid(cs)).astype(BF16)
    o_ref[...] = _dot(a, w_ref[...].astype(BF16)) + b_ref[...]


def _mod_call(cs, w_ada, b_ada):
    n_layers = w_ada.shape[0]
    return pl.pallas_call(
        _mod_kernel,
        grid=(n_layers, N_ADA),
        in_specs=[pl.BlockSpec((MOD_ROWS, D_MODEL), lambda l, k: (0, 0)),
                  pl.BlockSpec((None, D_MODEL, D_MODEL), lambda l, k: (l, 0, k)),
                  pl.BlockSpec((None, 1, D_MODEL), lambda l, k: (l, 0, k))],
        out_specs=pl.BlockSpec((None, MOD_ROWS, D_MODEL), lambda l, k: (l, 0, k)),
        out_shape=jax.ShapeDtypeStruct((n_layers, MOD_ROWS, N_ADA * D_MODEL), F32),
        compiler_params=_params(2),
        name="ada_map",
    )(cs, w_ada, b_ada.reshape(n_layers, 1, N_ADA * D_MODEL))


def _mod_spec(layer, k, row_fn):
    base = (layer * N_ADA + k) * MOD_ROWS
    return pl.BlockSpec((None, 1, D_MODEL), lambda i: (base + row_fn(i), 0, 0))


def _ffn_kernel(*refs, chunks, final):
    if final:
        h_ref, sh_ref, sc_ref, gt_ref, g_ref, w1_ref, w3_ref, w2_ref, gf_ref, o_ref = refs
    else:
        h_ref, sh_ref, sc_ref, gt_ref, g_ref, w1_ref, w3_ref, w2_ref, o_ref = refs
    h = h_ref[...]
    xb = (_rms(h, g_ref[...]) * (1.0 + sc_ref[...]) + sh_ref[...]).astype(BF16)
    acc = None
    off = 0
    for c in chunks:
        a = _dot(xb, w1_ref[:, off:off + c])
        b = _dot(xb, w3_ref[:, off:off + c])
        act = (a * _sigmoid(a) * b).astype(BF16)
        y = _dot(act, w2_ref[off:off + c, :])
        acc = y if acc is None else acc + y
        off += c
    out = h + (HALF * gt_ref[...]) * acc
    if final:
        out = _rms(out, gf_ref[...])
    o_ref[...] = out


def _ffn_call(hs, modr, layer, kbase, g, w1, w3, w2, *, n_tiles, tm, row_fn, final_g=None):
    chunks = (512,) * (D_FF // 512) + ((D_FF % 512,) if D_FF % 512 else ())
    final = final_g is not None
    in_specs = [pl.BlockSpec((tm, D_MODEL), lambda i: (i, 0)),
                _mod_spec(layer, kbase, row_fn), _mod_spec(layer, kbase + 1, row_fn),
                _mod_spec(layer, kbase + 2, row_fn),
                _const_spec((1, D_MODEL)),
                _const_spec((D_MODEL, D_FF)), _const_spec((D_MODEL, D_FF)),
                _const_spec((D_FF, D_MODEL))]
    args = [hs, modr, modr, modr, g.reshape(1, D_MODEL), w1, w3, w2]
    if final:
        in_specs.append(_const_spec((1, D_MODEL)))
        args.append(final_g.reshape(1, D_MODEL))
    return pl.pallas_call(
        functools.partial(_ffn_kernel, chunks=chunks, final=final),
        grid=(n_tiles,),
        in_specs=in_specs,
        out_specs=pl.BlockSpec((tm, D_MODEL), lambda i: (i, 0)),
        out_shape=jax.ShapeDtypeStruct((n_tiles * tm, D_MODEL), F32),
        compiler_params=_params(),
        name="ffn",
    )(*args)


def _mix_in_kernel(h_ref, sh_ref, sc_ref, g_ref, win_ref, gq_ref, wqa_ref, wqb_ref, gkv_ref,
                   wuk_ref, wuvt_ref, cos_ref, sin_ref, dftc_ref,
                   q_ref, k_ref, vt_ref, vc_ref, gc_ref, gs_ref):
    h = h_ref[...]
    u = (_rms(h, g_ref[...]) * (1.0 + sc_ref[...]) + sh_ref[...]).astype(BF16)
    z = _dot(u, win_ref[...])
    o_cq = 2 * CONV_CH
    o_ckv = o_cq + Q_LORA
    o_kr = o_ckv + KV_LORA
    o_four = o_kr + 2 * HEAD_PAD
    vc_ref[...] = z[:, :CONV_CH] * _sigmoid(z[:, CONV_CH:o_cq])
    cos = cos_ref[...]
    sin = sin_ref[...]
    cqn = _rms(z[:, o_cq:o_ckv], gq_ref[...]).astype(BF16)
    qa = _dot(cqn, wqa_ref[...])
    qb = _dot(cqn, wqb_ref[...])
    for hh in range(N_HEADS):
        s = slice(hh * HEAD_PAD, (hh + 1) * HEAD_PAD)
        q_ref[hh] = ((qa[:, s] * cos + qb[:, s] * sin) * Q_SCALE).astype(BF16)
    ckvn = _rms(z[:, o_ckv:o_kr], gkv_ref[...]).astype(BF16)
    kn = _dot(ckvn, wuk_ref[...])
    kr = z[:, o_kr:o_kr + HEAD_PAD] * cos + z[:, o_kr + HEAD_PAD:o_four] * sin
    for hh in range(N_HEADS):
        s = slice(hh * HEAD_PAD, (hh + 1) * HEAD_PAD)
        k_ref[hh] = (kn[:, s] + kr).astype(BF16)
    vt_ref[...] = lax.dot_general(wuvt_ref[...], ckvn, NT_DIMS,
                                  preferred_element_type=F32).astype(BF16)
    zf = z[:, o_four:].astype(BF16)
    for gi in range(FOURIER_GROUPS):
        s = slice(gi * FOURIER_GROUP_CH, (gi + 1) * FOURIER_GROUP_CH)
        r = _dot(zf[:, s], dftc_ref[...])
        gc_ref[:, s] = r[:, :FOURIER_GROUP_CH]
        gs_ref[:, s] = r[:, FOURIER_GROUP_CH:]


def _mix_in_call(hs, modr, layer, g_mix, wl, tabs, dims):
    bsz, seq, lc, tm = dims["B"], dims["S"], dims["Lc"], dims["tm"]
    n_rows = bsz * (seq + lc)
    tpb = seq // tm
    n_lat = bsz * tpb
    n_tiles = n_rows // tm
    row_fn = dims["row_fn"]

    def tab_idx(i):
        return (jnp.where(i < n_lat, i % tpb, tpb), 0)

    def g_idx(i):
        return (jnp.where(i < n_lat, i % tpb, tpb), jnp.where(i < n_lat, i // tpb, i - n_lat))

    in_specs = [pl.BlockSpec((tm, D_MODEL), lambda i: (i, 0)),
                _mod_spec(layer, 3, row_fn), _mod_spec(layer, 4, row_fn),
                _const_spec((1, D_MODEL)),
                _const_spec((D_MODEL, W_IN_ALL)),
                _const_spec((1, Q_LORA)),
                _const_spec((Q_LORA, N_HEADS * HEAD_PAD)), _const_spec((Q_LORA, N_HEADS * HEAD_PAD)),
                _const_spec((1, KV_LORA)),
                _const_spec((KV_LORA, N_HEADS * HEAD_PAD)),
                _const_spec((N_HEADS * V_DIM, KV_LORA)),
                pl.BlockSpec((tm, HEAD_PAD), tab_idx), pl.BlockSpec((tm, HEAD_PAD), tab_idx),
                _const_spec((FOURIER_GROUP_CH, 2 * FOURIER_GROUP_CH))]
    out_specs = [pl.BlockSpec((N_HEADS, tm, HEAD_PAD), lambda i: (0, i, 0)),
                 pl.BlockSpec((N_HEADS, tm, HEAD_PAD), lambda i: (0, i, 0)),
                 pl.BlockSpec((N_HEADS * V_DIM, tm), lambda i: (0, i)),
                 pl.BlockSpec((tm, CONV_CH), lambda i: (i, 0)),
                 pl.BlockSpec((tm, FOURIER_WIDTH), g_idx),
                 pl.BlockSpec((tm, FOURIER_WIDTH), g_idx)]
    out_shape = [jax.ShapeDtypeStruct((N_HEADS, n_rows, HEAD_PAD), BF16),
                 jax.ShapeDtypeStruct((N_HEADS, n_rows, HEAD_PAD), BF16),
                 jax.ShapeDtypeStruct((N_HEADS * V_DIM, n_rows), BF16),
                 jax.ShapeDtypeStruct((n_rows, CONV_CH), F32),
                 jax.ShapeDtypeStruct((seq + lc, bsz * FOURIER_WIDTH), F32),
                 jax.ShapeDtypeStruct((seq + lc, bsz * FOURIER_WIDTH), F32)]
    return pl.pallas_call(
        _mix_in_kernel,
        grid=(n_tiles,),
        in_specs=in_specs,
        out_specs=out_specs,
        out_shape=out_shape,
        compiler_params=_params(),
        name="mix_in",
    )(hs, modr, modr, g_mix.reshape(1, D_MODEL), wl["w_in_all"], wl["g_q"], wl["w_uq_a"],
      wl["w_uq_b"], wl["g_kv"], wl["w_uk"], wl["w_uvt"], tabs["cos"], tabs["sin"], tabs["dftc"])


def _attn_kernel(*refs, with_lat, lc, seq, tk):
    if with_lat:
        q_ref, kc_ref, vc_ref, kl_ref, vl_ref, o_ref, s_ref = refs
    else:
        q_ref, kc_ref, vc_ref, o_ref, s_ref = refs
    q = q_ref[...]
    segs = [(kc_ref, vc_ref, 0, lc)]
    if with_lat:
        segs += [(kl_ref, vl_ref, s0, tk) for s0 in range(0, seq, tk)]

    def scores(j):
        kref, _, s0, size = segs[j]
        s = lax.dot_general(kref[s0:s0 + size, :], q, NT_DIMS,
                            preferred_element_type=F32)
        s_ref[j % 2, 0:size, :] = s
        return jnp.max(s, axis=0, keepdims=True)

    m = acc = None
    smax = scores(0)
    for j, (_, vref, s0, size) in enumerate(segs):
        smax_next = scores(j + 1) if j + 1 < len(segs) else None
        m_new = smax if m is None else jnp.maximum(m, smax)
        p = jnp.exp2(s_ref[j % 2, 0:size, :] - m_new).astype(BF16)
        v_aug = jnp.concatenate([vref[:, s0:s0 + size], jnp.ones((DENOM_ROWS, size), BF16)], axis=0)
        pv = _dot(v_aug, p)
        acc = pv if m is None else jnp.exp2(m - m_new) * acc + pv
        m = m_new
        smax = smax_next
    o_ref[...] = (acc[:V_DIM] / acc[V_DIM:V_DIM + 1]).astype(BF16)


def _attn_call(q, k, vt, dims, *, latent_queries):
    bsz, seq, lc = dims["B"], dims["S"], dims["Lc"]
    ctx_blk0 = (bsz * seq) // lc
    if latent_queries:
        tq = dims["tq"]
        nq = seq // tq
        q_idx = lambda b, h, i: (h, b * nq + i, 0)
        o_idx = lambda b, h, i: (h, b * nq + i)
        n_q = bsz * seq
    else:
        tq, nq = lc, 1
        q_idx = lambda b, h, i: (h, ctx_blk0 + b, 0)
        o_idx = lambda b, h, i: (h, b)
        n_q = bsz * lc
    in_specs = [pl.BlockSpec((None, tq, HEAD_PAD), q_idx),
                pl.BlockSpec((None, lc, HEAD_PAD), lambda b, h, i: (h, ctx_blk0 + b, 0)),
                pl.BlockSpec((V_DIM, lc), lambda b, h, i: (h, ctx_blk0 + b))]
    args = [q, k, vt]
    if latent_queries:
        in_specs += [pl.BlockSpec((None, seq, HEAD_PAD), lambda b, h, i: (h, b, 0)),
                     pl.BlockSpec((V_DIM, seq), lambda b, h, i: (h, b))]
        args += [k, vt]
    return pl.pallas_call(
        functools.partial(_attn_kernel, with_lat=latent_queries, lc=lc, seq=seq, tk=dims["tk"]),
        grid=(bsz, N_HEADS, nq),
        in_specs=in_specs,
        out_specs=pl.BlockSpec((V_DIM, tq), o_idx),
        out_shape=jax.ShapeDtypeStruct((N_HEADS * V_DIM, n_q), BF16),
        scratch_shapes=[pltpu.VMEM((2, max(dims["tk"], lc), tq), F32)],
        compiler_params=_params(3),
        name="attn_lat" if latent_queries else "attn_ctx",
    )(*args)


def _fft1_kernel(gc_ref, gs_ref, m_ref, zr_ref, zi_ref):
    n1 = gc_ref.shape[0]
    for j in range(FFT_GROUP):
        x = jnp.concatenate([gc_ref[:, j, :], gs_ref[:, j, :]], axis=0).astype(BF16)
        z = _dot(m_ref[j], x)
        zr_ref[:, j, :] = z[:n1]
        zi_ref[:, j, :] = z[n1:]


def _fft2_kernel(zr_ref, zi_ref, w_ref, f_ref):
    for j in range(FFT_GROUP):
        z = jnp.concatenate([zr_ref[j], zi_ref[j]], axis=0).astype(BF16)
        f_ref[:, j, :] = _dot(w_ref[...], z)


def _fft_tables(seq):
    n1 = int(round(seq ** 0.5))
    n2 = seq // n1
    assert n1 * n2 == seq and n1 % FFT_GROUP == 0 and n2 % FFT_GROUP == 0
    k1 = np.arange(n1)[None, :, None]
    t1 = np.arange(n1)[None, None, :]
    t2 = np.arange(n2)[:, None, None]
    ang = -2.0 * np.pi * (((t1 * k1) % n1) / n1 + ((t2 * k1) % seq) / seq)
    ar = np.cos(ang) / np.sqrt(n1)
    ai = np.sin(ang) / np.sqrt(n1)
    m1 = np.concatenate([np.concatenate([ar, ai], axis=2),
                         np.concatenate([ai, -ar], axis=2)], axis=1)
    k2 = np.arange(n2)[:, None]
    tt = np.arange(n2)[None, :]
    a2 = 2.0 * np.pi * ((k2 * tt) % n2) / n2
    w2 = np.concatenate([np.cos(a2), np.sin(a2)], axis=1) / np.sqrt(n2)
    return n1, n2, jnp.asarray(m1, F32).astype(BF16), jnp.asarray(w2, F32).astype(BF16)


def _fourier_lat_call(gc, gs, dims):
    bsz, seq = dims["B"], dims["S"]
    n1, n2, m1, w2 = _fft_tables(seq)
    ncol = bsz * FOURIER_WIDTH
    assert gc.shape[0] % n2 == 0
    gc3 = gc.reshape(gc.shape[0] // n2, n2, ncol)
    gs3 = gs.reshape(gs.shape[0] // n2, n2, ncol)
    zshape = jax.ShapeDtypeStruct((n1, n2, ncol), F32)
    strided = pl.BlockSpec((n1, FFT_GROUP, ncol), lambda j: (0, j, 0))
    zr, zi = pl.pallas_call(
        _fft1_kernel,
        grid=(n2 // FFT_GROUP,),
        in_specs=[strided, strided,
                  pl.BlockSpec((FFT_GROUP, 2 * n1, 2 * n1), lambda j: (j, 0, 0))],
        out_specs=[strided, strided],
        out_shape=[zshape, zshape],
        compiler_params=_params(),
        name="fft_stage1",
    )(gc3, gs3, m1)
    slab = pl.BlockSpec((FFT_GROUP, n2, ncol), lambda j: (j, 0, 0))
    f3 = pl.pallas_call(
        _fft2_kernel,
        grid=(n1 // FFT_GROUP,),
        in_specs=[slab, slab, _const_spec((n2, 2 * n2))],
        out_specs=pl.BlockSpec((n2, FFT_GROUP, ncol), lambda j: (0, j, 0)),
        out_shape=jax.ShapeDtypeStruct((n2, n1, ncol), F32),
        compiler_params=_params(),
        name="fft_stage2",
    )(zr, zi, w2)
    return f3.reshape(seq, ncol)


def _dft_ctx_kernel(gc_ref, gs_ref, c_ref, s_ref, f_ref):
    f_ref[...] = (_dot(c_ref[...], gc_ref[...].astype(BF16))
                  - _dot(s_ref[...], gs_ref[...].astype(BF16)))


def _fourier_ctx_call(gc, gs, dims):
    bsz, lc = dims["B"], dims["Lc"]
    ncol = bsz * FOURIER_WIDTH
    kt = (np.arange(lc)[:, None] * np.arange(lc)[None, :]) % lc
    ang = 2.0 * np.pi * kt / lc
    cm = jnp.asarray(np.cos(ang) / np.sqrt(lc), F32).astype(BF16)
    sm = jnp.asarray(np.sin(ang) / np.sqrt(lc), F32).astype(BF16)
    ctx_cols = pl.BlockSpec((lc, ncol), lambda j: (dims["S"] // lc, 0))
    return pl.pallas_call(
        _dft_ctx_kernel,
        grid=(1,),
        in_specs=[ctx_cols, ctx_cols, _const_spec((lc, lc)), _const_spec((lc, lc))],
        out_specs=pl.BlockSpec((lc, ncol), lambda j: (0, 0)),
        out_shape=jax.ShapeDtypeStruct((lc, ncol), F32),
        compiler_params=_params(),
        name="dft_ctx",
    )(gc, gs, cm, sm)


def _merge_kernel(*refs, with_ctx, n_lat, tpb, tm):
    refs = list(refs)
    xs_ref = refs.pop()
    xpad_ref = refs.pop()
    o_ref = refs.pop()
    it = iter(refs)
    h_ref, sh_ref, sc_ref, gt_ref, g_ref, wbg_ref, bbg_ref = [next(it) for _ in range(7)]
    vc_ref, vp_ref, vn_ref, wdw_ref, bdw_ref, lng_ref, lnb_ref, wpw_ref, bpw_ref = [
        next(it) for _ in range(9)]
    otl_ref = next(it)
    otc_ref = next(it) if with_ctx else None
    wo_ref = next(it)
    fl_ref = next(it)
    fc_ref = next(it) if with_ctx else None
    wf_ref, bf_ref, wout_ref = [next(it) for _ in range(3)]

    i = pl.program_id(0)
    pos = i % tpb
    first = pos == 0
    last = pos == tpb - 1
    if with_ctx:
        is_lat = i < n_lat
        first = jnp.logical_or(first, jnp.logical_not(is_lat))
        last = jnp.logical_or(last, jnp.logical_not(is_lat))

    h = h_ref[...]
    u = (_rms(h, g_ref[...]) * (1.0 + sc_ref[...]) + sh_ref[...]).astype(BF16)

    xpad_ref[0:HALO, :] = jnp.where(first, 0.0, vp_ref[...])
    xpad_ref[HALO:HALO + tm, :] = vc_ref[...]
    xpad_ref[HALO + tm:, :] = jnp.where(last, 0.0, vn_ref[...])
    base = HALO - CONV_K // 2
    n_shift_rows = xs_ref.shape[1]
    for r in range(SUBLANES):
        xs_ref[r] = xpad_ref[pl.ds(r, n_shift_rows), :]

    def conv_lanes(c0):
        acc = None
        for kk in range(CONV_K):
            off = base + kk
            a0 = (off // SUBLANES) * SUBLANES
            t = (xs_ref[off % SUBLANES, a0:a0 + tm, c0:c0 + LANES]
                 * wdw_ref[kk:kk + 1, c0:c0 + LANES])
            acc = t if acc is None else acc + t
        return acc + bdw_ref[:, c0:c0 + LANES]

    ot = otl_ref[...]
    fo = fl_ref[...]
    if with_ctx:
        ot = jnp.where(is_lat, ot, otc_ref[...])
        fo = jnp.where(is_lat, fo, fc_ref[...])
    gate_pre = []
    conv_parts = []
    for br in range(N_BRANCH):
        cols = slice(br * D_MODEL, (br + 1) * D_MODEL)
        gate_pre.append(_dot(u, wbg_ref[:, cols]) + bbg_ref[:, cols])
        conv_parts.append(conv_lanes(br * LANES))
    y_mla = lax.dot_general(ot, wo_ref[...], TN_DIMS, preferred_element_type=F32)
    y_four = _dot(fo.astype(BF16), wf_ref[...]) + bf_ref[...]
    conv = jnp.concatenate(conv_parts, axis=-1)
    mu = jnp.mean(conv, axis=-1, keepdims=True)
    cen = conv - mu
    var = jnp.mean(cen * cen, axis=-1, keepdims=True)
    ln = cen * lax.rsqrt(var + EPS) * lng_ref[...] + lnb_ref[...]
    y_conv = _dot((ln * _sigmoid(ln)).astype(BF16), wpw_ref[...]) + bpw_ref[...]

    mix = (_sigmoid(gate_pre[0]) * y_conv + _sigmoid(gate_pre[1]) * y_mla
           + _sigmoid(gate_pre[2]) * y_four)
    y = _dot(mix.astype(BF16), wout_ref[...])
    o_ref[...] = h + gt_ref[...] * y


def _merge_call(hs, modr, layer, g_mix, wl, vconv, ot_lat, ot_ctx, f_lat, f_ctx, dims, *, with_ctx):
    bsz, seq, lc, tm = dims["B"], dims["S"], dims["Lc"], dims["tm"]
    tpb = seq // tm
    n_lat = bsz * tpb
    n_tiles = n_lat + (bsz * lc // tm if with_ctx else 0)
    row_fn = dims["row_fn"]
    hpt = tm // HALO
    n_halo = vconv.shape[0] // HALO

    in_specs = [pl.BlockSpec((tm, D_MODEL), lambda i: (i, 0)),
                _mod_spec(layer, 3, row_fn), _mod_spec(layer, 4, row_fn),
                _mod_spec(layer, 5, row_fn),
                _const_spec((1, D_MODEL)),
                _const_spec((D_MODEL, N_BRANCH * D_MODEL)), _const_spec((1, N_BRANCH * D_MODEL)),
                pl.BlockSpec((tm, CONV_CH), lambda i: (i, 0)),
                pl.BlockSpec((HALO, CONV_CH), lambda i: (jnp.maximum(i * hpt - 1, 0), 0)),
                pl.BlockSpec((HALO, CONV_CH), lambda i: (jnp.minimum((i + 1) * hpt, n_halo - 1), 0)),
                _const_spec((CONV_K + 1, CONV_CH)),
                _const_spec((1, CONV_CH)), _const_spec((1, CONV_CH)), _const_spec((1, CONV_CH)),
                _const_spec((CONV_CH, D_MODEL)), _const_spec((1, D_MODEL)),
                pl.BlockSpec((N_HEADS * V_DIM, tm), lambda i: (0, jnp.minimum(i, n_lat - 1)))]
    args = [hs, modr, modr, modr, g_mix.reshape(1, D_MODEL), wl["w_bg"], wl["b_bg"],
            vconv, vconv, vconv, wl["w_dw"], wl["b_dw"], wl["ln_g"], wl["ln_b"], wl["w_pw"],
            wl["b_pw"], ot_lat]
    if with_ctx:
        in_specs.append(pl.BlockSpec((N_HEADS * V_DIM, tm), lambda i: (0, jnp.maximum(i - n_lat, 0))))
        args.append(ot_ctx)
    in_specs.append(_const_spec((N_HEADS * V_DIM, D_MODEL)))
    args.append(wl["w_o"])
    in_specs.append(pl.BlockSpec(
        (tm, FOURIER_WIDTH),
        lambda i: (jnp.where(i < n_lat, i % tpb, 0), jnp.where(i < n_lat, i // tpb, 0))))
    args.append(f_lat)
    if with_ctx:
        in_specs.append(pl.BlockSpec((tm, FOURIER_WIDTH), lambda i: (0, jnp.maximum(i - n_lat, 0))))
        args.append(f_ctx)
    in_specs += [_const_spec((FOURIER_WIDTH, D_MODEL)), _const_spec((1, D_MODEL)),
                 _const_spec((D_MODEL, D_MODEL))]
    args += [wl["w_f"], wl["b_f"], wl["w_out"]]
    return pl.pallas_call(
        functools.partial(_merge_kernel, with_ctx=with_ctx, n_lat=n_lat, tpb=tpb, tm=tm),
        grid=(n_tiles,),
        in_specs=in_specs,
        out_specs=pl.BlockSpec((tm, D_MODEL), lambda i: (i, 0)),
        out_shape=jax.ShapeDtypeStruct((n_tiles * tm, D_MODEL), F32),
        scratch_shapes=[pltpu.VMEM((tm + 2 * HALO, CONV_CH), F32),
                        pltpu.VMEM((SUBLANES, tm + 2 * HALO - SUBLANES, CONV_CH), F32)],
        compiler_params=_params(),
        name="merge",
    )(*args)


def _rotate_half_cols(w):
    wr = w.reshape(w.shape[:-1] + (2, 2, ROPE_AXIS // 2))
    wr = jnp.concatenate([-wr[..., 1:2, :], wr[..., 0:1, :]], axis=-2)
    return wr.reshape(w.shape)


def _layer_weights(l, p):
    o1 = 2 * CONV_CH
    o2 = o1 + Q_LORA
    o3 = o2 + KV_LORA
    o4 = o3 + QK_ROPE
    w_in = p["w_in"][l]
    w_kr = w_in[:, o3:o4]
    zeros = lambda n: jnp.zeros((D_MODEL, n), F32)
    pad_kr = lambda w: jnp.concatenate([zeros(QK_NOPE), w, zeros(HEAD_PAD - QK_NOPE - QK_ROPE)], axis=1)
    w_in_all = jnp.concatenate(
        [w_in[:, :o3], pad_kr(w_kr), pad_kr(_rotate_half_cols(w_kr)), w_in[:, o4:]], axis=1)
    wq = p["w_uq"][l].reshape(Q_LORA, N_HEADS, QK_NOPE + QK_ROPE)
    zq = lambda n: jnp.zeros((Q_LORA, N_HEADS, n), F32)
    w_uq_a = jnp.concatenate([wq, zq(HEAD_PAD - QK_NOPE - QK_ROPE)], axis=-1)
    w_uq_b = jnp.concatenate([zq(QK_NOPE), _rotate_half_cols(wq[..., QK_NOPE:]),
                              zq(HEAD_PAD - QK_NOPE - QK_ROPE)], axis=-1)
    wkv = p["w_ukv"][l].reshape(KV_LORA, N_HEADS, QK_NOPE + V_DIM)
    w_uk = jnp.concatenate([wkv[..., :QK_NOPE], jnp.zeros((KV_LORA, N_HEADS, HEAD_PAD - QK_NOPE), F32)],
                           axis=-1)
    w_uvt = wkv[..., QK_NOPE:].reshape(KV_LORA, N_HEADS * V_DIM).T
    row = lambda v: v.reshape(1, -1)
    return {
        "w_in_all": w_in_all.astype(BF16),
        "g_q": row(p["g_qnorm"][l]), "g_kv": row(p["g_kvnorm"][l]),
        "w_uq_a": w_uq_a.reshape(Q_LORA, N_HEADS * HEAD_PAD).astype(BF16),
        "w_uq_b": w_uq_b.reshape(Q_LORA, N_HEADS * HEAD_PAD).astype(BF16),
        "w_uk": w_uk.reshape(KV_LORA, N_HEADS * HEAD_PAD).astype(BF16),
        "w_uvt": w_uvt.astype(BF16),
        "w_bg": p["w_bgate"][l].astype(BF16), "b_bg": row(p["b_bgate"][l]),
        "w_dw": jnp.concatenate([p["w_dw"][l], jnp.zeros((1, CONV_CH), F32)], axis=0),
        "b_dw": row(p["b_dw"][l]), "ln_g": row(p["ln_g_conv"][l]), "ln_b": row(p["ln_b_conv"][l]),
        "w_pw": p["w_pw_conv"][l].astype(BF16), "b_pw": row(p["b_pw_conv"][l]),
        "w_o": p["w_o_mla"][l].astype(BF16),
        "w_f": p["w_fourier"][l].astype(BF16), "b_f": row(p["b_fourier"][l]),
        "w_out": p["w_out"][l].astype(BF16),
    }


def _tables(seq, lc):
    rows = seq // GRID_W
    row = jnp.broadcast_to(jnp.arange(rows, dtype=F32)[:, None], (rows, GRID_W)).reshape(-1)
    col = jnp.broadcast_to(jnp.arange(GRID_W, dtype=F32)[None, :], (rows, GRID_W)).reshape(-1)
    inv = 1.0 / (ROPE_BASE ** (jnp.arange(ROPE_AXIS // 2, dtype=F32) * 2.0 / ROPE_AXIS))
    ar = row[:, None] * inv
    ac = col[:, None] * inv
    ang = jnp.concatenate([ar, ar, ac, ac], axis=-1)
    tail = HEAD_PAD - QK_NOPE - QK_ROPE
    cos = jnp.concatenate([jnp.ones((seq, QK_NOPE), F32), jnp.cos(ang), jnp.ones((seq, tail), F32)], axis=1)
    sin = jnp.concatenate([jnp.zeros((seq, QK_NOPE), F32), jnp.sin(ang), jnp.zeros((seq, tail), F32)], axis=1)
    cos = jnp.concatenate([cos, jnp.ones((lc, HEAD_PAD), F32)], axis=0)
    sin = jnp.concatenate([sin, jnp.zeros((lc, HEAD_PAD), F32)], axis=0)
    c = np.arange(FOURIER_GROUP_CH)
    a = 2.0 * np.pi * ((c[:, None] * c[None, :]) % FOURIER_GROUP_CH) / FOURIER_GROUP_CH
    dftc = np.concatenate([np.cos(a), np.sin(a)], axis=1) / np.sqrt(FOURIER_GROUP_CH)
    return {"cos": cos, "sin": sin, "dftc": jnp.asarray(dftc, F32).astype(BF16)}


def kernel(x, c, ctx, c_ctx, w_ada, b_ada, g_ffn1, w1_ffn1, w3_ffn1, w2_ffn1, g_mix, w_in, w_dw, b_dw, ln_g_conv, ln_b_conv, w_pw_conv, b_pw_conv, g_qnorm, w_uq, g_kvnorm, w_ukv, w_o_mla, w_fourier, b_fourier, w_bgate, b_bgate, w_out, g_ffn2, w1_ffn2, w3_ffn2, w2_ffn2, g_final):
    bsz, seq, _ = x.shape
    lc = ctx.shape[1]
    depth = w_ada.shape[0]
    tm = lc
    tf = 2 * tm
    assert bsz + 1 <= MOD_ROWS and seq % tf == 0 and (bsz * lc) % tf == 0 and tm % HALO == 0
    n_lat_rows = bsz * seq
    p = dict(w_in=w_in, w_dw=w_dw, b_dw=b_dw, ln_g_conv=ln_g_conv, ln_b_conv=ln_b_conv,
             w_pw_conv=w_pw_conv, b_pw_conv=b_pw_conv, g_qnorm=g_qnorm, w_uq=w_uq,
             g_kvnorm=g_kvnorm, w_ukv=w_ukv, w_o_mla=w_o_mla, w_fourier=w_fourier,
             b_fourier=b_fourier, w_bgate=w_bgate, b_bgate=b_bgate, w_out=w_out)

    def make_row_fn(tile):
        n_lat_tiles, tiles_per_seq = n_lat_rows // tile, seq // tile
        return lambda i: jnp.where(i < n_lat_tiles, i // tiles_per_seq, bsz)

    dims = {"B": bsz, "S": seq, "Lc": lc, "tm": tm, "row_fn": make_row_fn(tm),
            "tq": min(2048, seq), "tk": min(256, seq)}
    ffn_row_fn = make_row_fn(tf)
    n_ffn_all = (n_lat_rows + bsz * lc) // tf
    n_ffn_lat = n_lat_rows // tf

    cs = jnp.concatenate([c, c_ctx[None, :], jnp.zeros((MOD_ROWS - bsz - 1, D_MODEL), F32)], axis=0)
    mod = _mod_call(cs, w_ada, b_ada)
    modr = mod.reshape(depth, MOD_ROWS, N_ADA, D_MODEL).transpose(0, 2, 1, 3)
    modr = modr.reshape(depth * N_ADA * MOD_ROWS, 1, D_MODEL)
    tabs = _tables(seq, lc)

    hs = jnp.concatenate([x.reshape(n_lat_rows, D_MODEL), ctx.reshape(bsz * lc, D_MODEL)], axis=0)
    for l in range(depth):
        last = l == depth - 1
        wl = _layer_weights(l, p)
        hs = _ffn_call(hs, modr, l, 0, g_ffn1[l], w1_ffn1[l].astype(BF16), w3_ffn1[l].astype(BF16),
                       w2_ffn1[l].astype(BF16), n_tiles=n_ffn_all, tm=tf, row_fn=ffn_row_fn)
        q, k, vt, vconv, gc, gs = _mix_in_call(hs, modr, l, g_mix[l], wl, tabs, dims)
        ot_lat = _attn_call(q, k, vt, dims, latent_queries=True)
        f_lat = _fourier_lat_call(gc, gs, dims)
        if last:
            ot_ctx = f_ctx = None
        else:
            ot_ctx = _attn_call(q, k, vt, dims, latent_queries=False)
            f_ctx = _fourier_ctx_call(gc, gs, dims)
        hs = _merge_call(hs, modr, l, g_mix[l], wl, vconv, ot_lat, ot_ctx, f_lat, f_ctx, dims,
                         with_ctx=not last)
        hs = _ffn_call(hs, modr, l, 6, g_ffn2[l], w1_ffn2[l].astype(BF16), w3_ffn2[l].astype(BF16),
                       w2_ffn2[l].astype(BF16), n_tiles=n_ffn_lat if last else n_ffn_all, tm=tf,
                       row_fn=ffn_row_fn, final_g=g_final if last else None)
    return hs.reshape(bsz, seq, D_MODEL)
```

```python
import functools

import numpy as np
import jax
import jax.numpy as jnp
from jax import lax
from jax.experimental import pallas as pl
from jax.experimental.pallas import tpu as pltpu

D_MODEL = 1024
D_FF = 2816
N_ADA = 9
CONV_CH = 384
CONV_K = 31
N_HEADS = 8
Q_LORA = 384
KV_LORA = 256
QK_NOPE = 64
QK_ROPE = 32
V_DIM = 64
ROPE_AXIS = QK_ROPE // 2
ROPE_BASE = 10000.0
GRID_W = 64
EPS = 1e-6
HALF = 0.5
ATTN_SCALE = (QK_NOPE + QK_ROPE) ** -0.5
FOURIER_GROUPS = 4
FOURIER_GROUP_CH = 128
FOURIER_WIDTH = FOURIER_GROUPS * FOURIER_GROUP_CH
N_BRANCH = 3

LANES = 128
SUBLANES = 8
HEAD_PAD = 128
MOD_ROWS = 8
HALO = 16
FFT_GROUP = 8
DENOM_ROWS = 16
QCOL = 256
Q_SCALE = ATTN_SCALE * float(np.log2(np.e))
VMEM_LIMIT = 56 * 1024 * 1024
W_IN_ALL = 2 * CONV_CH + Q_LORA + KV_LORA + 2 * HEAD_PAD + FOURIER_WIDTH

BF16 = jnp.bfloat16
F32 = jnp.float32
NT_DIMS = (((1,), (1,)), ((), ()))
TN_DIMS = (((0,), (0,)), ((), ()))


def _sigmoid(x):
    return 1.0 / (1.0 + jnp.exp(-x))


def _rms(x, g):
    return x * lax.rsqrt(jnp.mean(x * x, axis=-1, keepdims=True) + EPS) * g


def _dot(a, b):
    return jnp.dot(a, b, preferred_element_type=F32)


def _params(n_axes=1):
    return pltpu.CompilerParams(dimension_semantics=("arbitrary",) * n_axes,
                                vmem_limit_bytes=VMEM_LIMIT)


def _const_spec(shape):
    zeros = (0,) * len(shape)
    return pl.BlockSpec(shape, lambda *_: zeros, pipeline_mode=pl.Buffered(1))


def _layer_spec(shape, layer):
    idx = (layer,) + (0,) * len(shape)
    return pl.BlockSpec((None,) + tuple(shape), lambda *_: idx, pipeline_mode=pl.Buffered(1))


def _mod_kernel(cs_ref, w_ref, b_ref, o_ref):
    cs = cs_ref[...]
    a = (cs * _sigmoid(cs)).astype(BF16)
    o_ref[...] = _dot(a, w_ref[...].astype(BF16)) + b_ref[...]


def _mod_call(cs, w_ada, b_ada):
    n_layers = w_ada.shape[0]
    return pl.pallas_call(
        _mod_kernel,
        grid=(n_layers, N_ADA),
        in_specs=[pl.BlockSpec((MOD_ROWS, D_MODEL), lambda l, k: (0, 0)),
                  pl.BlockSpec((None, D_MODEL, D_MODEL), lambda l, k: (l, 0, k)),
                  pl.BlockSpec((None, 1, D_MODEL), lambda l, k: (l, 0, k))],
        out_specs=pl.BlockSpec((None, MOD_ROWS, D_MODEL), lambda l, k: (l, 0, k)),
        out_shape=jax.ShapeDtypeStruct((n_layers, MOD_ROWS, N_ADA * D_MODEL), F32),
        compiler_params=_params(2),
        name="ada_map",
    )(cs, w_ada, b_ada.reshape(n_layers, 1, N_ADA * D_MODEL))


def _mod_spec(layer, k, row_fn):
    base = (layer * N_ADA + k) * MOD_ROWS
    return pl.BlockSpec((None, 1, D_MODEL), lambda i: (base + row_fn(i), 0, 0))


def _ffn_kernel(*refs, chunks, final, n_first):
    refs = list(refs)
    o_ref = refs.pop()
    gf_ref = refs.pop() if final else None
    h2_ref = refs.pop(1) if n_first is not None else None
    h_ref, sh_ref, sc_ref, gt_ref, g_ref, w1_ref, w3_ref, w2_ref = refs
    h = h_ref[...]
    if n_first is not None:
        h = jnp.where(pl.program_id(0) < n_first, h, h2_ref[...])
    xb = (_rms(h, g_ref[...]) * (1.0 + sc_ref[...]) + sh_ref[...]).astype(BF16)
    acc = None
    off = 0
    for c in chunks:
        a = _dot(xb, w1_ref[:, off:off + c])
        b = _dot(xb, w3_ref[:, off:off + c])
        act = (a * _sigmoid(a) * b).astype(BF16)
        y = _dot(act, w2_ref[off:off + c, :])
        acc = y if acc is None else acc + y
        off += c
    out = h + (HALF * gt_ref[...]) * acc
    if final:
        out = _rms(out, gf_ref[...])
    o_ref[...] = out


def _ffn_call(hs, modr, layer, kbase, g, w1, w3, w2, *, n_tiles, tm, row_fn, final_g=None, hs2=None):
    chunks = (512,) * (D_FF // 512) + ((D_FF % 512,) if D_FF % 512 else ())
    final = final_g is not None
    n_first = None if hs2 is None else hs.shape[0] // tm
    if hs2 is None:
        in_specs = [pl.BlockSpec((tm, D_MODEL), lambda i: (i, 0))]
        args = [hs]
    else:
        in_specs = [pl.BlockSpec((tm, D_MODEL), lambda i: (jnp.minimum(i, n_first - 1), 0)),
                    pl.BlockSpec((tm, D_MODEL), lambda i: (jnp.maximum(i - n_first, 0), 0))]
        args = [hs, hs2]
    in_specs += [_mod_spec(layer, kbase, row_fn), _mod_spec(layer, kbase + 1, row_fn),
                 _mod_spec(layer, kbase + 2, row_fn),
                 _layer_spec((1, D_MODEL), layer),
                 _layer_spec((D_MODEL, D_FF), layer), _layer_spec((D_MODEL, D_FF), layer),
                 _layer_spec((D_FF, D_MODEL), layer)]
    args += [modr, modr, modr, g, w1, w3, w2]
    if final:
        in_specs.append(_const_spec((1, D_MODEL)))
        args.append(final_g.reshape(1, D_MODEL))
    return pl.pallas_call(
        functools.partial(_ffn_kernel, chunks=chunks, final=final, n_first=n_first),
        grid=(n_tiles,),
        in_specs=in_specs,
        out_specs=pl.BlockSpec((tm, D_MODEL), lambda i: (i, 0)),
        out_shape=jax.ShapeDtypeStruct((n_tiles * tm, D_MODEL), F32),
        compiler_params=_params(),
        name="ffn",
    )(*args)


def _mix_in_kernel(h_ref, sh_ref, sc_ref, g_ref, win_ref, gq_ref, wqa_ref, wqb_ref, gkv_ref,
                   wuk_ref, wuvt_ref, cos_ref, sin_ref, dftc_ref,
                   q_ref, k_ref, vt_ref, vc_ref, gc_ref, gs_ref):
    h = h_ref[...]
    u = (_rms(h, g_ref[...]) * (1.0 + sc_ref[...]) + sh_ref[...]).astype(BF16)
    z = _dot(u, win_ref[...])
    o_cq = 2 * CONV_CH
    o_ckv = o_cq + Q_LORA
    o_kr = o_ckv + KV_LORA
    o_four = o_kr + 2 * HEAD_PAD
    vc_ref[...] = z[:, :CONV_CH] * _sigmoid(z[:, CONV_CH:o_cq])
    cos = cos_ref[...]
    sin = sin_ref[...]
    cqn = _rms(z[:, o_cq:o_ckv], gq_ref[...]).astype(BF16)
    qa = _dot(cqn, wqa_ref[...])
    qb = _dot(cqn, wqb_ref[...])
    for hh in range(N_HEADS):
        s = slice(hh * HEAD_PAD, (hh + 1) * HEAD_PAD)
        q_ref[hh] = ((qa[:, s] * cos + qb[:, s] * sin) * Q_SCALE).astype(BF16)
    ckvn = _rms(z[:, o_ckv:o_kr], gkv_ref[...]).astype(BF16)
    kn = _dot(ckvn, wuk_ref[...])
    kr = z[:, o_kr:o_kr + HEAD_PAD] * cos + z[:, o_kr + HEAD_PAD:o_four] * sin
    for hh in range(N_HEADS):
        s = slice(hh * HEAD_PAD, (hh + 1) * HEAD_PAD)
        k_ref[hh] = (kn[:, s] + kr).astype(BF16)
    vt_ref[...] = lax.dot_general(wuvt_ref[...], ckvn, NT_DIMS,
                                  preferred_element_type=F32).astype(BF16)
    zf = z[:, o_four:].astype(BF16)
    for gi in range(FOURIER_GROUPS):
        s = slice(gi * FOURIER_GROUP_CH, (gi + 1) * FOURIER_GROUP_CH)
        r = _dot(zf[:, s], dftc_ref[...])
        gc_ref[:, s] = r[:, :FOURIER_GROUP_CH]
        gs_ref[:, s] = r[:, FOURIER_GROUP_CH:]


def _mix_in_call(hs, modr, layer, g_mix, wl, tabs, dims):
    bsz, seq, lc, tm = dims["B"], dims["S"], dims["Lc"], dims["tm"]
    n_rows = bsz * (seq + lc)
    tpb = seq // tm
    n_lat = bsz * tpb
    n_tiles = n_rows // tm
    row_fn = dims["row_fn"]

    def tab_idx(i):
        return (jnp.where(i < n_lat, i % tpb, tpb), 0)

    def g_idx(i):
        return (jnp.where(i < n_lat, i % tpb, tpb), jnp.where(i < n_lat, i // tpb, i - n_lat))

    in_specs = [pl.BlockSpec((tm, D_MODEL), lambda i: (i, 0)),
                _mod_spec(layer, 3, row_fn), _mod_spec(layer, 4, row_fn),
                _layer_spec((1, D_MODEL), layer),
                _layer_spec((D_MODEL, W_IN_ALL), layer),
                _layer_spec((1, Q_LORA), layer),
                _layer_spec((Q_LORA, N_HEADS * HEAD_PAD), layer),
                _layer_spec((Q_LORA, N_HEADS * HEAD_PAD), layer),
                _layer_spec((1, KV_LORA), layer),
                _layer_spec((KV_LORA, N_HEADS * HEAD_PAD), layer),
                _layer_spec((N_HEADS * V_DIM, KV_LORA), layer),
                pl.BlockSpec((tm, HEAD_PAD), tab_idx), pl.BlockSpec((tm, HEAD_PAD), tab_idx),
                _const_spec((FOURIER_GROUP_CH, 2 * FOURIER_GROUP_CH))]
    out_specs = [pl.BlockSpec((N_HEADS, tm, HEAD_PAD), lambda i: (0, i, 0)),
                 pl.BlockSpec((N_HEADS, tm, HEAD_PAD), lambda i: (0, i, 0)),
                 pl.BlockSpec((N_HEADS * V_DIM, tm), lambda i: (0, i)),
                 pl.BlockSpec((tm, CONV_CH), lambda i: (i, 0)),
                 pl.BlockSpec((tm, FOURIER_WIDTH), g_idx),
                 pl.BlockSpec((tm, FOURIER_WIDTH), g_idx)]
    out_shape = [jax.ShapeDtypeStruct((N_HEADS, n_rows, HEAD_PAD), BF16),
                 jax.ShapeDtypeStruct((N_HEADS, n_rows, HEAD_PAD), BF16),
                 jax.ShapeDtypeStruct((N_HEADS * V_DIM, n_rows), BF16),
                 jax.ShapeDtypeStruct((n_rows, CONV_CH), F32),
                 jax.ShapeDtypeStruct((seq + lc, bsz * FOURIER_WIDTH), F32),
                 jax.ShapeDtypeStruct((seq + lc, bsz * FOURIER_WIDTH), F32)]
    return pl.pallas_call(
        _mix_in_kernel,
        grid=(n_tiles,),
        in_specs=in_specs,
        out_specs=out_specs,
        out_shape=out_shape,
        compiler_params=_params(),
        name="mix_in",
    )(hs, modr, modr, g_mix, wl["w_in_all"], wl["g_q"], wl["w_uq_a"],
      wl["w_uq_b"], wl["g_kv"], wl["w_uk"], wl["w_uvt"], tabs["cos"], tabs["sin"], tabs["dftc"])


def _attn_kernel(*refs, with_lat, lc, seq, tk):
    if with_lat:
        q_ref, kc_ref, vc_ref, kl_ref, vl_ref, o_ref = refs
    else:
        q_ref, kc_ref, vc_ref, o_ref = refs
    segs = [(kc_ref, vc_ref, 0, lc)]
    if with_lat:
        segs += [(kl_ref, vl_ref, s0, tk) for s0 in range(0, seq, tk)]
    n_blk = q_ref.shape[0] // QCOL
    cols = [slice(i * QCOL, (i + 1) * QCOL) for i in range(n_blk)]
    qs = [q_ref[c, :] for c in cols]

    def scores(j, i):
        kref, _, s0, size = segs[j]
        return lax.dot_general(kref[s0:s0 + size, :], qs[i], NT_DIMS,
                               preferred_element_type=F32)

    m = [None] * n_blk
    acc = [None] * n_blk
    s_cur = [scores(0, i) for i in range(n_blk)]
    for j, (_, vref, s0, size) in enumerate(segs):
        v_aug = jnp.concatenate([vref[:, s0:s0 + size], jnp.ones((DENOM_ROWS, size), BF16)], axis=0)
        s_next = [None] * n_blk
        for i in range(n_blk):
            if j + 1 < len(segs):
                s_next[i] = scores(j + 1, i)
            smax = jnp.max(s_cur[i], axis=0, keepdims=True)
            m_new = smax if m[i] is None else jnp.maximum(m[i], smax)
            p = jnp.exp2(s_cur[i] - m_new).astype(BF16)
            pv = _dot(v_aug, p)
            acc[i] = pv if m[i] is None else jnp.exp2(m[i] - m_new) * acc[i] + pv
            m[i] = m_new
        s_cur = s_next
    for i in range(n_blk):
        o_ref[:, cols[i]] = (acc[i][:V_DIM] / acc[i][V_DIM:V_DIM + 1]).astype(BF16)


def _attn_call(q, k, vt, dims, *, latent_queries):
    bsz, seq, lc = dims["B"], dims["S"], dims["Lc"]
    ctx_blk0 = (bsz * seq) // lc
    if latent_queries:
        tq = dims["tq"]
        nq = seq // tq
        q_idx = lambda b, h, i: (h, b * nq + i, 0)
        o_idx = lambda b, h, i: (h, b * nq + i)
        n_q = bsz * seq
    else:
        tq, nq = lc, 1
        q_idx = lambda b, h, i: (h, ctx_blk0 + b, 0)
        o_idx = lambda b, h, i: (h, b)
        n_q = bsz * lc
    in_specs = [pl.BlockSpec((None, tq, HEAD_PAD), q_idx),
                pl.BlockSpec((None, lc, HEAD_PAD), lambda b, h, i: (h, ctx_blk0 + b, 0)),
                pl.BlockSpec((V_DIM, lc), lambda b, h, i: (h, ctx_blk0 + b))]
    args = [q, k, vt]
    if latent_queries:
        in_specs += [pl.BlockSpec((None, seq, HEAD_PAD), lambda b, h, i: (h, b, 0)),
                     pl.BlockSpec((V_DIM, seq), lambda b, h, i: (h, b))]
        args += [k, vt]
    return pl.pallas_call(
        functools.partial(_attn_kernel, with_lat=latent_queries, lc=lc, seq=seq, tk=dims["tk"]),
        grid=(bsz, N_HEADS, nq),
        in_specs=in_specs,
        out_specs=pl.BlockSpec((V_DIM, tq), o_idx),
        out_shape=jax.ShapeDtypeStruct((N_HEADS * V_DIM, n_q), BF16),
        compiler_params=_params(3),
        name="attn_lat" if latent_queries else "attn_ctx",
    )(*args)


def _fft1_kernel(gc_ref, gs_ref, m_ref, zr_ref, zi_ref):
    n1 = gc_ref.shape[0]
    for j in range(FFT_GROUP):
        x = jnp.concatenate([gc_ref[:, j, :], gs_ref[:, j, :]], axis=0).astype(BF16)
        z = _dot(m_ref[j], x)
        zr_ref[:, j, :] = z[:n1]
        zi_ref[:, j, :] = z[n1:]


def _fft2_kernel(zr_ref, zi_ref, w_ref, f_ref):
    for j in range(FFT_GROUP):
        z = jnp.concatenate([zr_ref[j], zi_ref[j]], axis=0).astype(BF16)
        f_ref[:, j, :] = _dot(w_ref[...], z)


def _fft_tables(seq):
    n1 = int(round(seq ** 0.5))
    n2 = seq // n1
    assert n1 * n2 == seq and n1 % FFT_GROUP == 0 and n2 % FFT_GROUP == 0
    k1 = np.arange(n1)[None, :, None]
    t1 = np.arange(n1)[None, None, :]
    t2 = np.arange(n2)[:, None, None]
    ang = -2.0 * np.pi * (((t1 * k1) % n1) / n1 + ((t2 * k1) % seq) / seq)
    ar = np.cos(ang) / np.sqrt(n1)
    ai = np.sin(ang) / np.sqrt(n1)
    m1 = np.concatenate([np.concatenate([ar, ai], axis=2),
                         np.concatenate([ai, -ar], axis=2)], axis=1)
    k2 = np.arange(n2)[:, None]
    tt = np.arange(n2)[None, :]
    a2 = 2.0 * np.pi * ((k2 * tt) % n2) / n2
    w2 = np.concatenate([np.cos(a2), np.sin(a2)], axis=1) / np.sqrt(n2)
    return n1, n2, jnp.asarray(m1, F32).astype(BF16), jnp.asarray(w2, F32).astype(BF16)


def _fourier_lat_call(gc, gs, dims):
    bsz, seq = dims["B"], dims["S"]
    n1, n2, m1, w2 = _fft_tables(seq)
    ncol = bsz * FOURIER_WIDTH
    assert gc.shape[0] % n2 == 0
    gc3 = gc.reshape(gc.shape[0] // n2, n2, ncol)
    gs3 = gs.reshape(gs.shape[0] // n2, n2, ncol)
    zshape = jax.ShapeDtypeStruct((n1, n2, ncol), F32)
    strided = pl.BlockSpec((n1, FFT_GROUP, ncol), lambda j: (0, j, 0))
    zr, zi = pl.pallas_call(
        _fft1_kernel,
        grid=(n2 // FFT_GROUP,),
        in_specs=[strided, strided,
                  pl.BlockSpec((FFT_GROUP, 2 * n1, 2 * n1), lambda j: (j, 0, 0))],
        out_specs=[strided, strided],
        out_shape=[zshape, zshape],
        compiler_params=_params(),
        name="fft_stage1",
    )(gc3, gs3, m1)
    slab = pl.BlockSpec((FFT_GROUP, n2, ncol), lambda j: (j, 0, 0))
    f3 = pl.pallas_call(
        _fft2_kernel,
        grid=(n1 // FFT_GROUP,),
        in_specs=[slab, slab, _const_spec((n2, 2 * n2))],
        out_specs=pl.BlockSpec((n2, FFT_GROUP, ncol), lambda j: (0, j, 0)),
        out_shape=jax.ShapeDtypeStruct((n2, n1, ncol), F32),
        compiler_params=_params(),
        name="fft_stage2",
    )(zr, zi, w2)
    return f3.reshape(seq, ncol)


def _dft_ctx_kernel(gc_ref, gs_ref, c_ref, s_ref, f_ref):
    f_ref[...] = (_dot(c_ref[...], gc_ref[...].astype(BF16))
                  - _dot(s_ref[...], gs_ref[...].astype(BF16)))


def _fourier_ctx_call(gc, gs, dims):
    bsz, lc = dims["B"], dims["Lc"]
    ncol = bsz * FOURIER_WIDTH
    kt = (np.arange(lc)[:, None] * np.arange(lc)[None, :]) % lc
    ang = 2.0 * np.pi * kt / lc
    cm = jnp.asarray(np.cos(ang) / np.sqrt(lc), F32).astype(BF16)
    sm = jnp.asarray(np.sin(ang) / np.sqrt(lc), F32).astype(BF16)
    ctx_cols = pl.BlockSpec((lc, ncol), lambda j: (dims["S"] // lc, 0))
    return pl.pallas_call(
        _dft_ctx_kernel,
        grid=(1,),
        in_specs=[ctx_cols, ctx_cols, _const_spec((lc, lc)), _const_spec((lc, lc))],
        out_specs=pl.BlockSpec((lc, ncol), lambda j: (0, 0)),
        out_shape=jax.ShapeDtypeStruct((lc, ncol), F32),
        compiler_params=_params(),
        name="dft_ctx",
    )(gc, gs, cm, sm)


def _merge_kernel(*refs, with_ctx, n_lat, tpb, tm):
    refs = list(refs)
    xs_ref = refs.pop()
    xpad_ref = refs.pop()
    o_ref = refs.pop()
    it = iter(refs)
    h_ref, sh_ref, sc_ref, gt_ref, g_ref, wbg_ref, bbg_ref = [next(it) for _ in range(7)]
    vc_ref, vp_ref, vn_ref, wdw_ref, bdw_ref, lng_ref, lnb_ref, wpw_ref, bpw_ref = [
        next(it) for _ in range(9)]
    otl_ref = next(it)
    otc_ref = next(it) if with_ctx else None
    wo_ref = next(it)
    fl_ref = next(it)
    fc_ref = next(it) if with_ctx else None
    wf_ref, bf_ref, wout_ref = [next(it) for _ in range(3)]

    i = pl.program_id(0)
    pos = i % tpb
    first = pos == 0
    last = pos == tpb - 1
    if with_ctx:
        is_lat = i < n_lat
        first = jnp.logical_or(first, jnp.logical_not(is_lat))
        last = jnp.logical_or(last, jnp.logical_not(is_lat))

    h = h_ref[...]
    u = (_rms(h, g_ref[...]) * (1.0 + sc_ref[...]) + sh_ref[...]).astype(BF16)

    xpad_ref[0:HALO, :] = jnp.where(first, 0.0, vp_ref[...])
    xpad_ref[HALO:HALO + tm, :] = vc_ref[...]
    xpad_ref[HALO + tm:, :] = jnp.where(last, 0.0, vn_ref[...])
    base = HALO - CONV_K // 2
    n_shift_rows = xs_ref.shape[1]
    for r in range(SUBLANES):
        xs_ref[r] = xpad_ref[pl.ds(r, n_shift_rows), :]

    def conv_lanes(c0):
        acc = None
        for kk in range(CONV_K):
            off = base + kk
            a0 = (off // SUBLANES) * SUBLANES
            t = (xs_ref[off % SUBLANES, a0:a0 + tm, c0:c0 + LANES]
                 * wdw_ref[kk:kk + 1, c0:c0 + LANES])
            acc = t if acc is None else acc + t
        return acc + bdw_ref[:, c0:c0 + LANES]

    ot = otl_ref[...]
    fo = fl_ref[...]
    if with_ctx:
        ot = jnp.where(is_lat, ot, otc_ref[...])
        fo = jnp.where(is_lat, fo, fc_ref[...])
    gate_pre = []
    conv_parts = []
    for br in range(N_BRANCH):
        cols = slice(br * D_MODEL, (br + 1) * D_MODEL)
        gate_pre.append(_dot(u, wbg_ref[:, cols]) + bbg_ref[:, cols])
        conv_parts.append(conv_lanes(br * LANES))
    y_mla = lax.dot_general(ot, wo_ref[...], TN_DIMS, preferred_element_type=F32)
    y_four = _dot(fo.astype(BF16), wf_ref[...]) + bf_ref[...]
    conv = jnp.concatenate(conv_parts, axis=-1)
    mu = jnp.mean(conv, axis=-1, keepdims=True)
    cen = conv - mu
    var = jnp.mean(cen * cen, axis=-1, keepdims=True)
    ln = cen * lax.rsqrt(var + EPS) * lng_ref[...] + lnb_ref[...]
    y_conv = _dot((ln * _sigmoid(ln)).astype(BF16), wpw_ref[...]) + bpw_ref[...]

    mix = (_sigmoid(gate_pre[0]) * y_conv + _sigmoid(gate_pre[1]) * y_mla
           + _sigmoid(gate_pre[2]) * y_four)
    y = _dot(mix.astype(BF16), wout_ref[...])
    o_ref[...] = h + gt_ref[...] * y


def _merge_call(hs, modr, layer, g_mix, wl, vconv, ot_lat, ot_ctx, f_lat, f_ctx, dims, *, with_ctx):
    bsz, seq, lc, tm = dims["B"], dims["S"], dims["Lc"], dims["tm"]
    tpb = seq // tm
    n_lat = bsz * tpb
    n_tiles = n_lat + (bsz * lc // tm if with_ctx else 0)
    row_fn = dims["row_fn"]
    hpt = tm // HALO
    n_halo = vconv.shape[0] // HALO

    in_specs = [pl.BlockSpec((tm, D_MODEL), lambda i: (i, 0)),
                _mod_spec(layer, 3, row_fn), _mod_spec(layer, 4, row_fn),
                _mod_spec(layer, 5, row_fn),
                _layer_spec((1, D_MODEL), layer),
                _layer_spec((D_MODEL, N_BRANCH * D_MODEL), layer),
                _layer_spec((1, N_BRANCH * D_MODEL), layer),
                pl.BlockSpec((tm, CONV_CH), lambda i: (i, 0)),
                pl.BlockSpec((HALO, CONV_CH), lambda i: (jnp.maximum(i * hpt - 1, 0), 0)),
                pl.BlockSpec((HALO, CONV_CH), lambda i: (jnp.minimum((i + 1) * hpt, n_halo - 1), 0)),
                _layer_spec((CONV_K + 1, CONV_CH), layer),
                _layer_spec((1, CONV_CH), layer), _layer_spec((1, CONV_CH), layer),
                _layer_spec((1, CONV_CH), layer),
                _layer_spec((CONV_CH, D_MODEL), layer), _layer_spec((1, D_MODEL), layer),
                pl.BlockSpec((N_HEADS * V_DIM, tm), lambda i: (0, jnp.minimum(i, n_lat - 1)))]
    args = [hs, modr, modr, modr, g_mix, wl["w_bg"], wl["b_bg"],
            vconv, vconv, vconv, wl["w_dw"], wl["b_dw"], wl["ln_g"], wl["ln_b"], wl["w_pw"],
            wl["b_pw"], ot_lat]
    if with_ctx:
        in_specs.append(pl.BlockSpec((N_HEADS * V_DIM, tm), lambda i: (0, jnp.maximum(i - n_lat, 0))))
        args.append(ot_ctx)
    in_specs.append(_layer_spec((N_HEADS * V_DIM, D_MODEL), layer))
    args.append(wl["w_o"])
    in_specs.append(pl.BlockSpec(
        (tm, FOURIER_WIDTH),
        lambda i: (jnp.where(i < n_lat, i % tpb, 0), jnp.where(i < n_lat, i // tpb, 0))))
    args.append(f_lat)
    if with_ctx:
        in_specs.append(pl.BlockSpec((tm, FOURIER_WIDTH), lambda i: (0, jnp.maximum(i - n_lat, 0))))
        args.append(f_ctx)
    in_specs += [_layer_spec((FOURIER_WIDTH, D_MODEL), layer), _layer_spec((1, D_MODEL), layer),
                 _layer_spec((D_MODEL, D_MODEL), layer)]
    args += [wl["w_f"], wl["b_f"], wl["w_out"]]
    return pl.pallas_call(
        functools.partial(_merge_kernel, with_ctx=with_ctx, n_lat=n_lat, tpb=tpb, tm=tm),
        grid=(n_tiles,),
        in_specs=in_specs,
        out_specs=pl.BlockSpec((tm, D_MODEL), lambda i: (i, 0)),
        out_shape=jax.ShapeDtypeStruct((n_tiles * tm, D_MODEL), F32),
        scratch_shapes=[pltpu.VMEM((tm + 2 * HALO, CONV_CH), F32),
                        pltpu.VMEM((SUBLANES, tm + 2 * HALO - SUBLANES, CONV_CH), F32)],
        compiler_params=_params(),
        name="merge",
    )(*args)


def _rotate_half_cols(w):
    wr = w.reshape(w.shape[:-1] + (2, 2, ROPE_AXIS // 2))
    wr = jnp.concatenate([-wr[..., 1:2, :], wr[..., 0:1, :]], axis=-2)
    return wr.reshape(w.shape)


def _stack_weights(p):
    o1 = 2 * CONV_CH
    o2 = o1 + Q_LORA
    o3 = o2 + KV_LORA
    o4 = o3 + QK_ROPE
    w_in = p["w_in"]
    depth = w_in.shape[0]
    w_kr = w_in[:, :, o3:o4]
    zeros = lambda *shape: jnp.zeros((depth,) + shape, F32)
    tail = HEAD_PAD - QK_NOPE - QK_ROPE
    pad_kr = lambda w: jnp.concatenate([zeros(D_MODEL, QK_NOPE), w, zeros(D_MODEL, tail)], axis=-1)
    w_in_all = jnp.concatenate(
        [w_in[:, :, :o3], pad_kr(w_kr), pad_kr(_rotate_half_cols(w_kr)), w_in[:, :, o4:]], axis=-1)
    wq = p["w_uq"].reshape(depth, Q_LORA, N_HEADS, QK_NOPE + QK_ROPE)
    w_uq_a = jnp.concatenate([wq, zeros(Q_LORA, N_HEADS, tail)], axis=-1)
    w_uq_b = jnp.concatenate([zeros(Q_LORA, N_HEADS, QK_NOPE), _rotate_half_cols(wq[..., QK_NOPE:]),
                              zeros(Q_LORA, N_HEADS, tail)], axis=-1)
    wkv = p["w_ukv"].reshape(depth, KV_LORA, N_HEADS, QK_NOPE + V_DIM)
    w_uk = jnp.concatenate([wkv[..., :QK_NOPE], zeros(KV_LORA, N_HEADS, HEAD_PAD - QK_NOPE)], axis=-1)
    w_uvt = wkv[..., QK_NOPE:].reshape(depth, KV_LORA, N_HEADS * V_DIM).transpose(0, 2, 1)
    row = lambda v: v.reshape(depth, 1, -1)
    return {
        "w_in_all": w_in_all.astype(BF16),
        "g_q": row(p["g_qnorm"]), "g_kv": row(p["g_kvnorm"]),
        "w_uq_a": w_uq_a.reshape(depth, Q_LORA, N_HEADS * HEAD_PAD).astype(BF16),
        "w_uq_b": w_uq_b.reshape(depth, Q_LORA, N_HEADS * HEAD_PAD).astype(BF16),
        "w_uk": w_uk.reshape(depth, KV_LORA, N_HEADS * HEAD_PAD).astype(BF16),
        "w_uvt": w_uvt.astype(BF16),
        "w_bg": p["w_bgate"].astype(BF16), "b_bg": row(p["b_bgate"]),
        "w_dw": jnp.concatenate([p["w_dw"], zeros(1, CONV_CH)], axis=1),
        "b_dw": row(p["b_dw"]), "ln_g": row(p["ln_g_conv"]), "ln_b": row(p["ln_b_conv"]),
        "w_pw": p["w_pw_conv"].astype(BF16), "b_pw": row(p["b_pw_conv"]),
        "w_o": p["w_o_mla"].astype(BF16),
        "w_f": p["w_fourier"].astype(BF16), "b_f": row(p["b_fourier"]),
        "w_out": p["w_out"].astype(BF16),
    }


def _tables(seq, lc):
    rows = seq // GRID_W
    row = jnp.broadcast_to(jnp.arange(rows, dtype=F32)[:, None], (rows, GRID_W)).reshape(-1)
    col = jnp.broadcast_to(jnp.arange(GRID_W, dtype=F32)[None, :], (rows, GRID_W)).reshape(-1)
    inv = 1.0 / (ROPE_BASE ** (jnp.arange(ROPE_AXIS // 2, dtype=F32) * 2.0 / ROPE_AXIS))
    ar = row[:, None] * inv
    ac = col[:, None] * inv
    ang = jnp.concatenate([ar, ar, ac, ac], axis=-1)
    tail = HEAD_PAD - QK_NOPE - QK_ROPE
    cos = jnp.concatenate([jnp.ones((seq, QK_NOPE), F32), jnp.cos(ang), jnp.ones((seq, tail), F32)], axis=1)
    sin = jnp.concatenate([jnp.zeros((seq, QK_NOPE), F32), jnp.sin(ang), jnp.zeros((seq, tail), F32)], axis=1)
    cos = jnp.concatenate([cos, jnp.ones((lc, HEAD_PAD), F32)], axis=0)
    sin = jnp.concatenate([sin, jnp.zeros((lc, HEAD_PAD), F32)], axis=0)
    c = np.arange(FOURIER_GROUP_CH)
    a = 2.0 * np.pi * ((c[:, None] * c[None, :]) % FOURIER_GROUP_CH) / FOURIER_GROUP_CH
    dftc = np.concatenate([np.cos(a), np.sin(a)], axis=1) / np.sqrt(FOURIER_GROUP_CH)
    return {"cos": cos, "sin": sin, "dftc": jnp.asarray(dftc, F32).astype(BF16)}


def kernel(x, c, ctx, c_ctx, w_ada, b_ada, g_ffn1, w1_ffn1, w3_ffn1, w2_ffn1, g_mix, w_in, w_dw, b_dw, ln_g_conv, ln_b_conv, w_pw_conv, b_pw_conv, g_qnorm, w_uq, g_kvnorm, w_ukv, w_o_mla, w_fourier, b_fourier, w_bgate, b_bgate, w_out, g_ffn2, w1_ffn2, w3_ffn2, w2_ffn2, g_final):
    bsz, seq, _ = x.shape
    lc = ctx.shape[1]
    depth = w_ada.shape[0]
    tm = lc
    tf = 2 * tm
    assert bsz + 1 <= MOD_ROWS and seq % tf == 0 and (bsz * lc) % tf == 0 and tm % HALO == 0
    n_lat_rows = bsz * seq
    p = dict(w_in=w_in, w_dw=w_dw, b_dw=b_dw, ln_g_conv=ln_g_conv, ln_b_conv=ln_b_conv,
             w_pw_conv=w_pw_conv, b_pw_conv=b_pw_conv, g_qnorm=g_qnorm, w_uq=w_uq,
             g_kvnorm=g_kvnorm, w_ukv=w_ukv, w_o_mla=w_o_mla, w_fourier=w_fourier,
             b_fourier=b_fourier, w_bgate=w_bgate, b_bgate=b_bgate, w_out=w_out)

    def make_row_fn(tile):
        n_lat_tiles, tiles_per_seq = n_lat_rows // tile, seq // tile
        return lambda i: jnp.where(i < n_lat_tiles, i // tiles_per_seq, bsz)

    dims = {"B": bsz, "S": seq, "Lc": lc, "tm": tm, "row_fn": make_row_fn(tm),
            "tq": min(2048, seq), "tk": min(256, seq)}
    ffn_row_fn = make_row_fn(tf)
    n_ffn_all = (n_lat_rows + bsz * lc) // tf
    n_ffn_lat = n_lat_rows // tf

    cs = jnp.concatenate([c, c_ctx[None, :], jnp.zeros((MOD_ROWS - bsz - 1, D_MODEL), F32)], axis=0)
    mod = _mod_call(cs, w_ada, b_ada)
    modr = mod.reshape(depth, MOD_ROWS, N_ADA, D_MODEL).transpose(0, 2, 1, 3)
    modr = modr.reshape(depth * N_ADA * MOD_ROWS, 1, D_MODEL)
    tabs = _tables(seq, lc)

    wl = _stack_weights(p)
    stack_row = lambda v: v.reshape(depth, 1, D_MODEL)
    g_mix_r = stack_row(g_mix)
    ffn1 = (stack_row(g_ffn1), w1_ffn1.astype(BF16), w3_ffn1.astype(BF16), w2_ffn1.astype(BF16))
    ffn2 = (stack_row(g_ffn2), w1_ffn2.astype(BF16), w3_ffn2.astype(BF16), w2_ffn2.astype(BF16))

    hs = None
    for l in range(depth):
        last = l == depth - 1
        if l == 0:
            hs = _ffn_call(x.reshape(n_lat_rows, D_MODEL), modr, l, 0, *ffn1, n_tiles=n_ffn_all, tm=tf,
                           row_fn=ffn_row_fn, hs2=ctx.reshape(bsz * lc, D_MODEL))
        else:
            hs = _ffn_call(hs, modr, l, 0, *ffn1, n_tiles=n_ffn_all, tm=tf, row_fn=ffn_row_fn)
        q, k, vt, vconv, gc, gs = _mix_in_call(hs, modr, l, g_mix_r, wl, tabs, dims)
        ot_lat = _attn_call(q, k, vt, dims, latent_queries=True)
        f_lat = _fourier_lat_call(gc, gs, dims)
        if last:
            ot_ctx = f_ctx = None
        else:
            ot_ctx = _attn_call(q, k, vt, dims, latent_queries=False)
            f_ctx = _fourier_ctx_call(gc, gs, dims)
        hs = _merge_call(hs, modr, l, g_mix_r, wl, vconv, ot_lat, ot_ctx, f_lat, f_ctx, dims,
                         with_ctx=not last)
        hs = _ffn_call(hs, modr, l, 6, *ffn2, n_tiles=n_ffn_lat if last else n_ffn_all, tm=tf,
                       row_fn=ffn_row_fn, final_g=g_final if last else None)
    return hs.reshape(bsz, seq, D_MODEL)
```

```python
import functools

import numpy as np
import jax
import jax.numpy as jnp
from jax import lax
from jax.experimental import pallas as pl
from jax.experimental.pallas import tpu as pltpu

D_MODEL = 1024
D_FF = 2816
N_ADA = 9
CONV_CH = 384
CONV_K = 31
N_HEADS = 8
Q_LORA = 384
KV_LORA = 256
QK_NOPE = 64
QK_ROPE = 32
V_DIM = 64
ROPE_AXIS = QK_ROPE // 2
ROPE_BASE = 10000.0
GRID_W = 64
EPS = 1e-6
HALF = 0.5
ATTN_SCALE = (QK_NOPE + QK_ROPE) ** -0.5
FOURIER_GROUPS = 4
FOURIER_GROUP_CH = 128
FOURIER_WIDTH = FOURIER_GROUPS * FOURIER_GROUP_CH
N_BRANCH = 3

LANES = 128
SUBLANES = 8
HEAD_PAD = 128
MOD_ROWS = 8
HALO = 16
FFT_GROUP = 8
DENOM_ROWS = 16
QCOL = 256
CONV_ROWS = 64
LOG2E = float(np.log2(np.e))
Q_SCALE = ATTN_SCALE * LOG2E
VMEM_LIMIT = 56 * 1024 * 1024
W_IN_ALL = 2 * CONV_CH + Q_LORA + KV_LORA + 2 * HEAD_PAD + FOURIER_WIDTH

BF16 = jnp.bfloat16
F32 = jnp.float32
NT_DIMS = (((1,), (1,)), ((), ()))
TN_DIMS = (((0,), (0,)), ((), ()))


def _sigmoid(x):
    return 1.0 / (1.0 + jnp.exp2(x * (-LOG2E)))


def _rms(x, g):
    return x * lax.rsqrt(jnp.mean(x * x, axis=-1, keepdims=True) + EPS) * g


def _dot(a, b):
    return jnp.dot(a, b, preferred_element_type=F32)


def _params(n_axes=1):
    return pltpu.CompilerParams(dimension_semantics=("arbitrary",) * n_axes,
                                vmem_limit_bytes=VMEM_LIMIT)


def _const_spec(shape):
    zeros = (0,) * len(shape)
    return pl.BlockSpec(shape, lambda *_: zeros, pipeline_mode=pl.Buffered(1))


def _layer_spec(shape, layer):
    idx = (layer,) + (0,) * len(shape)
    return pl.BlockSpec((None,) + tuple(shape), lambda *_: idx, pipeline_mode=pl.Buffered(1))


def _mod_kernel(cs_ref, w_ref, b_ref, o_ref):
    cs = cs_ref[...]
    a = (cs * _sigmoid(cs)).astype(BF16)
    o_ref[...] = _dot(a, w_ref[...].astype(BF16)) + b_ref[...]


def _mod_call(cs, w_ada, b_ada):
    n_layers = w_ada.shape[0]
    return pl.pallas_call(
        _mod_kernel,
        grid=(n_layers, N_ADA),
        in_specs=[pl.BlockSpec((MOD_ROWS, D_MODEL), lambda l, k: (0, 0)),
                  pl.BlockSpec((None, D_MODEL, D_MODEL), lambda l, k: (l, 0, k)),
                  pl.BlockSpec((None, 1, D_MODEL), lambda l, k: (l, 0, k))],
        out_specs=pl.BlockSpec((None, MOD_ROWS, D_MODEL), lambda l, k: (l, 0, k)),
        out_shape=jax.ShapeDtypeStruct((n_layers, MOD_ROWS, N_ADA * D_MODEL), F32),
        compiler_params=_params(2),
        name="ada_map",
    )(cs, w_ada, b_ada.reshape(n_layers, 1, N_ADA * D_MODEL))


def _mod_spec(layer, k, row_fn):
    base = (layer * N_ADA + k) * MOD_ROWS
    return pl.BlockSpec((None, 1, D_MODEL), lambda i: (base + row_fn(i), 0, 0))


def _ffn_kernel(*refs, chunks, final, n_first):
    refs = list(refs)
    o_ref = refs.pop()
    gf_ref = refs.pop() if final else None
    h2_ref = refs.pop(1) if n_first is not None else None
    h_ref, sh_ref, sc_ref, gt_ref, g_ref, w1_ref, w3_ref, w2_ref = refs
    h = h_ref[...]
    if n_first is not None:
        h = jnp.where(pl.program_id(0) < n_first, h, h2_ref[...])
    xb = (_rms(h, g_ref[...]) * (1.0 + sc_ref[...]) + sh_ref[...]).astype(BF16)
    acc = None
    off = 0
    for c in chunks:
        a = _dot(xb, w1_ref[:, off:off + c].astype(BF16))
        b = _dot(xb, w3_ref[:, off:off + c].astype(BF16))
        act = (a * _sigmoid(a) * b).astype(BF16)
        y = _dot(act, w2_ref[off:off + c, :].astype(BF16))
        acc = y if acc is None else acc + y
        off += c
    out = h + (HALF * gt_ref[...]) * acc
    if final:
        out = _rms(out, gf_ref[...])
    o_ref[...] = out


def _ffn_call(hs, modr, layer, kbase, g, w1, w3, w2, *, n_tiles, tm, row_fn, final_g=None, hs2=None):
    chunks = (512,) * (D_FF // 512) + ((D_FF % 512,) if D_FF % 512 else ())
    final = final_g is not None
    n_first = None if hs2 is None else hs.shape[0] // tm
    if hs2 is None:
        in_specs = [pl.BlockSpec((tm, D_MODEL), lambda i: (i, 0))]
        args = [hs]
    else:
        in_specs = [pl.BlockSpec((tm, D_MODEL), lambda i: (jnp.minimum(i, n_first - 1), 0)),
                    pl.BlockSpec((tm, D_MODEL), lambda i: (jnp.maximum(i - n_first, 0), 0))]
        args = [hs, hs2]
    in_specs += [_mod_spec(layer, kbase, row_fn), _mod_spec(layer, kbase + 1, row_fn),
                 _mod_spec(layer, kbase + 2, row_fn),
                 _layer_spec((1, D_MODEL), layer),
                 _layer_spec((D_MODEL, D_FF), layer), _layer_spec((D_MODEL, D_FF), layer),
                 _layer_spec((D_FF, D_MODEL), layer)]
    args += [modr, modr, modr, g, w1, w3, w2]
    if final:
        in_specs.append(_const_spec((1, D_MODEL)))
        args.append(final_g.reshape(1, D_MODEL))
    return pl.pallas_call(
        functools.partial(_ffn_kernel, chunks=chunks, final=final, n_first=n_first),
        grid=(n_tiles,),
        in_specs=in_specs,
        out_specs=pl.BlockSpec((tm, D_MODEL), lambda i: (i, 0)),
        out_shape=jax.ShapeDtypeStruct((n_tiles * tm, D_MODEL), F32),
        compiler_params=_params(),
        name="ffn",
    )(*args)


def _mix_in_kernel(h_ref, sh_ref, sc_ref, g_ref, win_ref, gq_ref, wqa_ref, wqb_ref, gkv_ref,
                   wuk_ref, wuvt_ref, cos_ref, sin_ref, dftc_ref,
                   q_ref, k_ref, vt_ref, vc_ref, gc_ref, gs_ref):
    h = h_ref[...]
    u = (_rms(h, g_ref[...]) * (1.0 + sc_ref[...]) + sh_ref[...]).astype(BF16)
    z = _dot(u, win_ref[...])
    o_cq = 2 * CONV_CH
    o_ckv = o_cq + Q_LORA
    o_kr = o_ckv + KV_LORA
    o_four = o_kr + 2 * HEAD_PAD
    vc_ref[...] = z[:, :CONV_CH] * _sigmoid(z[:, CONV_CH:o_cq])
    cos = cos_ref[...]
    sin = sin_ref[...]
    cqn = _rms(z[:, o_cq:o_ckv], gq_ref[...]).astype(BF16)
    qa = _dot(cqn, wqa_ref[...])
    qb = _dot(cqn, wqb_ref[...])
    for hh in range(N_HEADS):
        s = slice(hh * HEAD_PAD, (hh + 1) * HEAD_PAD)
        q_ref[hh] = ((qa[:, s] * cos + qb[:, s] * sin) * Q_SCALE).astype(BF16)
    ckvn = _rms(z[:, o_ckv:o_kr], gkv_ref[...]).astype(BF16)
    kn = _dot(ckvn, wuk_ref[...])
    kr = z[:, o_kr:o_kr + HEAD_PAD] * cos + z[:, o_kr + HEAD_PAD:o_four] * sin
    for hh in range(N_HEADS):
        s = slice(hh * HEAD_PAD, (hh + 1) * HEAD_PAD)
        k_ref[hh] = (kn[:, s] + kr).astype(BF16)
    vt_ref[...] = lax.dot_general(wuvt_ref[...], ckvn, NT_DIMS,
                                  preferred_element_type=F32).astype(BF16)
    zf = z[:, o_four:].astype(BF16)
    for gi in range(FOURIER_GROUPS):
        s = slice(gi * FOURIER_GROUP_CH, (gi + 1) * FOURIER_GROUP_CH)
        r = _dot(zf[:, s], dftc_ref[...])
        gc_ref[:, s] = r[:, :FOURIER_GROUP_CH]
        gs_ref[:, s] = r[:, FOURIER_GROUP_CH:]


def _mix_in_call(hs, modr, layer, g_mix, wl, tabs, dims):
    bsz, seq, lc, tm = dims["B"], dims["S"], dims["Lc"], dims["tm"]
    n_rows = bsz * (seq + lc)
    tpb = seq // tm
    n_lat = bsz * tpb
    n_tiles = n_rows // tm
    row_fn = dims["row_fn"]

    def tab_idx(i):
        return (jnp.where(i < n_lat, i % tpb, tpb), 0)

    def g_idx(i):
        return (jnp.where(i < n_lat, i % tpb, tpb), jnp.where(i < n_lat, i // tpb, i - n_lat))

    in_specs = [pl.BlockSpec((tm, D_MODEL), lambda i: (i, 0)),
                _mod_spec(layer, 3, row_fn), _mod_spec(layer, 4, row_fn),
                _layer_spec((1, D_MODEL), layer),
                _layer_spec((D_MODEL, W_IN_ALL), layer),
                _layer_spec((1, Q_LORA), layer),
                _layer_spec((Q_LORA, N_HEADS * HEAD_PAD), layer),
                _layer_spec((Q_LORA, N_HEADS * HEAD_PAD), layer),
                _layer_spec((1, KV_LORA), layer),
                _layer_spec((KV_LORA, N_HEADS * HEAD_PAD), layer),
                _layer_spec((N_HEADS * V_DIM, KV_LORA), layer),
                pl.BlockSpec((tm, HEAD_PAD), tab_idx), pl.BlockSpec((tm, HEAD_PAD), tab_idx),
                _const_spec((FOURIER_GROUP_CH, 2 * FOURIER_GROUP_CH))]
    out_specs = [pl.BlockSpec((N_HEADS, tm, HEAD_PAD), lambda i: (0, i, 0)),
                 pl.BlockSpec((N_HEADS, tm, HEAD_PAD), lambda i: (0, i, 0)),
                 pl.BlockSpec((N_HEADS * V_DIM, tm), lambda i: (0, i)),
                 pl.BlockSpec((tm, CONV_CH), lambda i: (i, 0)),
                 pl.BlockSpec((tm, FOURIER_WIDTH), g_idx),
                 pl.BlockSpec((tm, FOURIER_WIDTH), g_idx)]
    out_shape = [jax.ShapeDtypeStruct((N_HEADS, n_rows, HEAD_PAD), BF16),
                 jax.ShapeDtypeStruct((N_HEADS, n_rows, HEAD_PAD), BF16),
                 jax.ShapeDtypeStruct((N_HEADS * V_DIM, n_rows), BF16),
                 jax.ShapeDtypeStruct((n_rows, CONV_CH), F32),
                 jax.ShapeDtypeStruct((seq + lc, bsz * FOURIER_WIDTH), F32),
                 jax.ShapeDtypeStruct((seq + lc, bsz * FOURIER_WIDTH), F32)]
    return pl.pallas_call(
        _mix_in_kernel,
        grid=(n_tiles,),
        in_specs=in_specs,
        out_specs=out_specs,
        out_shape=out_shape,
        compiler_params=_params(),
        name="mix_in",
    )(hs, modr, modr, g_mix, wl["w_in_all"], wl["g_q"], wl["w_uq_a"],
      wl["w_uq_b"], wl["g_kv"], wl["w_uk"], wl["w_uvt"], tabs["cos"], tabs["sin"], tabs["dftc"])


def _attn_kernel(*refs, with_lat, lc, seq, tk):
    if with_lat:
        q_ref, kc_ref, vc_ref, kl_ref, vl_ref, o_ref = refs
    else:
        q_ref, kc_ref, vc_ref, o_ref = refs
    segs = [(kc_ref, vc_ref, 0, lc)]
    if with_lat:
        segs += [(kl_ref, vl_ref, s0, tk) for s0 in range(0, seq, tk)]
    n_blk = q_ref.shape[0] // QCOL
    cols = [slice(i * QCOL, (i + 1) * QCOL) for i in range(n_blk)]
    qs = [q_ref[c, :] for c in cols]

    def scores(j, i):
        kref, _, s0, size = segs[j]
        return lax.dot_general(kref[s0:s0 + size, :], qs[i], NT_DIMS,
                               preferred_element_type=F32)

    m = [None] * n_blk
    acc = [None] * n_blk
    s_cur = [scores(0, i) for i in range(n_blk)]
    for j, (_, vref, s0, size) in enumerate(segs):
        v_aug = jnp.concatenate([vref[:, s0:s0 + size], jnp.ones((DENOM_ROWS, size), BF16)], axis=0)
        s_next = [None] * n_blk
        for i in range(n_blk):
            if j + 1 < len(segs):
                s_next[i] = scores(j + 1, i)
            smax = jnp.max(s_cur[i], axis=0, keepdims=True)
            m_new = smax if m[i] is None else jnp.maximum(m[i], smax)
            p = jnp.exp2(s_cur[i] - m_new).astype(BF16)
            pv = _dot(v_aug, p)
            acc[i] = pv if m[i] is None else jnp.exp2(m[i] - m_new) * acc[i] + pv
            m[i] = m_new
        s_cur = s_next
    for i in range(n_blk):
        o_ref[:, cols[i]] = (acc[i][:V_DIM] / acc[i][V_DIM:V_DIM + 1]).astype(BF16)


def _attn_call(q, k, vt, dims, *, latent_queries):
    bsz, seq, lc = dims["B"], dims["S"], dims["Lc"]
    ctx_blk0 = (bsz * seq) // lc
    if latent_queries:
        tq = dims["tq"]
        nq = seq // tq
        q_idx = lambda b, h, i: (h, b * nq + i, 0)
        o_idx = lambda b, h, i: (h, b * nq + i)
        n_q = bsz * seq
    else:
        tq, nq = lc, 1
        q_idx = lambda b, h, i: (h, ctx_blk0 + b, 0)
        o_idx = lambda b, h, i: (h, b)
        n_q = bsz * lc
    in_specs = [pl.BlockSpec((None, tq, HEAD_PAD), q_idx),
                pl.BlockSpec((None, lc, HEAD_PAD), lambda b, h, i: (h, ctx_blk0 + b, 0)),
                pl.BlockSpec((V_DIM, lc), lambda b, h, i: (h, ctx_blk0 + b))]
    args = [q, k, vt]
    if latent_queries:
        in_specs += [pl.BlockSpec((None, seq, HEAD_PAD), lambda b, h, i: (h, b, 0)),
                     pl.BlockSpec((V_DIM, seq), lambda b, h, i: (h, b))]
        args += [k, vt]
    return pl.pallas_call(
        functools.partial(_attn_kernel, with_lat=latent_queries, lc=lc, seq=seq, tk=dims["tk"]),
        grid=(bsz, N_HEADS, nq),
        in_specs=in_specs,
        out_specs=pl.BlockSpec((V_DIM, tq), o_idx),
        out_shape=jax.ShapeDtypeStruct((N_HEADS * V_DIM, n_q), BF16),
        compiler_params=_params(3),
        name="attn_lat" if latent_queries else "attn_ctx",
    )(*args)


def _fft1_kernel(gc_ref, gs_ref, m_ref, zr_ref, zi_ref):
    n1 = gc_ref.shape[0]
    for j in range(FFT_GROUP):
        x = jnp.concatenate([gc_ref[:, j, :], gs_ref[:, j, :]], axis=0).astype(BF16)
        z = _dot(m_ref[j], x)
        zr_ref[:, j, :] = z[:n1]
        zi_ref[:, j, :] = z[n1:]


def _fft2_kernel(zr_ref, zi_ref, w_ref, f_ref):
    for j in range(FFT_GROUP):
        z = jnp.concatenate([zr_ref[j], zi_ref[j]], axis=0).astype(BF16)
        f_ref[:, j, :] = _dot(w_ref[...], z)


def _fft_tables(seq):
    n1 = int(round(seq ** 0.5))
    n2 = seq // n1
    assert n1 * n2 == seq and n1 % FFT_GROUP == 0 and n2 % FFT_GROUP == 0
    k1 = np.arange(n1)[None, :, None]
    t1 = np.arange(n1)[None, None, :]
    t2 = np.arange(n2)[:, None, None]
    ang = -2.0 * np.pi * (((t1 * k1) % n1) / n1 + ((t2 * k1) % seq) / seq)
    ar = np.cos(ang) / np.sqrt(n1)
    ai = np.sin(ang) / np.sqrt(n1)
    m1 = np.concatenate([np.concatenate([ar, ai], axis=2),
                         np.concatenate([ai, -ar], axis=2)], axis=1)
    k2 = np.arange(n2)[:, None]
    tt = np.arange(n2)[None, :]
    a2 = 2.0 * np.pi * ((k2 * tt) % n2) / n2
    w2 = np.concatenate([np.cos(a2), np.sin(a2)], axis=1) / np.sqrt(n2)
    return n1, n2, jnp.asarray(m1, F32).astype(BF16), jnp.asarray(w2, F32).astype(BF16)


def _fourier_lat_call(gc, gs, dims):
    bsz, seq = dims["B"], dims["S"]
    n1, n2, m1, w2 = _fft_tables(seq)
    ncol = bsz * FOURIER_WIDTH
    assert gc.shape[0] % n2 == 0
    gc3 = gc.reshape(gc.shape[0] // n2, n2, ncol)
    gs3 = gs.reshape(gs.shape[0] // n2, n2, ncol)
    zshape = jax.ShapeDtypeStruct((n1, n2, ncol), F32)
    strided = pl.BlockSpec((n1, FFT_GROUP, ncol), lambda j: (0, j, 0))
    zr, zi = pl.pallas_call(
        _fft1_kernel,
        grid=(n2 // FFT_GROUP,),
        in_specs=[strided, strided,
                  pl.BlockSpec((FFT_GROUP, 2 * n1, 2 * n1), lambda j: (j, 0, 0))],
        out_specs=[strided, strided],
        out_shape=[zshape, zshape],
        compiler_params=_params(),
        name="fft_stage1",
    )(gc3, gs3, m1)
    slab = pl.BlockSpec((FFT_GROUP, n2, ncol), lambda j: (j, 0, 0))
    f3 = pl.pallas_call(
        _fft2_kernel,
        grid=(n1 // FFT_GROUP,),
        in_specs=[slab, slab, _const_spec((n2, 2 * n2))],
        out_specs=pl.BlockSpec((n2, FFT_GROUP, ncol), lambda j: (0, j, 0)),
        out_shape=jax.ShapeDtypeStruct((n2, n1, ncol), F32),
        compiler_params=_params(),
        name="fft_stage2",
    )(zr, zi, w2)
    return f3.reshape(seq, ncol)


def _dft_ctx_kernel(gc_ref, gs_ref, c_ref, s_ref, f_ref):
    f_ref[...] = (_dot(c_ref[...], gc_ref[...].astype(BF16))
                  - _dot(s_ref[...], gs_ref[...].astype(BF16)))


def _fourier_ctx_call(gc, gs, dims):
    bsz, lc = dims["B"], dims["Lc"]
    ncol = bsz * FOURIER_WIDTH
    kt = (np.arange(lc)[:, None] * np.arange(lc)[None, :]) % lc
    ang = 2.0 * np.pi * kt / lc
    cm = jnp.asarray(np.cos(ang) / np.sqrt(lc), F32).astype(BF16)
    sm = jnp.asarray(np.sin(ang) / np.sqrt(lc), F32).astype(BF16)
    ctx_cols = pl.BlockSpec((lc, ncol), lambda j: (dims["S"] // lc, 0))
    return pl.pallas_call(
        _dft_ctx_kernel,
        grid=(1,),
        in_specs=[ctx_cols, ctx_cols, _const_spec((lc, lc)), _const_spec((lc, lc))],
        out_specs=pl.BlockSpec((lc, ncol), lambda j: (0, 0)),
        out_shape=jax.ShapeDtypeStruct((lc, ncol), F32),
        compiler_params=_params(),
        name="dft_ctx",
    )(gc, gs, cm, sm)


def _merge_kernel(*refs, with_ctx, n_lat, tpb, tm):
    refs = list(refs)
    xs_ref = refs.pop()
    xpad_ref = refs.pop()
    o_ref = refs.pop()
    it = iter(refs)
    h_ref, sh_ref, sc_ref, gt_ref, g_ref, wbg_ref, bbg_ref = [next(it) for _ in range(7)]
    vc_ref, vp_ref, vn_ref, wdw_ref, bdw_ref, lng_ref, lnb_ref, wpw_ref, bpw_ref = [
        next(it) for _ in range(9)]
    otl_ref = next(it)
    otc_ref = next(it) if with_ctx else None
    wo_ref = next(it)
    fl_ref = next(it)
    fc_ref = next(it) if with_ctx else None
    wf_ref, bf_ref, wout_ref = [next(it) for _ in range(3)]

    i = pl.program_id(0)
    pos = i % tpb
    first = pos == 0
    last = pos == tpb - 1
    if with_ctx:
        is_lat = i < n_lat
        first = jnp.logical_or(first, jnp.logical_not(is_lat))
        last = jnp.logical_or(last, jnp.logical_not(is_lat))

    h = h_ref[...]
    u = (_rms(h, g_ref[...]) * (1.0 + sc_ref[...]) + sh_ref[...]).astype(BF16)

    xpad_ref[0:HALO, :] = jnp.where(first, 0.0, vp_ref[...])
    xpad_ref[HALO:HALO + tm, :] = vc_ref[...]
    xpad_ref[HALO + tm:, :] = jnp.where(last, 0.0, vn_ref[...])
    base = HALO - CONV_K // 2
    n_shift_rows = xs_ref.shape[1]
    for r in range(SUBLANES):
        xs_ref[r] = xpad_ref[pl.ds(r, n_shift_rows), :]

    def conv_block(r0, c0):
        acc = None
        for kk in range(CONV_K):
            off = base + kk
            a0 = (off // SUBLANES) * SUBLANES + r0
            t = (xs_ref[off % SUBLANES, a0:a0 + CONV_ROWS, c0:c0 + LANES]
                 * wdw_ref[kk:kk + 1, c0:c0 + LANES])
            acc = t if acc is None else acc + t
        return acc + bdw_ref[:, c0:c0 + LANES]

    def gate(br):
        cols = slice(br * D_MODEL, (br + 1) * D_MODEL)
        return _sigmoid(_dot(u, wbg_ref[:, cols]) + bbg_ref[:, cols])

    ot = otl_ref[...]
    fo = fl_ref[...]
    if with_ctx:
        ot = jnp.where(is_lat, ot, otc_ref[...])
        fo = jnp.where(is_lat, fo, fc_ref[...])
    mix = gate(1) * lax.dot_general(ot, wo_ref[...], TN_DIMS, preferred_element_type=F32)
    mix = mix + gate(2) * (_dot(fo.astype(BF16), wf_ref[...]) + bf_ref[...])
    g_conv = gate(0)
    conv = jnp.concatenate(
        [jnp.concatenate([conv_block(r0, c0) for c0 in range(0, CONV_CH, LANES)], axis=-1)
         for r0 in range(0, tm, CONV_ROWS)], axis=0)
    mu = jnp.mean(conv, axis=-1, keepdims=True)
    cen = conv - mu
    var = jnp.mean(cen * cen, axis=-1, keepdims=True)
    ln = cen * lax.rsqrt(var + EPS) * lng_ref[...] + lnb_ref[...]
    y_conv = _dot((ln * _sigmoid(ln)).astype(BF16), wpw_ref[...]) + bpw_ref[...]
    mix = mix + g_conv * y_conv
    y = _dot(mix.astype(BF16), wout_ref[...])
    o_ref[...] = h + gt_ref[...] * y


def _merge_call(hs, modr, layer, g_mix, wl, vconv, ot_lat, ot_ctx, f_lat, f_ctx, dims, *, with_ctx):
    bsz, seq, lc, tm = dims["B"], dims["S"], dims["Lc"], dims["tm"]
    tpb = seq // tm
    n_lat = bsz * tpb
    n_tiles = n_lat + (bsz * lc // tm if with_ctx else 0)
    row_fn = dims["row_fn"]
    hpt = tm // HALO
    n_halo = vconv.shape[0] // HALO

    in_specs = [pl.BlockSpec((tm, D_MODEL), lambda i: (i, 0)),
                _mod_spec(layer, 3, row_fn), _mod_spec(layer, 4, row_fn),
                _mod_spec(layer, 5, row_fn),
                _layer_spec((1, D_MODEL), layer),
                _layer_spec((D_MODEL, N_BRANCH * D_MODEL), layer),
                _layer_spec((1, N_BRANCH * D_MODEL), layer),
                pl.BlockSpec((tm, CONV_CH), lambda i: (i, 0)),
                pl.BlockSpec((HALO, CONV_CH), lambda i: (jnp.maximum(i * hpt - 1, 0), 0)),
                pl.BlockSpec((HALO, CONV_CH), lambda i: (jnp.minimum((i + 1) * hpt, n_halo - 1), 0)),
                _layer_spec((CONV_K + 1, CONV_CH), layer),
                _layer_spec((1, CONV_CH), layer), _layer_spec((1, CONV_CH), layer),
                _layer_spec((1, CONV_CH), layer),
                _layer_spec((CONV_CH, D_MODEL), layer), _layer_spec((1, D_MODEL), layer),
                pl.BlockSpec((N_HEADS * V_DIM, tm), lambda i: (0, jnp.minimum(i, n_lat - 1)))]
    args = [hs, modr, modr, modr, g_mix, wl["w_bg"], wl["b_bg"],
            vconv, vconv, vconv, wl["w_dw"], wl["b_dw"], wl["ln_g"], wl["ln_b"], wl["w_pw"],
            wl["b_pw"], ot_lat]
    if with_ctx:
        in_specs.append(pl.BlockSpec((N_HEADS * V_DIM, tm), lambda i: (0, jnp.maximum(i - n_lat, 0))))
        args.append(ot_ctx)
    in_specs.append(_layer_spec((N_HEADS * V_DIM, D_MODEL), layer))
    args.append(wl["w_o"])
    in_specs.append(pl.BlockSpec(
        (tm, FOURIER_WIDTH),
        lambda i: (jnp.where(i < n_lat, i % tpb, 0), jnp.where(i < n_lat, i // tpb, 0))))
    args.append(f_lat)
    if with_ctx:
        in_specs.append(pl.BlockSpec((tm, FOURIER_WIDTH), lambda i: (0, jnp.maximum(i - n_lat, 0))))
        args.append(f_ctx)
    in_specs += [_layer_spec((FOURIER_WIDTH, D_MODEL), layer), _layer_spec((1, D_MODEL), layer),
                 _layer_spec((D_MODEL, D_MODEL), layer)]
    args += [wl["w_f"], wl["b_f"], wl["w_out"]]
    return pl.pallas_call(
        functools.partial(_merge_kernel, with_ctx=with_ctx, n_lat=n_lat, tpb=tpb, tm=tm),
        grid=(n_tiles,),
        in_specs=in_specs,
        out_specs=pl.BlockSpec((tm, D_MODEL), lambda i: (i, 0)),
        out_shape=jax.ShapeDtypeStruct((n_tiles * tm, D_MODEL), F32),
        scratch_shapes=[pltpu.VMEM((tm + 2 * HALO, CONV_CH), F32),
                        pltpu.VMEM((SUBLANES, tm + 2 * HALO - SUBLANES, CONV_CH), F32)],
        compiler_params=_params(),
        name="merge",
    )(*args)


def _rotate_half_cols(w):
    wr = w.reshape(w.shape[:-1] + (2, 2, ROPE_AXIS // 2))
    wr = jnp.concatenate([-wr[..., 1:2, :], wr[..., 0:1, :]], axis=-2)
    return wr.reshape(w.shape)


def _stack_weights(p):
    o1 = 2 * CONV_CH
    o2 = o1 + Q_LORA
    o3 = o2 + KV_LORA
    o4 = o3 + QK_ROPE
    w_in = p["w_in"]
    depth = w_in.shape[0]
    w_kr = w_in[:, :, o3:o4]
    zeros = lambda *shape: jnp.zeros((depth,) + shape, F32)
    tail = HEAD_PAD - QK_NOPE - QK_ROPE
    pad_kr = lambda w: jnp.concatenate([zeros(D_MODEL, QK_NOPE), w, zeros(D_MODEL, tail)], axis=-1)
    w_in_all = jnp.concatenate(
        [w_in[:, :, :o3], pad_kr(w_kr), pad_kr(_rotate_half_cols(w_kr)), w_in[:, :, o4:]], axis=-1)
    wq = p["w_uq"].reshape(depth, Q_LORA, N_HEADS, QK_NOPE + QK_ROPE)
    w_uq_a = jnp.concatenate([wq, zeros(Q_LORA, N_HEADS, tail)], axis=-1)
    w_uq_b = jnp.concatenate([zeros(Q_LORA, N_HEADS, QK_NOPE), _rotate_half_cols(wq[..., QK_NOPE:]),
                              zeros(Q_LORA, N_HEADS, tail)], axis=-1)
    wkv = p["w_ukv"].reshape(depth, KV_LORA, N_HEADS, QK_NOPE + V_DIM)
    w_uk = jnp.concatenate([wkv[..., :QK_NOPE], zeros(KV_LORA, N_HEADS, HEAD_PAD - QK_NOPE)], axis=-1)
    w_uvt = wkv[..., QK_NOPE:].reshape(depth, KV_LORA, N_HEADS * V_DIM).transpose(0, 2, 1)
    row = lambda v: v.reshape(depth, 1, -1)
    return {
        "w_in_all": w_in_all.astype(BF16),
        "g_q": row(p["g_qnorm"]), "g_kv": row(p["g_kvnorm"]),
        "w_uq_a": w_uq_a.reshape(depth, Q_LORA, N_HEADS * HEAD_PAD).astype(BF16),
        "w_uq_b": w_uq_b.reshape(depth, Q_LORA, N_HEADS * HEAD_PAD).astype(BF16),
        "w_uk": w_uk.reshape(depth, KV_LORA, N_HEADS * HEAD_PAD).astype(BF16),
        "w_uvt": w_uvt.astype(BF16),
        "w_bg": p["w_bgate"].astype(BF16), "b_bg": row(p["b_bgate"]),
        "w_dw": jnp.concatenate([p["w_dw"], zeros(1, CONV_CH)], axis=1),
        "b_dw": row(p["b_dw"]), "ln_g": row(p["ln_g_conv"]), "ln_b": row(p["ln_b_conv"]),
        "w_pw": p["w_pw_conv"].astype(BF16), "b_pw": row(p["b_pw_conv"]),
        "w_o": p["w_o_mla"].astype(BF16),
        "w_f": p["w_fourier"].astype(BF16), "b_f": row(p["b_fourier"]),
        "w_out": p["w_out"].astype(BF16),
    }


def _tables(seq, lc):
    rows = seq // GRID_W
    row = jnp.broadcast_to(jnp.arange(rows, dtype=F32)[:, None], (rows, GRID_W)).reshape(-1)
    col = jnp.broadcast_to(jnp.arange(GRID_W, dtype=F32)[None, :], (rows, GRID_W)).reshape(-1)
    inv = 1.0 / (ROPE_BASE ** (jnp.arange(ROPE_AXIS // 2, dtype=F32) * 2.0 / ROPE_AXIS))
    ar = row[:, None] * inv
    ac = col[:, None] * inv
    ang = jnp.concatenate([ar, ar, ac, ac], axis=-1)
    tail = HEAD_PAD - QK_NOPE - QK_ROPE
    cos = jnp.concatenate([jnp.ones((seq, QK_NOPE), F32), jnp.cos(ang), jnp.ones((seq, tail), F32)], axis=1)
    sin = jnp.concatenate([jnp.zeros((seq, QK_NOPE), F32), jnp.sin(ang), jnp.zeros((seq, tail), F32)], axis=1)
    cos = jnp.concatenate([cos, jnp.ones((lc, HEAD_PAD), F32)], axis=0)
    sin = jnp.concatenate([sin, jnp.zeros((lc, HEAD_PAD), F32)], axis=0)
    c = np.arange(FOURIER_GROUP_CH)
    a = 2.0 * np.pi * ((c[:, None] * c[None, :]) % FOURIER_GROUP_CH) / FOURIER_GROUP_CH
    dftc = np.concatenate([np.cos(a), np.sin(a)], axis=1) / np.sqrt(FOURIER_GROUP_CH)
    return {"cos": cos, "sin": sin, "dftc": jnp.asarray(dftc, F32).astype(BF16)}


def kernel(x, c, ctx, c_ctx, w_ada, b_ada, g_ffn1, w1_ffn1, w3_ffn1, w2_ffn1, g_mix, w_in, w_dw, b_dw, ln_g_conv, ln_b_conv, w_pw_conv, b_pw_conv, g_qnorm, w_uq, g_kvnorm, w_ukv, w_o_mla, w_fourier, b_fourier, w_bgate, b_bgate, w_out, g_ffn2, w1_ffn2, w3_ffn2, w2_ffn2, g_final):
    bsz, seq, _ = x.shape
    lc = ctx.shape[1]
    depth = w_ada.shape[0]
    tm = lc
    tf = 2 * tm
    assert bsz + 1 <= MOD_ROWS and seq % tf == 0 and (bsz * lc) % tf == 0 and tm % HALO == 0
    n_lat_rows = bsz * seq
    p = dict(w_in=w_in, w_dw=w_dw, b_dw=b_dw, ln_g_conv=ln_g_conv, ln_b_conv=ln_b_conv,
             w_pw_conv=w_pw_conv, b_pw_conv=b_pw_conv, g_qnorm=g_qnorm, w_uq=w_uq,
             g_kvnorm=g_kvnorm, w_ukv=w_ukv, w_o_mla=w_o_mla, w_fourier=w_fourier,
             b_fourier=b_fourier, w_bgate=w_bgate, b_bgate=b_bgate, w_out=w_out)

    def make_row_fn(tile):
        n_lat_tiles, tiles_per_seq = n_lat_rows // tile, seq // tile
        return lambda i: jnp.where(i < n_lat_tiles, i // tiles_per_seq, bsz)

    dims = {"B": bsz, "S": seq, "Lc": lc, "tm": tm, "row_fn": make_row_fn(tm),
            "tq": min(4096, seq), "tk": min(256, seq)}
    ffn_row_fn = make_row_fn(tf)
    n_ffn_all = (n_lat_rows + bsz * lc) // tf
    n_ffn_lat = n_lat_rows // tf

    cs = jnp.concatenate([c, c_ctx[None, :], jnp.zeros((MOD_ROWS - bsz - 1, D_MODEL), F32)], axis=0)
    mod = _mod_call(cs, w_ada, b_ada)
    modr = mod.reshape(depth, MOD_ROWS, N_ADA, D_MODEL).transpose(0, 2, 1, 3)
    modr = modr.reshape(depth * N_ADA * MOD_ROWS, 1, D_MODEL)
    tabs = _tables(seq, lc)

    wl = _stack_weights(p)
    stack_row = lambda v: v.reshape(depth, 1, D_MODEL)
    g_mix_r = stack_row(g_mix)
    ffn1 = (stack_row(g_ffn1), w1_ffn1, w3_ffn1, w2_ffn1)
    ffn2 = (stack_row(g_ffn2), w1_ffn2, w3_ffn2, w2_ffn2)

    hs = None
    for l in range(depth):
        last = l == depth - 1
        if l == 0:
            hs = _ffn_call(x.reshape(n_lat_rows, D_MODEL), modr, l, 0, *ffn1, n_tiles=n_ffn_all, tm=tf,
                           row_fn=ffn_row_fn, hs2=ctx.reshape(bsz * lc, D_MODEL))
        else:
            hs = _ffn_call(hs, modr, l, 0, *ffn1, n_tiles=n_ffn_all, tm=tf, row_fn=ffn_row_fn)
        q, k, vt, vconv, gc, gs = _mix_in_call(hs, modr, l, g_mix_r, wl, tabs, dims)
        ot_lat = _attn_call(q, k, vt, dims, latent_queries=True)
        f_lat = _fourier_lat_call(gc, gs, dims)
        if last:
            ot_ctx = f_ctx = None
        else:
            ot_ctx = _attn_call(q, k, vt, dims, latent_queries=False)
            f_ctx = _fourier_ctx_call(gc, gs, dims)
        hs = _merge_call(hs, modr, l, g_mix_r, wl, vconv, ot_lat, ot_ctx, f_lat, f_ctx, dims,
                         with_ctx=not last)
        hs = _ffn_call(hs, modr, l, 6, *ffn2, n_tiles=n_ffn_lat if last else n_ffn_all, tm=tf,
                       row_fn=ffn_row_fn, final_g=g_final if last else None)
    return hs.reshape(bsz, seq, D_MODEL)
```

```python
import functools

import numpy as np
import jax
import jax.numpy as jnp
from jax import lax
from jax.experimental import pallas as pl
from jax.experimental.pallas import tpu as pltpu

D_MODEL = 1024
D_FF = 2816
N_ADA = 9
CONV_CH = 384
CONV_K = 31
N_HEADS = 8
Q_LORA = 384
KV_LORA = 256
QK_NOPE = 64
QK_ROPE = 32
V_DIM = 64
ROPE_AXIS = QK_ROPE // 2
ROPE_BASE = 10000.0
GRID_W = 64
EPS = 1e-6
HALF = 0.5
ATTN_SCALE = (QK_NOPE + QK_ROPE) ** -0.5
FOURIER_GROUPS = 4
FOURIER_GROUP_CH = 128
FOURIER_WIDTH = FOURIER_GROUPS * FOURIER_GROUP_CH
N_BRANCH = 3

LANES = 128
SUBLANES = 8
HEAD_PAD = 128
MOD_ROWS = 8
HALO = 16
FFT_GROUP = 8
DENOM_ROWS = 16
QCOL = 256
LOG2E = float(np.log2(np.e))
Q_SCALE = ATTN_SCALE * LOG2E
VMEM_LIMIT = 56 * 1024 * 1024
W_IN_ALL = 2 * CONV_CH + Q_LORA + KV_LORA + 2 * HEAD_PAD + FOURIER_WIDTH
MIX_WIDTH = 2 * CONV_CH + Q_LORA + KV_LORA + QK_ROPE + FOURIER_WIDTH

BF16 = jnp.bfloat16
F32 = jnp.float32
NT_DIMS = (((1,), (1,)), ((), ()))
TN_DIMS = (((0,), (0,)), ((), ()))


def _sigmoid(x):
    return 1.0 / (1.0 + jnp.exp2(x * (-LOG2E)))


def _rms(x, g):
    return x * lax.rsqrt(jnp.mean(x * x, axis=-1, keepdims=True) + EPS) * g


def _dot(a, b):
    return jnp.dot(a, b, preferred_element_type=F32)


def _params(n_axes=1):
    return pltpu.CompilerParams(dimension_semantics=("arbitrary",) * n_axes,
                                vmem_limit_bytes=VMEM_LIMIT)


def _const_spec(shape):
    zeros = (0,) * len(shape)
    return pl.BlockSpec(shape, lambda *_: zeros, pipeline_mode=pl.Buffered(1))


def _layer_spec(shape, layer):
    idx = (layer,) + (0,) * len(shape)
    return pl.BlockSpec((None,) + tuple(shape), lambda *_: idx, pipeline_mode=pl.Buffered(1))


def _mod_kernel(cs_ref, w_ref, b_ref, o_ref):
    cs = cs_ref[...]
    a = (cs * _sigmoid(cs)).astype(BF16)
    o_ref[...] = _dot(a, w_ref[...].astype(BF16)) + b_ref[...]


def _mod_call(cs, w_ada, b_ada):
    n_layers = w_ada.shape[0]
    return pl.pallas_call(
        _mod_kernel,
        grid=(n_layers, N_ADA),
        in_specs=[pl.BlockSpec((MOD_ROWS, D_MODEL), lambda l, k: (0, 0)),
                  pl.BlockSpec((None, D_MODEL, D_MODEL), lambda l, k: (l, 0, k)),
                  pl.BlockSpec((None, 1, D_MODEL), lambda l, k: (l, 0, k))],
        out_specs=pl.BlockSpec((None, MOD_ROWS, D_MODEL), lambda l, k: (l, 0, k)),
        out_shape=jax.ShapeDtypeStruct((n_layers, MOD_ROWS, N_ADA * D_MODEL), F32),
        compiler_params=_params(2),
        name="ada_map",
    )(cs, w_ada, b_ada.reshape(n_layers, 1, N_ADA * D_MODEL))


def _mod_spec(layer, k, row_fn):
    base = (layer * N_ADA + k) * MOD_ROWS
    return pl.BlockSpec((None, 1, D_MODEL), lambda i: (base + row_fn(i), 0, 0))


def _ffn_kernel(*refs, chunks, final, n_first):
    refs = list(refs)
    o_ref = refs.pop()
    gf_ref = refs.pop() if final else None
    h2_ref = refs.pop(1) if n_first is not None else None
    h_ref, sh_ref, sc_ref, gt_ref, g_ref, w1_ref, w3_ref, w2_ref = refs
    h = h_ref[...]
    if n_first is not None:
        h = jnp.where(pl.program_id(0) < n_first, h, h2_ref[...])
    xb = (_rms(h, g_ref[...]) * (1.0 + sc_ref[...]) + sh_ref[...]).astype(BF16)
    acc = None
    off = 0
    for c in chunks:
        a = _dot(xb, w1_ref[:, off:off + c].astype(BF16))
        b = _dot(xb, w3_ref[:, off:off + c].astype(BF16))
        act = (a * _sigmoid(a) * b).astype(BF16)
        y = _dot(act, w2_ref[off:off + c, :].astype(BF16))
        acc = y if acc is None else acc + y
        off += c
    out = h + (HALF * gt_ref[...]) * acc
    if final:
        out = _rms(out, gf_ref[...])
    o_ref[...] = out


def _ffn_call(hs, modr, layer, kbase, g, w1, w3, w2, *, n_tiles, tm, row_fn, final_g=None, hs2=None):
    chunks = (512,) * (D_FF // 512) + ((D_FF % 512,) if D_FF % 512 else ())
    final = final_g is not None
    n_first = None if hs2 is None else hs.shape[0] // tm
    if hs2 is None:
        in_specs = [pl.BlockSpec((tm, D_MODEL), lambda i: (i, 0))]
        args = [hs]
    else:
        in_specs = [pl.BlockSpec((tm, D_MODEL), lambda i: (jnp.minimum(i, n_first - 1), 0)),
                    pl.BlockSpec((tm, D_MODEL), lambda i: (jnp.maximum(i - n_first, 0), 0))]
        args = [hs, hs2]
    in_specs += [_mod_spec(layer, kbase, row_fn), _mod_spec(layer, kbase + 1, row_fn),
                 _mod_spec(layer, kbase + 2, row_fn),
                 _layer_spec((1, D_MODEL), layer),
                 _layer_spec((D_MODEL, D_FF), layer), _layer_spec((D_MODEL, D_FF), layer),
                 _layer_spec((D_FF, D_MODEL), layer)]
    args += [modr, modr, modr, g, w1, w3, w2]
    if final:
        in_specs.append(_const_spec((1, D_MODEL)))
        args.append(final_g.reshape(1, D_MODEL))
    return pl.pallas_call(
        functools.partial(_ffn_kernel, chunks=chunks, final=final, n_first=n_first),
        grid=(n_tiles,),
        in_specs=in_specs,
        out_specs=pl.BlockSpec((tm, D_MODEL), lambda i: (i, 0)),
        out_shape=jax.ShapeDtypeStruct((n_tiles * tm, D_MODEL), F32),
        compiler_params=_params(),
        name="ffn",
    )(*args)


def _mix_in_kernel(h_ref, sh_ref, sc_ref, g_ref, win_ref, wkr_ref, gq_ref, wqa_ref, wqb_ref, gkv_ref,
                   wuk_ref, wuvt_ref, cos_ref, sin_ref, dftc_ref,
                   q_ref, k_ref, vt_ref, vc_ref, gc_ref, gs_ref):
    h = h_ref[...]
    u = (_rms(h, g_ref[...]) * (1.0 + sc_ref[...]) + sh_ref[...]).astype(BF16)
    o_cq = 2 * CONV_CH
    o_ckv = o_cq + Q_LORA
    o_kr = o_ckv + KV_LORA
    o_four = o_kr + 2 * HEAD_PAD
    z = jnp.concatenate(
        [_dot(u, win_ref[:, :o_kr].astype(BF16)), _dot(u, wkr_ref[...]),
         _dot(u, win_ref[:, o_kr + QK_ROPE:].astype(BF16))], axis=-1)
    vc_ref[...] = z[:, :CONV_CH] * _sigmoid(z[:, CONV_CH:o_cq])
    cos = cos_ref[...]
    sin = sin_ref[...]
    cqn = _rms(z[:, o_cq:o_ckv], gq_ref[...]).astype(BF16)
    qa = _dot(cqn, wqa_ref[...])
    qb = _dot(cqn, wqb_ref[...])
    for hh in range(N_HEADS):
        s = slice(hh * HEAD_PAD, (hh + 1) * HEAD_PAD)
        q_ref[hh] = ((qa[:, s] * cos + qb[:, s] * sin) * Q_SCALE).astype(BF16)
    ckvn = _rms(z[:, o_ckv:o_kr], gkv_ref[...]).astype(BF16)
    kn = _dot(ckvn, wuk_ref[...])
    kr = z[:, o_kr:o_kr + HEAD_PAD] * cos + z[:, o_kr + HEAD_PAD:o_four] * sin
    for hh in range(N_HEADS):
        s = slice(hh * HEAD_PAD, (hh + 1) * HEAD_PAD)
        k_ref[hh] = (kn[:, s] + kr).astype(BF16)
    vt_ref[...] = lax.dot_general(wuvt_ref[...], ckvn, NT_DIMS,
                                  preferred_element_type=F32).astype(BF16)
    zf = z[:, o_four:].astype(BF16)
    for gi in range(FOURIER_GROUPS):
        s = slice(gi * FOURIER_GROUP_CH, (gi + 1) * FOURIER_GROUP_CH)
        r = _dot(zf[:, s], dftc_ref[...])
        gc_ref[:, s] = r[:, :FOURIER_GROUP_CH]
        gs_ref[:, s] = r[:, FOURIER_GROUP_CH:]


def _mix_in_call(hs, modr, layer, g_mix, wl, tabs, dims):
    bsz, seq, lc, tm = dims["B"], dims["S"], dims["Lc"], dims["tm"]
    n_rows = bsz * (seq + lc)
    tpb = seq // tm
    n_lat = bsz * tpb
    n_tiles = n_rows // tm
    row_fn = dims["row_fn"]

    def tab_idx(i):
        return (jnp.where(i < n_lat, i % tpb, tpb), 0)

    def g_idx(i):
        return (jnp.where(i < n_lat, i % tpb, tpb), jnp.where(i < n_lat, i // tpb, i - n_lat))

    in_specs = [pl.BlockSpec((tm, D_MODEL), lambda i: (i, 0)),
                _mod_spec(layer, 3, row_fn), _mod_spec(layer, 4, row_fn),
                _layer_spec((1, D_MODEL), layer),
                _layer_spec((D_MODEL, MIX_WIDTH), layer),
                _layer_spec((D_MODEL, 2 * HEAD_PAD), layer),
                _layer_spec((1, Q_LORA), layer),
                _layer_spec((Q_LORA, N_HEADS * HEAD_PAD), layer),
                _layer_spec((Q_LORA, N_HEADS * HEAD_PAD), layer),
                _layer_spec((1, KV_LORA), layer),
                _layer_spec((KV_LORA, N_HEADS * HEAD_PAD), layer),
                _layer_spec((N_HEADS * V_DIM, KV_LORA), layer),
                pl.BlockSpec((tm, HEAD_PAD), tab_idx), pl.BlockSpec((tm, HEAD_PAD), tab_idx),
                _const_spec((FOURIER_GROUP_CH, 2 * FOURIER_GROUP_CH))]
    out_specs = [pl.BlockSpec((N_HEADS, tm, HEAD_PAD), lambda i: (0, i, 0)),
                 pl.BlockSpec((N_HEADS, tm, HEAD_PAD), lambda i: (0, i, 0)),
                 pl.BlockSpec((N_HEADS * V_DIM, tm), lambda i: (0, i)),
                 pl.BlockSpec((tm, CONV_CH), lambda i: (i, 0)),
                 pl.BlockSpec((tm, FOURIER_WIDTH), g_idx),
                 pl.BlockSpec((tm, FOURIER_WIDTH), g_idx)]
    out_shape = [jax.ShapeDtypeStruct((N_HEADS, n_rows, HEAD_PAD), BF16),
                 jax.ShapeDtypeStruct((N_HEADS, n_rows, HEAD_PAD), BF16),
                 jax.ShapeDtypeStruct((N_HEADS * V_DIM, n_rows), BF16),
                 jax.ShapeDtypeStruct((n_rows, CONV_CH), F32),
                 jax.ShapeDtypeStruct((seq + lc, bsz * FOURIER_WIDTH), F32),
                 jax.ShapeDtypeStruct((seq + lc, bsz * FOURIER_WIDTH), F32)]
    return pl.pallas_call(
        _mix_in_kernel,
        grid=(n_tiles,),
        in_specs=in_specs,
        out_specs=out_specs,
        out_shape=out_shape,
        compiler_params=_params(),
        name="mix_in",
    )(hs, modr, modr, g_mix, wl["w_in"], wl["w_kr"], wl["g_q"], wl["w_uq_a"],
      wl["w_uq_b"], wl["g_kv"], wl["w_uk"], wl["w_uvt"], tabs["cos"], tabs["sin"], tabs["dftc"])


def _attn_kernel(*refs, with_lat, lc, seq, tk):
    if with_lat:
        q_ref, kc_ref, vc_ref, kl_ref, vl_ref, o_ref = refs
    else:
        q_ref, kc_ref, vc_ref, o_ref = refs
    segs = [(kc_ref, vc_ref, 0, lc)]
    if with_lat:
        segs += [(kl_ref, vl_ref, s0, tk) for s0 in range(0, seq, tk)]
    n_blk = q_ref.shape[0] // QCOL
    cols = [slice(i * QCOL, (i + 1) * QCOL) for i in range(n_blk)]
    qs = [q_ref[c, :] for c in cols]

    def scores(j, i):
        kref, _, s0, size = segs[j]
        return lax.dot_general(kref[s0:s0 + size, :], qs[i], NT_DIMS,
                               preferred_element_type=F32)

    m = [None] * n_blk
    acc = [None] * n_blk
    s_cur = [scores(0, i) for i in range(n_blk)]
    for j, (_, vref, s0, size) in enumerate(segs):
        v_aug = jnp.concatenate([vref[:, s0:s0 + size], jnp.ones((DENOM_ROWS, size), BF16)], axis=0)
        s_next = [None] * n_blk
        for i in range(n_blk):
            if j + 1 < len(segs):
                s_next[i] = scores(j + 1, i)
            smax = jnp.max(s_cur[i], axis=0, keepdims=True)
            m_new = smax if m[i] is None else jnp.maximum(m[i], smax)
            p = jnp.exp2(s_cur[i] - m_new).astype(BF16)
            pv = _dot(v_aug, p)
            acc[i] = pv if m[i] is None else jnp.exp2(m[i] - m_new) * acc[i] + pv
            m[i] = m_new
        s_cur = s_next
    for i in range(n_blk):
        o_ref[:, cols[i]] = (acc[i][:V_DIM] / acc[i][V_DIM:V_DIM + 1]).astype(BF16)


def _attn_call(q, k, vt, dims, *, latent_queries):
    bsz, seq, lc = dims["B"], dims["S"], dims["Lc"]
    ctx_blk0 = (bsz * seq) // lc
    if latent_queries:
        tq = dims["tq"]
        nq = seq // tq
        q_idx = lambda b, h, i: (h, b * nq + i, 0)
        o_idx = lambda b, h, i: (h, b * nq + i)
        n_q = bsz * seq
    else:
        tq, nq = lc, 1
        q_idx = lambda b, h, i: (h, ctx_blk0 + b, 0)
        o_idx = lambda b, h, i: (h, b)
        n_q = bsz * lc
    in_specs = [pl.BlockSpec((None, tq, HEAD_PAD), q_idx),
                pl.BlockSpec((None, lc, HEAD_PAD), lambda b, h, i: (h, ctx_blk0 + b, 0)),
                pl.BlockSpec((V_DIM, lc), lambda b, h, i: (h, ctx_blk0 + b))]
    args = [q, k, vt]
    if latent_queries:
        in_specs += [pl.BlockSpec((None, seq, HEAD_PAD), lambda b, h, i: (h, b, 0)),
                     pl.BlockSpec((V_DIM, seq), lambda b, h, i: (h, b))]
        args += [k, vt]
    return pl.pallas_call(
        functools.partial(_attn_kernel, with_lat=latent_queries, lc=lc, seq=seq, tk=dims["tk"]),
        grid=(bsz, N_HEADS, nq),
        in_specs=in_specs,
        out_specs=pl.BlockSpec((V_DIM, tq), o_idx),
        out_shape=jax.ShapeDtypeStruct((N_HEADS * V_DIM, n_q), BF16),
        compiler_params=_params(3),
        name="attn_lat" if latent_queries else "attn_ctx",
    )(*args)


def _fft1_kernel(gc_ref, gs_ref, m_ref, zr_ref, zi_ref):
    n1 = gc_ref.shape[0]
    for j in range(FFT_GROUP):
        x = jnp.concatenate([gc_ref[:, j, :], gs_ref[:, j, :]], axis=0).astype(BF16)
        z = _dot(m_ref[j], x)
        zr_ref[:, j, :] = z[:n1]
        zi_ref[:, j, :] = z[n1:]


def _fft2_kernel(zr_ref, zi_ref, w_ref, f_ref):
    for j in range(FFT_GROUP):
        z = jnp.concatenate([zr_ref[j], zi_ref[j]], axis=0).astype(BF16)
        f_ref[:, j, :] = _dot(w_ref[...], z)


def _fft_tables(seq):
    n1 = int(round(seq ** 0.5))
    n2 = seq // n1
    assert n1 * n2 == seq and n1 % FFT_GROUP == 0 and n2 % FFT_GROUP == 0
    k1 = np.arange(n1)[None, :, None]
    t1 = np.arange(n1)[None, None, :]
    t2 = np.arange(n2)[:, None, None]
    ang = -2.0 * np.pi * (((t1 * k1) % n1) / n1 + ((t2 * k1) % seq) / seq)
    ar = np.cos(ang) / np.sqrt(n1)
    ai = np.sin(ang) / np.sqrt(n1)
    m1 = np.concatenate([np.concatenate([ar, ai], axis=2),
                         np.concatenate([ai, -ar], axis=2)], axis=1)
    k2 = np.arange(n2)[:, None]
    tt = np.arange(n2)[None, :]
    a2 = 2.0 * np.pi * ((k2 * tt) % n2) / n2
    w2 = np.concatenate([np.cos(a2), np.sin(a2)], axis=1) / np.sqrt(n2)
    return n1, n2, jnp.asarray(m1, F32).astype(BF16), jnp.asarray(w2, F32).astype(BF16)


def _fourier_lat_call(gc, gs, dims):
    bsz, seq = dims["B"], dims["S"]
    n1, n2, m1, w2 = _fft_tables(seq)
    ncol = bsz * FOURIER_WIDTH
    assert gc.shape[0] % n2 == 0
    gc3 = gc.reshape(gc.shape[0] // n2, n2, ncol)
    gs3 = gs.reshape(gs.shape[0] // n2, n2, ncol)
    zshape = jax.ShapeDtypeStruct((n1, n2, ncol), F32)
    strided = pl.BlockSpec((n1, FFT_GROUP, ncol), lambda j: (0, j, 0))
    zr, zi = pl.pallas_call(
        _fft1_kernel,
        grid=(n2 // FFT_GROUP,),
        in_specs=[strided, strided,
                  pl.BlockSpec((FFT_GROUP, 2 * n1, 2 * n1), lambda j: (j, 0, 0))],
        out_specs=[strided, strided],
        out_shape=[zshape, zshape],
        compiler_params=_params(),
        name="fft_stage1",
    )(gc3, gs3, m1)
    slab = pl.BlockSpec((FFT_GROUP, n2, ncol), lambda j: (j, 0, 0))
    f3 = pl.pallas_call(
        _fft2_kernel,
        grid=(n1 // FFT_GROUP,),
        in_specs=[slab, slab, _const_spec((n2, 2 * n2))],
        out_specs=pl.BlockSpec((n2, FFT_GROUP, ncol), lambda j: (0, j, 0)),
        out_shape=jax.ShapeDtypeStruct((n2, n1, ncol), F32),
        compiler_params=_params(),
        name="fft_stage2",
    )(zr, zi, w2)
    return f3.reshape(seq, ncol)


def _dft_ctx_kernel(gc_ref, gs_ref, c_ref, s_ref, f_ref):
    f_ref[...] = (_dot(c_ref[...], gc_ref[...].astype(BF16))
                  - _dot(s_ref[...], gs_ref[...].astype(BF16)))


def _fourier_ctx_call(gc, gs, dims):
    bsz, lc = dims["B"], dims["Lc"]
    ncol = bsz * FOURIER_WIDTH
    kt = (np.arange(lc)[:, None] * np.arange(lc)[None, :]) % lc
    ang = 2.0 * np.pi * kt / lc
    cm = jnp.asarray(np.cos(ang) / np.sqrt(lc), F32).astype(BF16)
    sm = jnp.asarray(np.sin(ang) / np.sqrt(lc), F32).astype(BF16)
    ctx_cols = pl.BlockSpec((lc, ncol), lambda j: (dims["S"] // lc, 0))
    return pl.pallas_call(
        _dft_ctx_kernel,
        grid=(1,),
        in_specs=[ctx_cols, ctx_cols, _const_spec((lc, lc)), _const_spec((lc, lc))],
        out_specs=pl.BlockSpec((lc, ncol), lambda j: (0, 0)),
        out_shape=jax.ShapeDtypeStruct((lc, ncol), F32),
        compiler_params=_params(),
        name="dft_ctx",
    )(gc, gs, cm, sm)


def _merge_kernel(*refs, with_ctx, n_lat, tpb, tm):
    refs = list(refs)
    xs_ref = refs.pop()
    xpad_ref = refs.pop()
    o_ref = refs.pop()
    it = iter(refs)
    h_ref, sh_ref, sc_ref, gt_ref, g_ref, wbg_ref, bbg_ref = [next(it) for _ in range(7)]
    vc_ref, vp_ref, vn_ref, wdw_ref, bdw_ref, lng_ref, lnb_ref, wpw_ref, bpw_ref = [
        next(it) for _ in range(9)]
    otl_ref = next(it)
    otc_ref = next(it) if with_ctx else None
    wo_ref = next(it)
    fl_ref = next(it)
    fc_ref = next(it) if with_ctx else None
    wf_ref, bf_ref, wout_ref = [next(it) for _ in range(3)]

    i = pl.program_id(0)
    pos = i % tpb
    first = pos == 0
    last = pos == tpb - 1
    if with_ctx:
        is_lat = i < n_lat
        first = jnp.logical_or(first, jnp.logical_not(is_lat))
        last = jnp.logical_or(last, jnp.logical_not(is_lat))

    h = h_ref[...]
    u = (_rms(h, g_ref[...]) * (1.0 + sc_ref[...]) + sh_ref[...]).astype(BF16)

    xpad_ref[0:HALO, :] = jnp.where(first, 0.0, vp_ref[...])
    xpad_ref[HALO:HALO + tm, :] = vc_ref[...]
    xpad_ref[HALO + tm:, :] = jnp.where(last, 0.0, vn_ref[...])
    base = HALO - CONV_K // 2
    n_shift_rows = xs_ref.shape[1]
    for r in range(SUBLANES):
        xs_ref[r] = xpad_ref[pl.ds(r, n_shift_rows), :]

    def conv_lanes(c0):
        acc = None
        for kk in range(CONV_K):
            off = base + kk
            a0 = (off // SUBLANES) * SUBLANES
            t = (xs_ref[off % SUBLANES, a0:a0 + tm, c0:c0 + LANES]
                 * wdw_ref[kk:kk + 1, c0:c0 + LANES])
            acc = t if acc is None else acc + t
        return acc + bdw_ref[:, c0:c0 + LANES]

    ot = otl_ref[...]
    fo = fl_ref[...]
    if with_ctx:
        ot = jnp.where(is_lat, ot, otc_ref[...])
        fo = jnp.where(is_lat, fo, fc_ref[...])
    assert CONV_CH == N_BRANCH * LANES
    gate_pre = []
    conv_parts = []
    for br in range(N_BRANCH):
        cols = slice(br * D_MODEL, (br + 1) * D_MODEL)
        gate_pre.append(_dot(u, wbg_ref[:, cols]) + bbg_ref[:, cols])
        conv_parts.append(conv_lanes(br * LANES))
    y_mla = lax.dot_general(ot, wo_ref[...], TN_DIMS, preferred_element_type=F32)
    y_four = _dot(fo.astype(BF16), wf_ref[...]) + bf_ref[...]
    conv = jnp.concatenate(conv_parts, axis=-1)
    mu = jnp.mean(conv, axis=-1, keepdims=True)
    cen = conv - mu
    var = jnp.mean(cen * cen, axis=-1, keepdims=True)
    ln = cen * lax.rsqrt(var + EPS) * lng_ref[...] + lnb_ref[...]
    y_conv = _dot((ln * _sigmoid(ln)).astype(BF16), wpw_ref[...]) + bpw_ref[...]

    mix = (_sigmoid(gate_pre[0]) * y_conv + _sigmoid(gate_pre[1]) * y_mla
           + _sigmoid(gate_pre[2]) * y_four)
    y = _dot(mix.astype(BF16), wout_ref[...])
    o_ref[...] = h + gt_ref[...] * y


def _merge_call(hs, modr, layer, g_mix, wl, vconv, ot_lat, ot_ctx, f_lat, f_ctx, dims, *, with_ctx):
    bsz, seq, lc, tm = dims["B"], dims["S"], dims["Lc"], dims["tm"]
    tpb = seq // tm
    n_lat = bsz * tpb
    n_tiles = n_lat + (bsz * lc // tm if with_ctx else 0)
    row_fn = dims["row_fn"]
    hpt = tm // HALO
    n_halo = vconv.shape[0] // HALO

    in_specs = [pl.BlockSpec((tm, D_MODEL), lambda i: (i, 0)),
                _mod_spec(layer, 3, row_fn), _mod_spec(layer, 4, row_fn),
                _mod_spec(layer, 5, row_fn),
                _layer_spec((1, D_MODEL), layer),
                _layer_spec((D_MODEL, N_BRANCH * D_MODEL), layer),
                _layer_spec((1, N_BRANCH * D_MODEL), layer),
                pl.BlockSpec((tm, CONV_CH), lambda i: (i, 0)),
                pl.BlockSpec((HALO, CONV_CH), lambda i: (jnp.maximum(i * hpt - 1, 0), 0)),
                pl.BlockSpec((HALO, CONV_CH), lambda i: (jnp.minimum((i + 1) * hpt, n_halo - 1), 0)),
                _layer_spec((CONV_K + 1, CONV_CH), layer),
                _layer_spec((1, CONV_CH), layer), _layer_spec((1, CONV_CH), layer),
                _layer_spec((1, CONV_CH), layer),
                _layer_spec((CONV_CH, D_MODEL), layer), _layer_spec((1, D_MODEL), layer),
                pl.BlockSpec((N_HEADS * V_DIM, tm), lambda i: (0, jnp.minimum(i, n_lat - 1)))]
    args = [hs, modr, modr, modr, g_mix, wl["w_bg"], wl["b_bg"],
            vconv, vconv, vconv, wl["w_dw"], wl["b_dw"], wl["ln_g"], wl["ln_b"], wl["w_pw"],
            wl["b_pw"], ot_lat]
    if with_ctx:
        in_specs.append(pl.BlockSpec((N_HEADS * V_DIM, tm), lambda i: (0, jnp.maximum(i - n_lat, 0))))
        args.append(ot_ctx)
    in_specs.append(_layer_spec((N_HEADS * V_DIM, D_MODEL), layer))
    args.append(wl["w_o"])
    in_specs.append(pl.BlockSpec(
        (tm, FOURIER_WIDTH),
        lambda i: (jnp.where(i < n_lat, i % tpb, 0), jnp.where(i < n_lat, i // tpb, 0))))
    args.append(f_lat)
    if with_ctx:
        in_specs.append(pl.BlockSpec((tm, FOURIER_WIDTH), lambda i: (0, jnp.maximum(i - n_lat, 0))))
        args.append(f_ctx)
    in_specs += [_layer_spec((FOURIER_WIDTH, D_MODEL), layer), _layer_spec((1, D_MODEL), layer),
                 _layer_spec((D_MODEL, D_MODEL), layer)]
    args += [wl["w_f"], wl["b_f"], wl["w_out"]]
    return pl.pallas_call(
        functools.partial(_merge_kernel, with_ctx=with_ctx, n_lat=n_lat, tpb=tpb, tm=tm),
        grid=(n_tiles,),
        in_specs=in_specs,
        out_specs=pl.BlockSpec((tm, D_MODEL), lambda i: (i, 0)),
        out_shape=jax.ShapeDtypeStruct((n_tiles * tm, D_MODEL), F32),
        scratch_shapes=[pltpu.VMEM((tm + 2 * HALO, CONV_CH), F32),
                        pltpu.VMEM((SUBLANES, tm + 2 * HALO - SUBLANES, CONV_CH), F32)],
        compiler_params=_params(),
        name="merge",
    )(*args)


def _rotate_half_cols(w):
    wr = w.reshape(w.shape[:-1] + (2, 2, ROPE_AXIS // 2))
    wr = jnp.concatenate([-wr[..., 1:2, :], wr[..., 0:1, :]], axis=-2)
    return wr.reshape(w.shape)


def _stack_weights(p):
    o1 = 2 * CONV_CH
    o2 = o1 + Q_LORA
    o3 = o2 + KV_LORA
    o4 = o3 + QK_ROPE
    w_in = p["w_in"]
    depth = w_in.shape[0]
    w_kr = w_in[:, :, o3:o4]
    zeros = lambda *shape: jnp.zeros((depth,) + shape, F32)
    tail = HEAD_PAD - QK_NOPE - QK_ROPE
    pad_kr = lambda w: jnp.concatenate([zeros(D_MODEL, QK_NOPE), w, zeros(D_MODEL, tail)], axis=-1)
    w_kr2 = jnp.concatenate([pad_kr(w_kr), pad_kr(_rotate_half_cols(w_kr))], axis=-1)
    wq = p["w_uq"].reshape(depth, Q_LORA, N_HEADS, QK_NOPE + QK_ROPE)
    w_uq_a = jnp.concatenate([wq, zeros(Q_LORA, N_HEADS, tail)], axis=-1)
    w_uq_b = jnp.concatenate([zeros(Q_LORA, N_HEADS, QK_NOPE), _rotate_half_cols(wq[..., QK_NOPE:]),
                              zeros(Q_LORA, N_HEADS, tail)], axis=-1)
    wkv = p["w_ukv"].reshape(depth, KV_LORA, N_HEADS, QK_NOPE + V_DIM)
    w_uk = jnp.concatenate([wkv[..., :QK_NOPE], zeros(KV_LORA, N_HEADS, HEAD_PAD - QK_NOPE)], axis=-1)
    w_uvt = wkv[..., QK_NOPE:].reshape(depth, KV_LORA, N_HEADS * V_DIM).transpose(0, 2, 1)
    row = lambda v: v.reshape(depth, 1, -1)
    return {
        "w_in": w_in, "w_kr": w_kr2.astype(BF16),
        "g_q": row(p["g_qnorm"]), "g_kv": row(p["g_kvnorm"]),
        "w_uq_a": w_uq_a.reshape(depth, Q_LORA, N_HEADS * HEAD_PAD).astype(BF16),
        "w_uq_b": w_uq_b.reshape(depth, Q_LORA, N_HEADS * HEAD_PAD).astype(BF16),
        "w_uk": w_uk.reshape(depth, KV_LORA, N_HEADS * HEAD_PAD).astype(BF16),
        "w_uvt": w_uvt.astype(BF16),
        "w_bg": p["w_bgate"].astype(BF16), "b_bg": row(p["b_bgate"]),
        "w_dw": jnp.concatenate([p["w_dw"], zeros(1, CONV_CH)], axis=1),
        "b_dw": row(p["b_dw"]), "ln_g": row(p["ln_g_conv"]), "ln_b": row(p["ln_b_conv"]),
        "w_pw": p["w_pw_conv"].astype(BF16), "b_pw": row(p["b_pw_conv"]),
        "w_o": p["w_o_mla"].astype(BF16),
        "w_f": p["w_fourier"].astype(BF16), "b_f": row(p["b_fourier"]),
        "w_out": p["w_out"].astype(BF16),
    }


def _tables(seq, lc):
    rows = seq // GRID_W
    row = jnp.broadcast_to(jnp.arange(rows, dtype=F32)[:, None], (rows, GRID_W)).reshape(-1)
    col = jnp.broadcast_to(jnp.arange(GRID_W, dtype=F32)[None, :], (rows, GRID_W)).reshape(-1)
    inv = 1.0 / (ROPE_BASE ** (jnp.arange(ROPE_AXIS // 2, dtype=F32) * 2.0 / ROPE_AXIS))
    ar = row[:, None] * inv
    ac = col[:, None] * inv
    ang = jnp.concatenate([ar, ar, ac, ac], axis=-1)
    tail = HEAD_PAD - QK_NOPE - QK_ROPE
    cos = jnp.concatenate([jnp.ones((seq, QK_NOPE), F32), jnp.cos(ang), jnp.ones((seq, tail), F32)], axis=1)
    sin = jnp.concatenate([jnp.zeros((seq, QK_NOPE), F32), jnp.sin(ang), jnp.zeros((seq, tail), F32)], axis=1)
    cos = jnp.concatenate([cos, jnp.ones((lc, HEAD_PAD), F32)], axis=0)
    sin = jnp.concatenate([sin, jnp.zeros((lc, HEAD_PAD), F32)], axis=0)
    c = np.arange(FOURIER_GROUP_CH)
    a = 2.0 * np.pi * ((c[:, None] * c[None, :]) % FOURIER_GROUP_CH) / FOURIER_GROUP_CH
    dftc = np.concatenate([np.cos(a), np.sin(a)], axis=1) / np.sqrt(FOURIER_GROUP_CH)
    return {"cos": cos, "sin": sin, "dftc": jnp.asarray(dftc, F32).astype(BF16)}


def kernel(x, c, ctx, c_ctx, w_ada, b_ada, g_ffn1, w1_ffn1, w3_ffn1, w2_ffn1, g_mix, w_in, w_dw, b_dw, ln_g_conv, ln_b_conv, w_pw_conv, b_pw_conv, g_qnorm, w_uq, g_kvnorm, w_ukv, w_o_mla, w_fourier, b_fourier, w_bgate, b_bgate, w_out, g_ffn2, w1_ffn2, w3_ffn2, w2_ffn2, g_final):
    bsz, seq, _ = x.shape
    lc = ctx.shape[1]
    depth = w_ada.shape[0]
    tm = lc
    tf = 2 * tm
    assert bsz + 1 <= MOD_ROWS and seq % tf == 0 and (bsz * lc) % tf == 0 and tm % HALO == 0
    n_lat_rows = bsz * seq
    p = dict(w_in=w_in, w_dw=w_dw, b_dw=b_dw, ln_g_conv=ln_g_conv, ln_b_conv=ln_b_conv,
             w_pw_conv=w_pw_conv, b_pw_conv=b_pw_conv, g_qnorm=g_qnorm, w_uq=w_uq,
             g_kvnorm=g_kvnorm, w_ukv=w_ukv, w_o_mla=w_o_mla, w_fourier=w_fourier,
             b_fourier=b_fourier, w_bgate=w_bgate, b_bgate=b_bgate, w_out=w_out)

    def make_row_fn(tile):
        n_lat_tiles, tiles_per_seq = n_lat_rows // tile, seq // tile
        return lambda i: jnp.where(i < n_lat_tiles, i // tiles_per_seq, bsz)

    dims = {"B": bsz, "S": seq, "Lc": lc, "tm": tm, "row_fn": make_row_fn(tm),
            "tq": min(4096, seq), "tk": min(256, seq)}
    ffn_row_fn = make_row_fn(tf)
    n_ffn_all = (n_lat_rows + bsz * lc) // tf
    n_ffn_lat = n_lat_rows // tf

    cs = jnp.concatenate([c, c_ctx[None, :], jnp.zeros((MOD_ROWS - bsz - 1, D_MODEL), F32)], axis=0)
    mod = _mod_call(cs, w_ada, b_ada)
    modr = mod.reshape(depth, MOD_ROWS, N_ADA, D_MODEL).transpose(0, 2, 1, 3)
    modr = modr.reshape(depth * N_ADA * MOD_ROWS, 1, D_MODEL)
    tabs = _tables(seq, lc)

    wl = _stack_weights(p)
    stack_row = lambda v: v.reshape(depth, 1, D_MODEL)
    g_mix_r = stack_row(g_mix)
    ffn1 = (stack_row(g_ffn1), w1_ffn1, w3_ffn1, w2_ffn1)
    ffn2 = (stack_row(g_ffn2), w1_ffn2, w3_ffn2, w2_ffn2)

    hs = None
    for l in range(depth):
        last = l == depth - 1
        if l == 0:
            hs = _ffn_call(x.reshape(n_lat_rows, D_MODEL), modr, l, 0, *ffn1, n_tiles=n_ffn_all, tm=tf,
                           row_fn=ffn_row_fn, hs2=ctx.reshape(bsz * lc, D_MODEL))
        else:
            hs = _ffn_call(hs, modr, l, 0, *ffn1, n_tiles=n_ffn_all, tm=tf, row_fn=ffn_row_fn)
        q, k, vt, vconv, gc, gs = _mix_in_call(hs, modr, l, g_mix_r, wl, tabs, dims)
        ot_lat = _attn_call(q, k, vt, dims, latent_queries=True)
        f_lat = _fourier_lat_call(gc, gs, dims)
        if last:
            ot_ctx = f_ctx = None
        else:
            ot_ctx = _attn_call(q, k, vt, dims, latent_queries=False)
            f_ctx = _fourier_ctx_call(gc, gs, dims)
        hs = _merge_call(hs, modr, l, g_mix_r, wl, vconv, ot_lat, ot_ctx, f_lat, f_ctx, dims,
                         with_ctx=not last)
        hs = _ffn_call(hs, modr, l, 6, *ffn2, n_tiles=n_ffn_lat if last else n_ffn_all, tm=tf,
                       row_fn=ffn_row_fn, final_g=g_final if last else None)
    return hs.reshape(bsz, seq, D_MODEL)
```

```python
import functools

import numpy as np
import jax
import jax.numpy as jnp
from jax import lax
from jax.experimental import pallas as pl
from jax.experimental.pallas import tpu as pltpu

D_MODEL = 1024
D_FF = 2816
N_ADA = 9
CONV_CH = 384
CONV_K = 31
N_HEADS = 8
Q_LORA = 384
KV_LORA = 256
QK_NOPE = 64
QK_ROPE = 32
V_DIM = 64
ROPE_AXIS = QK_ROPE // 2
ROPE_BASE = 10000.0
GRID_W = 64
EPS = 1e-6
HALF = 0.5
ATTN_SCALE = (QK_NOPE + QK_ROPE) ** -0.5
FOURIER_GROUPS = 4
FOURIER_GROUP_CH = 128
FOURIER_WIDTH = FOURIER_GROUPS * FOURIER_GROUP_CH
N_BRANCH = 3

LANES = 128
SUBLANES = 8
HEAD_PAD = 128
MOD_ROWS = 8
HALO = 16
FFT_GROUP = 8
DENOM_ROWS = 16
QCOL = 256
LOG2E = float(np.log2(np.e))
Q_SCALE = ATTN_SCALE * LOG2E
VMEM_LIMIT = 56 * 1024 * 1024
W_IN_ALL = 2 * CONV_CH + Q_LORA + KV_LORA + HEAD_PAD + FOURIER_WIDTH
MIX_WIDTH = 2 * CONV_CH + Q_LORA + KV_LORA + QK_ROPE + FOURIER_WIDTH

BF16 = jnp.bfloat16
F32 = jnp.float32
NT_DIMS = (((1,), (1,)), ((), ()))
TN_DIMS = (((0,), (0,)), ((), ()))


def _sigmoid(x):
    return 1.0 / (1.0 + jnp.exp2(x * (-LOG2E)))


def _rms(x, g):
    return x * lax.rsqrt(jnp.mean(x * x, axis=-1, keepdims=True) + EPS) * g


def _dot(a, b):
    return jnp.dot(a, b, preferred_element_type=F32)


def _params(n_axes=1):
    return pltpu.CompilerParams(dimension_semantics=("arbitrary",) * n_axes,
                                vmem_limit_bytes=VMEM_LIMIT)


def _const_spec(shape):
    zeros = (0,) * len(shape)
    return pl.BlockSpec(shape, lambda *_: zeros, pipeline_mode=pl.Buffered(1))


def _layer_spec(shape, layer):
    idx = (layer,) + (0,) * len(shape)
    return pl.BlockSpec((None,) + tuple(shape), lambda *_: idx, pipeline_mode=pl.Buffered(1))


def _mod_kernel(cs_ref, w_ref, b_ref, o_ref):
    cs = cs_ref[...]
    a = (cs * _sigmoid(cs)).astype(BF16)
    o_ref[...] = _dot(a, w_ref[...].astype(BF16)) + b_ref[...]


def _mod_call(cs, w_ada, b_ada):
    n_layers = w_ada.shape[0]
    return pl.pallas_call(
        _mod_kernel,
        grid=(n_layers, N_ADA),
        in_specs=[pl.BlockSpec((MOD_ROWS, D_MODEL), lambda l, k: (0, 0)),
                  pl.BlockSpec((None, D_MODEL, D_MODEL), lambda l, k: (l, 0, k)),
                  pl.BlockSpec((None, 1, D_MODEL), lambda l, k: (l, 0, k))],
        out_specs=pl.BlockSpec((None, MOD_ROWS, D_MODEL), lambda l, k: (l, 0, k)),
        out_shape=jax.ShapeDtypeStruct((n_layers, MOD_ROWS, N_ADA * D_MODEL), F32),
        compiler_params=_params(2),
        name="ada_map",
    )(cs, w_ada, b_ada.reshape(n_layers, 1, N_ADA * D_MODEL))


def _mod_spec(layer, k, row_fn):
    base = (layer * N_ADA + k) * MOD_ROWS
    return pl.BlockSpec((None, 1, D_MODEL), lambda i: (base + row_fn(i), 0, 0))


def _ffn_kernel(*refs, chunks, final, n_first, layer):
    refs = list(refs)
    o_ref = refs.pop()
    gf_ref = refs.pop() if final else None
    h2_ref = refs.pop(1) if n_first is not None else None
    h_ref, sh_ref, sc_ref, gt_ref, g_ref, w1_ref, w3_ref, w2_ref = refs
    g_ref = g_ref.at[pl.ds(layer, 1)]
    h = h_ref[...]
    if n_first is not None:
        h = jnp.where(pl.program_id(0) < n_first, h, h2_ref[...])
    xb = (_rms(h, g_ref[...]) * (1.0 + sc_ref[...]) + sh_ref[...]).astype(BF16)
    acc = None
    off = 0
    for c in chunks:
        a = _dot(xb, w1_ref[:, off:off + c].astype(BF16))
        b = _dot(xb, w3_ref[:, off:off + c].astype(BF16))
        act = (a * _sigmoid(a) * b).astype(BF16)
        y = _dot(act, w2_ref[off:off + c, :].astype(BF16))
        acc = y if acc is None else acc + y
        off += c
    out = h + (HALF * gt_ref[...]) * acc
    if final:
        out = _rms(out, gf_ref[...])
    o_ref[...] = out


def _ffn_call(hs, modr, layer, kbase, g, w1, w3, w2, *, n_tiles, tm, row_fn, final_g=None, hs2=None):
    chunks = (512,) * (D_FF // 512) + ((D_FF % 512,) if D_FF % 512 else ())
    final = final_g is not None
    n_first = None if hs2 is None else hs.shape[0] // tm
    if hs2 is None:
        in_specs = [pl.BlockSpec((tm, D_MODEL), lambda i: (i, 0))]
        args = [hs]
    else:
        in_specs = [pl.BlockSpec((tm, D_MODEL), lambda i: (jnp.minimum(i, n_first - 1), 0)),
                    pl.BlockSpec((tm, D_MODEL), lambda i: (jnp.maximum(i - n_first, 0), 0))]
        args = [hs, hs2]
    in_specs += [_mod_spec(layer, kbase, row_fn), _mod_spec(layer, kbase + 1, row_fn),
                 _mod_spec(layer, kbase + 2, row_fn),
                 _const_spec(g.shape),
                 _layer_spec((D_MODEL, D_FF), layer), _layer_spec((D_MODEL, D_FF), layer),
                 _layer_spec((D_FF, D_MODEL), layer)]
    args += [modr, modr, modr, g, w1, w3, w2]
    if final:
        in_specs.append(_const_spec((1, D_MODEL)))
        args.append(final_g.reshape(1, D_MODEL))
    return pl.pallas_call(
        functools.partial(_ffn_kernel, chunks=chunks, final=final, n_first=n_first, layer=layer),
        grid=(n_tiles,),
        in_specs=in_specs,
        out_specs=pl.BlockSpec((tm, D_MODEL), lambda i: (i, 0)),
        out_shape=jax.ShapeDtypeStruct((n_tiles * tm, D_MODEL), F32),
        compiler_params=_params(),
        name="ffn",
    )(*args)


def _mix_in_kernel(h_ref, sh_ref, sc_ref, g_ref, win_ref, wkr_ref, gq_ref, wqa_ref, gkv_ref,
                   wuk_ref, wuvt_ref, cos_ref, sin_ref, dftc_ref,
                   q_ref, k_ref, vt_ref, vc_ref, gc_ref, gs_ref, *, layer):
    g_ref, gq_ref, gkv_ref = (r.at[pl.ds(layer, 1)] for r in (g_ref, gq_ref, gkv_ref))
    h = h_ref[...]
    u = (_rms(h, g_ref[...]) * (1.0 + sc_ref[...]) + sh_ref[...]).astype(BF16)
    o_cq = 2 * CONV_CH
    o_ckv = o_cq + Q_LORA
    o_kr = o_ckv + KV_LORA
    o_four = o_kr + HEAD_PAD
    z = jnp.concatenate(
        [_dot(u, win_ref[:, :o_kr].astype(BF16)), _dot(u, wkr_ref[...]),
         _dot(u, win_ref[:, o_kr + QK_ROPE:].astype(BF16))], axis=-1)
    vc_ref[...] = z[:, :CONV_CH] * _sigmoid(z[:, CONV_CH:o_cq])
    cos = cos_ref[...]
    sin = sin_ref[...]
    lane = lax.broadcasted_iota(jnp.int32, (1, HEAD_PAD), 1)
    first_half = ((lane - QK_NOPE) % ROPE_AXIS) < (ROPE_AXIS // 2)

    def rot_half(x):
        fwd = pltpu.roll(x, ROPE_AXIS // 2, axis=1)
        bwd = pltpu.roll(x, HEAD_PAD - ROPE_AXIS // 2, axis=1)
        return jnp.where(first_half, -bwd, fwd)

    cqn = _rms(z[:, o_cq:o_ckv], gq_ref[...]).astype(BF16)
    qa = _dot(cqn, wqa_ref[...])
    for hh in range(N_HEADS):
        s = slice(hh * HEAD_PAD, (hh + 1) * HEAD_PAD)
        q_ref[hh] = ((qa[:, s] * cos + rot_half(qa[:, s]) * sin) * Q_SCALE).astype(BF16)
    ckvn = _rms(z[:, o_ckv:o_kr], gkv_ref[...]).astype(BF16)
    kn = _dot(ckvn, wuk_ref[...])
    zkr = z[:, o_kr:o_four]
    kr = zkr * cos + rot_half(zkr) * sin
    for hh in range(N_HEADS):
        s = slice(hh * HEAD_PAD, (hh + 1) * HEAD_PAD)
        k_ref[hh] = (kn[:, s] + kr).astype(BF16)
    vt_ref[...] = lax.dot_general(wuvt_ref[...], ckvn, NT_DIMS,
                                  preferred_element_type=F32).astype(BF16)
    zf = z[:, o_four:].astype(BF16)
    for gi in range(FOURIER_GROUPS):
        s = slice(gi * FOURIER_GROUP_CH, (gi + 1) * FOURIER_GROUP_CH)
        r = _dot(zf[:, s], dftc_ref[...])
        gc_ref[:, s] = r[:, :FOURIER_GROUP_CH]
        gs_ref[:, s] = r[:, FOURIER_GROUP_CH:]


def _mix_in_call(hs, modr, layer, g_mix, wl, tabs, dims):
    bsz, seq, lc, tm = dims["B"], dims["S"], dims["Lc"], dims["tm"]
    n_rows = bsz * (seq + lc)
    tpb = seq // tm
    n_lat = bsz * tpb
    n_tiles = n_rows // tm
    row_fn = dims["row_fn"]

    def tab_idx(i):
        return (jnp.where(i < n_lat, i % tpb, tpb), 0)

    def g_idx(i):
        return (jnp.where(i < n_lat, i % tpb, tpb), jnp.where(i < n_lat, i // tpb, i - n_lat))

    in_specs = [pl.BlockSpec((tm, D_MODEL), lambda i: (i, 0)),
                _mod_spec(layer, 3, row_fn), _mod_spec(layer, 4, row_fn),
                _const_spec(g_mix.shape),
                _layer_spec((D_MODEL, MIX_WIDTH), layer),
                _layer_spec((D_MODEL, HEAD_PAD), layer),
                _const_spec(wl["g_q"].shape),
                _layer_spec((Q_LORA, N_HEADS * HEAD_PAD), layer),
                _const_spec(wl["g_kv"].shape),
                _layer_spec((KV_LORA, N_HEADS * HEAD_PAD), layer),
                _layer_spec((N_HEADS * V_DIM, KV_LORA), layer),
                pl.BlockSpec((tm, HEAD_PAD), tab_idx), pl.BlockSpec((tm, HEAD_PAD), tab_idx),
                _const_spec((FOURIER_GROUP_CH, 2 * FOURIER_GROUP_CH))]
    out_specs = [pl.BlockSpec((N_HEADS, tm, HEAD_PAD), lambda i: (0, i, 0)),
                 pl.BlockSpec((N_HEADS, tm, HEAD_PAD), lambda i: (0, i, 0)),
                 pl.BlockSpec((N_HEADS * V_DIM, tm), lambda i: (0, i)),
                 pl.BlockSpec((tm, CONV_CH), lambda i: (i, 0)),
                 pl.BlockSpec((tm, FOURIER_WIDTH), g_idx),
                 pl.BlockSpec((tm, FOURIER_WIDTH), g_idx)]
    out_shape = [jax.ShapeDtypeStruct((N_HEADS, n_rows, HEAD_PAD), BF16),
                 jax.ShapeDtypeStruct((N_HEADS, n_rows, HEAD_PAD), BF16),
                 jax.ShapeDtypeStruct((N_HEADS * V_DIM, n_rows), BF16),
                 jax.ShapeDtypeStruct((n_rows, CONV_CH), F32),
                 jax.ShapeDtypeStruct((seq + lc, bsz * FOURIER_WIDTH), F32),
                 jax.ShapeDtypeStruct((seq + lc, bsz * FOURIER_WIDTH), F32)]
    return pl.pallas_call(
        functools.partial(_mix_in_kernel, layer=layer),
        grid=(n_tiles,),
        in_specs=in_specs,
        out_specs=out_specs,
        out_shape=out_shape,
        compiler_params=_params(),
        name="mix_in",
    )(hs, modr, modr, g_mix, wl["w_in"], wl["w_kr"], wl["g_q"], wl["w_uq_a"],
      wl["g_kv"], wl["w_uk"], wl["w_uvt"], tabs["cos"], tabs["sin"], tabs["dftc"])


def _attn_kernel(q_ref, kc_ref, vc_ref, kl_ref, vl_ref, o_ref, *, tk):
    segs = [(kc_ref, vc_ref, 0, kc_ref.shape[0])]
    segs += [(kl_ref, vl_ref, s0, tk) for s0 in range(0, kl_ref.shape[0], tk)]
    n_blk = q_ref.shape[0] // QCOL
    cols = [slice(i * QCOL, (i + 1) * QCOL) for i in range(n_blk)]
    qs = [q_ref[c, :] for c in cols]

    def scores(j, i):
        kref, _, s0, size = segs[j]
        return lax.dot_general(kref[s0:s0 + size, :], qs[i], NT_DIMS,
                               preferred_element_type=F32)

    m = [None] * n_blk
    acc = [None] * n_blk
    s_cur = [scores(0, i) for i in range(n_blk)]
    for j, (_, vref, s0, size) in enumerate(segs):
        v_aug = jnp.concatenate([vref[:, s0:s0 + size], jnp.ones((DENOM_ROWS, size), BF16)], axis=0)
        s_next = [None] * n_blk
        for i in range(n_blk):
            if j + 1 < len(segs):
                s_next[i] = scores(j + 1, i)
            smax = jnp.max(s_cur[i], axis=0, keepdims=True)
            m_new = smax if m[i] is None else jnp.maximum(m[i], smax)
            p = jnp.exp2(s_cur[i] - m_new).astype(BF16)
            pv = _dot(v_aug, p)
            acc[i] = pv if m[i] is None else jnp.exp2(m[i] - m_new) * acc[i] + pv
            m[i] = m_new
        s_cur = s_next
    for i in range(n_blk):
        o_ref[:, cols[i]] = (acc[i][:V_DIM] / acc[i][V_DIM:V_DIM + 1]).astype(BF16)


def _attn_ctx_kernel(q_ref, k_ref, vt_ref, o_ref):
    ones = jnp.ones((DENOM_ROWS, k_ref.shape[1]), BF16)
    for hh in range(N_HEADS):
        rows = slice(hh * V_DIM, (hh + 1) * V_DIM)
        s = lax.dot_general(k_ref[hh], q_ref[hh], NT_DIMS, preferred_element_type=F32)
        p = jnp.exp2(s - jnp.max(s, axis=0, keepdims=True)).astype(BF16)
        pv = _dot(jnp.concatenate([vt_ref[rows, :], ones], axis=0), p)
        o_ref[rows, :] = (pv[:V_DIM] / pv[V_DIM:V_DIM + 1]).astype(BF16)


def _attn_call(q, k, vt, dims):
    bsz, seq, lc, tq = dims["B"], dims["S"], dims["Lc"], dims["tq"]
    ctx_blk0 = (bsz * seq) // lc
    nq = seq // tq
    return pl.pallas_call(
        functools.partial(_attn_kernel, tk=dims["tk"]),
        grid=(bsz, N_HEADS, nq),
        in_specs=[pl.BlockSpec((None, tq, HEAD_PAD), lambda b, h, i: (h, b * nq + i, 0)),
                  pl.BlockSpec((None, lc, HEAD_PAD), lambda b, h, i: (h, ctx_blk0 + b, 0)),
                  pl.BlockSpec((V_DIM, lc), lambda b, h, i: (h, ctx_blk0 + b)),
                  pl.BlockSpec((None, seq, HEAD_PAD), lambda b, h, i: (h, b, 0)),
                  pl.BlockSpec((V_DIM, seq), lambda b, h, i: (h, b))],
        out_specs=pl.BlockSpec((V_DIM, tq), lambda b, h, i: (h, b * nq + i)),
        out_shape=jax.ShapeDtypeStruct((N_HEADS * V_DIM, bsz * seq), BF16),
        compiler_params=_params(3),
        name="attn_lat",
    )(q, k, vt, k, vt)


def _attn_ctx_call(q, k, vt, dims):
    bsz, seq, lc = dims["B"], dims["S"], dims["Lc"]
    ctx_blk0 = (bsz * seq) // lc
    heads = pl.BlockSpec((N_HEADS, lc, HEAD_PAD), lambda b: (0, ctx_blk0 + b, 0))
    return pl.pallas_call(
        _attn_ctx_kernel,
        grid=(bsz,),
        in_specs=[heads, heads, pl.BlockSpec((N_HEADS * V_DIM, lc), lambda b: (0, ctx_blk0 + b))],
        out_specs=pl.BlockSpec((N_HEADS * V_DIM, lc), lambda b: (0, b)),
        out_shape=jax.ShapeDtypeStruct((N_HEADS * V_DIM, bsz * lc), BF16),
        compiler_params=_params(),
        name="attn_ctx",
    )(q, k, vt)


def _fft1_kernel(gc_ref, gs_ref, m_ref, zr_ref, zi_ref):
    n1 = gc_ref.shape[0]
    for j in range(FFT_GROUP):
        x = jnp.concatenate([gc_ref[:, j, :], gs_ref[:, j, :]], axis=0).astype(BF16)
        z = _dot(m_ref[j], x)
        zr_ref[:, j, :] = z[:n1]
        zi_ref[:, j, :] = z[n1:]


def _fft2_kernel(zr_ref, zi_ref, w_ref, f_ref):
    for j in range(FFT_GROUP):
        z = jnp.concatenate([zr_ref[j], zi_ref[j]], axis=0).astype(BF16)
        f_ref[:, j, :] = _dot(w_ref[...], z)


def _fft_tables(seq):
    n1 = int(round(seq ** 0.5))
    n2 = seq // n1
    assert n1 * n2 == seq and n1 % FFT_GROUP == 0 and n2 % FFT_GROUP == 0
    k1 = np.arange(n1)[None, :, None]
    t1 = np.arange(n1)[None, None, :]
    t2 = np.arange(n2)[:, None, None]
    ang = -2.0 * np.pi * (((t1 * k1) % n1) / n1 + ((t2 * k1) % seq) / seq)
    ar = np.cos(ang) / np.sqrt(n1)
    ai = np.sin(ang) / np.sqrt(n1)
    m1 = np.concatenate([np.concatenate([ar, ai], axis=2),
                         np.concatenate([ai, -ar], axis=2)], axis=1)
    k2 = np.arange(n2)[:, None]
    tt = np.arange(n2)[None, :]
    a2 = 2.0 * np.pi * ((k2 * tt) % n2) / n2
    w2 = np.concatenate([np.cos(a2), np.sin(a2)], axis=1) / np.sqrt(n2)
    return n1, n2, jnp.asarray(m1, F32).astype(BF16), jnp.asarray(w2, F32).astype(BF16)


def _fourier_lat_call(gc, gs, dims):
    bsz, seq = dims["B"], dims["S"]
    n1, n2, m1, w2 = _fft_tables(seq)
    ncol = bsz * FOURIER_WIDTH
    assert gc.shape[0] % n2 == 0
    gc3 = gc.reshape(gc.shape[0] // n2, n2, ncol)
    gs3 = gs.reshape(gs.shape[0] // n2, n2, ncol)
    zshape = jax.ShapeDtypeStruct((n1, n2, ncol), F32)
    strided = pl.BlockSpec((n1, FFT_GROUP, ncol), lambda j: (0, j, 0))
    zr, zi = pl.pallas_call(
        _fft1_kernel,
        grid=(n2 // FFT_GROUP,),
        in_specs=[strided, strided,
                  pl.BlockSpec((FFT_GROUP, 2 * n1, 2 * n1), lambda j: (j, 0, 0))],
        out_specs=[strided, strided],
        out_shape=[zshape, zshape],
        compiler_params=_params(),
        name="fft_stage1",
    )(gc3, gs3, m1)
    slab = pl.BlockSpec((FFT_GROUP, n2, ncol), lambda j: (j, 0, 0))
    f3 = pl.pallas_call(
        _fft2_kernel,
        grid=(n1 // FFT_GROUP,),
        in_specs=[slab, slab, _const_spec((n2, 2 * n2))],
        out_specs=pl.BlockSpec((n2, FFT_GROUP, ncol), lambda j: (0, j, 0)),
        out_shape=jax.ShapeDtypeStruct((n2, n1, ncol), F32),
        compiler_params=_params(),
        name="fft_stage2",
    )(zr, zi, w2)
    return f3.reshape(seq, ncol)


def _dft_ctx_kernel(gc_ref, gs_ref, c_ref, s_ref, f_ref):
    f_ref[...] = (_dot(c_ref[...], gc_ref[...].astype(BF16))
                  - _dot(s_ref[...], gs_ref[...].astype(BF16)))


def _fourier_ctx_call(gc, gs, dims):
    bsz, lc = dims["B"], dims["Lc"]
    ncol = bsz * FOURIER_WIDTH
    kt = (np.arange(lc)[:, None] * np.arange(lc)[None, :]) % lc
    ang = 2.0 * np.pi * kt / lc
    cm = jnp.asarray(np.cos(ang) / np.sqrt(lc), F32).astype(BF16)
    sm = jnp.asarray(np.sin(ang) / np.sqrt(lc), F32).astype(BF16)
    ctx_cols = pl.BlockSpec((lc, ncol), lambda j: (dims["S"] // lc, 0))
    return pl.pallas_call(
        _dft_ctx_kernel,
        grid=(1,),
        in_specs=[ctx_cols, ctx_cols, _const_spec((lc, lc)), _const_spec((lc, lc))],
        out_specs=pl.BlockSpec((lc, ncol), lambda j: (0, 0)),
        out_shape=jax.ShapeDtypeStruct((lc, ncol), F32),
        compiler_params=_params(),
        name="dft_ctx",
    )(gc, gs, cm, sm)


def _merge_kernel(*refs, with_ctx, n_lat, tpb, tm, layer):
    refs = list(refs)
    xs_ref = refs.pop()
    xpad_ref = refs.pop()
    o_ref = refs.pop()
    it = iter(refs)
    h_ref, sh_ref, sc_ref, gt_ref, g_ref, wbg_ref, bbg_ref = [next(it) for _ in range(7)]
    vc_ref, vp_ref, vn_ref, wdw_ref, bdw_ref, lng_ref, lnb_ref, wpw_ref, bpw_ref = [
        next(it) for _ in range(9)]
    otl_ref = next(it)
    otc_ref = next(it) if with_ctx else None
    wo_ref = next(it)
    fl_ref = next(it)
    fc_ref = next(it) if with_ctx else None
    wf_ref, bf_ref, wout_ref = [next(it) for _ in range(3)]
    g_ref, bbg_ref, bdw_ref, lng_ref, lnb_ref, bpw_ref, bf_ref = (
        r.at[pl.ds(layer, 1)] for r in (g_ref, bbg_ref, bdw_ref, lng_ref, lnb_ref, bpw_ref, bf_ref))

    i = pl.program_id(0)
    pos = i % tpb
    first = pos == 0
    last = pos == tpb - 1
    if with_ctx:
        is_lat = i < n_lat
        first = jnp.logical_or(first, jnp.logical_not(is_lat))
        last = jnp.logical_or(last, jnp.logical_not(is_lat))

    h = h_ref[...]
    u = (_rms(h, g_ref[...]) * (1.0 + sc_ref[...]) + sh_ref[...]).astype(BF16)

    xpad_ref[0:HALO, :] = jnp.where(first, 0.0, vp_ref[...])
    xpad_ref[HALO:HALO + tm, :] = vc_ref[...]
    xpad_ref[HALO + tm:, :] = jnp.where(last, 0.0, vn_ref[...])
    base = HALO - CONV_K // 2
    n_shift_rows = xs_ref.shape[1]
    for r in range(SUBLANES):
        xs_ref[r] = xpad_ref[pl.ds(r, n_shift_rows), :]

    def conv_lanes(c0):
        acc = None
        for kk in range(CONV_K):
            off = base + kk
            a0 = (off // SUBLANES) * SUBLANES
            t = (xs_ref[off % SUBLANES, a0:a0 + tm, c0:c0 + LANES]
                 * wdw_ref[kk:kk + 1, c0:c0 + LANES])
            acc = t if acc is None else acc + t
        return acc + bdw_ref[:, c0:c0 + LANES]

    ot = otl_ref[...]
    fo = fl_ref[...]
    if with_ctx:
        ot = jnp.where(is_lat, ot, otc_ref[...])
        fo = jnp.where(is_lat, fo, fc_ref[...])
    assert CONV_CH == N_BRANCH * LANES
    gate_pre = []
    conv_parts = []
    for br in range(N_BRANCH):
        cols = slice(br * D_MODEL, (br + 1) * D_MODEL)
        gate_pre.append(_dot(u, wbg_ref[:, cols]) + bbg_ref[:, cols])
        conv_parts.append(conv_lanes(br * LANES))
    y_mla = lax.dot_general(ot, wo_ref[...], TN_DIMS, preferred_element_type=F32)
    y_four = _dot(fo.astype(BF16), wf_ref[...]) + bf_ref[...]
    conv = jnp.concatenate(conv_parts, axis=-1)
    mu = jnp.mean(conv, axis=-1, keepdims=True)
    cen = conv - mu
    var = jnp.mean(cen * cen, axis=-1, keepdims=True)
    ln = cen * lax.rsqrt(var + EPS) * lng_ref[...] + lnb_ref[...]
    y_conv = _dot((ln * _sigmoid(ln)).astype(BF16), wpw_ref[...]) + bpw_ref[...]

    mix = (_sigmoid(gate_pre[0]) * y_conv + _sigmoid(gate_pre[1]) * y_mla
           + _sigmoid(gate_pre[2]) * y_four)
    y = _dot(mix.astype(BF16), wout_ref[...])
    o_ref[...] = h + gt_ref[...] * y


def _merge_call(hs, modr, layer, g_mix, wl, vconv, ot_lat, ot_ctx, f_lat, f_ctx, dims, *, with_ctx):
    bsz, seq, lc, tm = dims["B"], dims["S"], dims["Lc"], dims["tm"]
    tpb = seq // tm
    n_lat = bsz * tpb
    n_tiles = n_lat + (bsz * lc // tm if with_ctx else 0)
    row_fn = dims["row_fn"]
    hpt = tm // HALO
    n_halo = vconv.shape[0] // HALO

    in_specs = [pl.BlockSpec((tm, D_MODEL), lambda i: (i, 0)),
                _mod_spec(layer, 3, row_fn), _mod_spec(layer, 4, row_fn),
                _mod_spec(layer, 5, row_fn),
                _const_spec(g_mix.shape),
                _layer_spec((D_MODEL, N_BRANCH * D_MODEL), layer),
                _const_spec(wl["b_bg"].shape),
                pl.BlockSpec((tm, CONV_CH), lambda i: (i, 0)),
                pl.BlockSpec((HALO, CONV_CH), lambda i: (jnp.maximum(i * hpt - 1, 0), 0)),
                pl.BlockSpec((HALO, CONV_CH), lambda i: (jnp.minimum((i + 1) * hpt, n_halo - 1), 0)),
                _layer_spec((CONV_K + 1, CONV_CH), layer),
                _const_spec(wl["b_dw"].shape), _const_spec(wl["ln_g"].shape),
                _const_spec(wl["ln_b"].shape),
                _layer_spec((CONV_CH, D_MODEL), layer), _const_spec(wl["b_pw"].shape),
                pl.BlockSpec((N_HEADS * V_DIM, tm), lambda i: (0, jnp.minimum(i, n_lat - 1)))]
    args = [hs, modr, modr, modr, g_mix, wl["w_bg"], wl["b_bg"],
            vconv, vconv, vconv, wl["w_dw"], wl["b_dw"], wl["ln_g"], wl["ln_b"], wl["w_pw"],
            wl["b_pw"], ot_lat]
    if with_ctx:
        in_specs.append(pl.BlockSpec((N_HEADS * V_DIM, tm), lambda i: (0, jnp.maximum(i - n_lat, 0))))
        args.append(ot_ctx)
    in_specs.append(_layer_spec((N_HEADS * V_DIM, D_MODEL), layer))
    args.append(wl["w_o"])
    in_specs.append(pl.BlockSpec(
        (tm, FOURIER_WIDTH),
        lambda i: (jnp.where(i < n_lat, i % tpb, 0), jnp.where(i < n_lat, i // tpb, 0))))
    args.append(f_lat)
    if with_ctx:
        in_specs.append(pl.BlockSpec((tm, FOURIER_WIDTH), lambda i: (0, jnp.maximum(i - n_lat, 0))))
        args.append(f_ctx)
    in_specs += [_layer_spec((FOURIER_WIDTH, D_MODEL), layer), _const_spec(wl["b_f"].shape),
                 _layer_spec((D_MODEL, D_MODEL), layer)]
    args += [wl["w_f"], wl["b_f"], wl["w_out"]]
    return pl.pallas_call(
        functools.partial(_merge_kernel, with_ctx=with_ctx, n_lat=n_lat, tpb=tpb, tm=tm, layer=layer),
        grid=(n_tiles,),
        in_specs=in_specs,
        out_specs=pl.BlockSpec((tm, D_MODEL), lambda i: (i, 0)),
        out_shape=jax.ShapeDtypeStruct((n_tiles * tm, D_MODEL), F32),
        scratch_shapes=[pltpu.VMEM((tm + 2 * HALO, CONV_CH), F32),
                        pltpu.VMEM((SUBLANES, tm + 2 * HALO - SUBLANES, CONV_CH), F32)],
        compiler_params=_params(),
        name="merge",
    )(*args)


def _stack_weights(p):
    o1 = 2 * CONV_CH
    o2 = o1 + Q_LORA
    o3 = o2 + KV_LORA
    o4 = o3 + QK_ROPE
    w_in = p["w_in"]
    depth = w_in.shape[0]
    w_kr = w_in[:, :, o3:o4]
    zeros = lambda *shape: jnp.zeros((depth,) + shape, F32)
    tail = HEAD_PAD - QK_NOPE - QK_ROPE
    w_kr_pad = jnp.concatenate([zeros(D_MODEL, QK_NOPE), w_kr, zeros(D_MODEL, tail)], axis=-1)
    wq = p["w_uq"].reshape(depth, Q_LORA, N_HEADS, QK_NOPE + QK_ROPE)
    w_uq_a = jnp.concatenate([wq, zeros(Q_LORA, N_HEADS, tail)], axis=-1)
    wkv = p["w_ukv"].reshape(depth, KV_LORA, N_HEADS, QK_NOPE + V_DIM)
    w_uk = jnp.concatenate([wkv[..., :QK_NOPE], zeros(KV_LORA, N_HEADS, HEAD_PAD - QK_NOPE)], axis=-1)
    w_uvt = wkv[..., QK_NOPE:].reshape(depth, KV_LORA, N_HEADS * V_DIM).transpose(0, 2, 1)
    row = lambda v: v
    return {
        "w_in": w_in, "w_kr": w_kr_pad.astype(BF16),
        "g_q": row(p["g_qnorm"]), "g_kv": row(p["g_kvnorm"]),
        "w_uq_a": w_uq_a.reshape(depth, Q_LORA, N_HEADS * HEAD_PAD).astype(BF16),
        "w_uk": w_uk.reshape(depth, KV_LORA, N_HEADS * HEAD_PAD).astype(BF16),
        "w_uvt": w_uvt.astype(BF16),
        "w_bg": p["w_bgate"].astype(BF16), "b_bg": row(p["b_bgate"]),
        "w_dw": jnp.concatenate([p["w_dw"], zeros(1, CONV_CH)], axis=1),
        "b_dw": row(p["b_dw"]), "ln_g": row(p["ln_g_conv"]), "ln_b": row(p["ln_b_conv"]),
        "w_pw": p["w_pw_conv"].astype(BF16), "b_pw": row(p["b_pw_conv"]),
        "w_o": p["w_o_mla"].astype(BF16),
        "w_f": p["w_fourier"].astype(BF16), "b_f": row(p["b_fourier"]),
        "w_out": p["w_out"].astype(BF16),
    }


def _tables(seq, lc):
    rows = seq // GRID_W
    row = jnp.broadcast_to(jnp.arange(rows, dtype=F32)[:, None], (rows, GRID_W)).reshape(-1)
    col = jnp.broadcast_to(jnp.arange(GRID_W, dtype=F32)[None, :], (rows, GRID_W)).reshape(-1)
    inv = 1.0 / (ROPE_BASE ** (jnp.arange(ROPE_AXIS // 2, dtype=F32) * 2.0 / ROPE_AXIS))
    ar = row[:, None] * inv
    ac = col[:, None] * inv
    ang = jnp.concatenate([ar, ar, ac, ac], axis=-1)
    tail = HEAD_PAD - QK_NOPE - QK_ROPE
    cos = jnp.concatenate([jnp.ones((seq, QK_NOPE), F32), jnp.cos(ang), jnp.ones((seq, tail), F32)], axis=1)
    sin = jnp.concatenate([jnp.zeros((seq, QK_NOPE), F32), jnp.sin(ang), jnp.zeros((seq, tail), F32)], axis=1)
    cos = jnp.concatenate([cos, jnp.ones((lc, HEAD_PAD), F32)], axis=0)
    sin = jnp.concatenate([sin, jnp.zeros((lc, HEAD_PAD), F32)], axis=0)
    c = np.arange(FOURIER_GROUP_CH)
    a = 2.0 * np.pi * ((c[:, None] * c[None, :]) % FOURIER_GROUP_CH) / FOURIER_GROUP_CH
    dftc = np.concatenate([np.cos(a), np.sin(a)], axis=1) / np.sqrt(FOURIER_GROUP_CH)
    return {"cos": cos, "sin": sin, "dftc": jnp.asarray(dftc, F32).astype(BF16)}


def kernel(x, c, ctx, c_ctx, w_ada, b_ada, g_ffn1, w1_ffn1, w3_ffn1, w2_ffn1, g_mix, w_in, w_dw, b_dw, ln_g_conv, ln_b_conv, w_pw_conv, b_pw_conv, g_qnorm, w_uq, g_kvnorm, w_ukv, w_o_mla, w_fourier, b_fourier, w_bgate, b_bgate, w_out, g_ffn2, w1_ffn2, w3_ffn2, w2_ffn2, g_final):
    bsz, seq, _ = x.shape
    lc = ctx.shape[1]
    depth = w_ada.shape[0]
    tm = lc
    tf = 2 * tm
    assert bsz + 1 <= MOD_ROWS and seq % tf == 0 and (bsz * lc) % tf == 0 and tm % HALO == 0
    n_lat_rows = bsz * seq
    p = dict(w_in=w_in, w_dw=w_dw, b_dw=b_dw, ln_g_conv=ln_g_conv, ln_b_conv=ln_b_conv,
             w_pw_conv=w_pw_conv, b_pw_conv=b_pw_conv, g_qnorm=g_qnorm, w_uq=w_uq,
             g_kvnorm=g_kvnorm, w_ukv=w_ukv, w_o_mla=w_o_mla, w_fourier=w_fourier,
             b_fourier=b_fourier, w_bgate=w_bgate, b_bgate=b_bgate, w_out=w_out)

    def make_row_fn(tile):
        n_lat_tiles, tiles_per_seq = n_lat_rows // tile, seq // tile
        return lambda i: jnp.where(i < n_lat_tiles, i // tiles_per_seq, bsz)

    dims = {"B": bsz, "S": seq, "Lc": lc, "tm": tm, "row_fn": make_row_fn(tm),
            "tq": min(4096, seq), "tk": min(256, seq)}
    ffn_row_fn = make_row_fn(tf)
    n_ffn_all = (n_lat_rows + bsz * lc) // tf
    n_ffn_lat = n_lat_rows // tf

    cs = jnp.concatenate([c, c_ctx[None, :], jnp.zeros((MOD_ROWS - bsz - 1, D_MODEL), F32)], axis=0)
    mod = _mod_call(cs, w_ada, b_ada)
    modr = mod.reshape(depth, MOD_ROWS, N_ADA, D_MODEL).transpose(0, 2, 1, 3)
    modr = modr.reshape(depth * N_ADA * MOD_ROWS, 1, D_MODEL)
    tabs = _tables(seq, lc)

    wl = _stack_weights(p)
    g_mix_r = g_mix
    ffn1 = (g_ffn1, w1_ffn1, w3_ffn1, w2_ffn1)
    ffn2 = (g_ffn2, w1_ffn2, w3_ffn2, w2_ffn2)

    hs = None
    for l in range(depth):
        last = l == depth - 1
        if l == 0:
            hs = _ffn_call(x.reshape(n_lat_rows, D_MODEL), modr, l, 0, *ffn1, n_tiles=n_ffn_all, tm=tf,
                           row_fn=ffn_row_fn, hs2=ctx.reshape(bsz * lc, D_MODEL))
        else:
            hs = _ffn_call(hs, modr, l, 0, *ffn1, n_tiles=n_ffn_all, tm=tf, row_fn=ffn_row_fn)
        q, k, vt, vconv, gc, gs = _mix_in_call(hs, modr, l, g_mix_r, wl, tabs, dims)
        ot_lat = _attn_call(q, k, vt, dims)
        f_lat = _fourier_lat_call(gc, gs, dims)
        if last:
            ot_ctx = f_ctx = None
        else:
            ot_ctx = _attn_ctx_call(q, k, vt, dims)
            f_ctx = _fourier_ctx_call(gc, gs, dims)
        hs = _merge_call(hs, modr, l, g_mix_r, wl, vconv, ot_lat, ot_ctx, f_lat, f_ctx, dims,
                         with_ctx=not last)
        hs = _ffn_call(hs, modr, l, 6, *ffn2, n_tiles=n_ffn_lat if last else n_ffn_all, tm=tf,
                       row_fn=ffn_row_fn, final_g=g_final if last else None)
    return hs.reshape(bsz, seq, D_MODEL)
```

```python
import functools

import numpy as np
import jax
import jax.numpy as jnp
from jax import lax
from jax.experimental import pallas as pl
from jax.experimental.pallas import tpu as pltpu

D_MODEL = 1024
D_FF = 2816
N_ADA = 9
CONV_CH = 384
CONV_K = 31
N_HEADS = 8
Q_LORA = 384
KV_LORA = 256
QK_NOPE = 64
QK_ROPE = 32
V_DIM = 64
ROPE_AXIS = QK_ROPE // 2
ROPE_BASE = 10000.0
GRID_W = 64
EPS = 1e-6
HALF = 0.5
ATTN_SCALE = (QK_NOPE + QK_ROPE) ** -0.5
FOURIER_GROUPS = 4
FOURIER_GROUP_CH = 128
FOURIER_WIDTH = FOURIER_GROUPS * FOURIER_GROUP_CH
N_BRANCH = 3

LANES = 128
SUBLANES = 8
HEAD_PAD = 128
MOD_ROWS = 8
HALO = 16
FFT_GROUP = 8
DENOM_ROWS = 16
QCOL = 256
LOG2E = float(np.log2(np.e))
Q_SCALE = ATTN_SCALE * LOG2E
VMEM_LIMIT = 56 * 1024 * 1024
MIX_WIDTH = 2 * CONV_CH + Q_LORA + KV_LORA + QK_ROPE + FOURIER_WIDTH

BF16 = jnp.bfloat16
F32 = jnp.float32
NT_DIMS = (((1,), (1,)), ((), ()))
TN_DIMS = (((0,), (0,)), ((), ()))


def _sigmoid(x):
    return 1.0 / (1.0 + jnp.exp2(x * (-LOG2E)))


def _rms(x, g):
    return x * lax.rsqrt(jnp.mean(x * x, axis=-1, keepdims=True) + EPS) * g


def _dot(a, b):
    return jnp.dot(a, b, preferred_element_type=F32)


def _params(n_axes=1):
    return pltpu.CompilerParams(dimension_semantics=("arbitrary",) * n_axes,
                                vmem_limit_bytes=VMEM_LIMIT)


def _const_spec(shape):
    zeros = (0,) * len(shape)
    return pl.BlockSpec(shape, lambda *_: zeros, pipeline_mode=pl.Buffered(1))


def _layer_spec(shape, layer):
    idx = (layer,) + (0,) * len(shape)
    return pl.BlockSpec((None,) + tuple(shape), lambda *_: idx, pipeline_mode=pl.Buffered(1))


def _mod_kernel(cs_ref, w_ref, b_ref, o_ref):
    cs = cs_ref[...]
    a = (cs * _sigmoid(cs)).astype(BF16)
    o_ref[...] = _dot(a, w_ref[...].astype(BF16)) + b_ref[...]


def _mod_call(cs, w_ada, b_ada):
    n_layers = w_ada.shape[0]
    return pl.pallas_call(
        _mod_kernel,
        grid=(n_layers, N_ADA),
        in_specs=[pl.BlockSpec((MOD_ROWS, D_MODEL), lambda l, k: (0, 0)),
                  pl.BlockSpec((None, D_MODEL, D_MODEL), lambda l, k: (l, 0, k)),
                  pl.BlockSpec((None, 1, D_MODEL), lambda l, k: (l, 0, k))],
        out_specs=pl.BlockSpec((None, MOD_ROWS, D_MODEL), lambda l, k: (l, 0, k)),
        out_shape=jax.ShapeDtypeStruct((n_layers, MOD_ROWS, N_ADA * D_MODEL), F32),
        compiler_params=_params(2),
        name="ada_map",
    )(cs, w_ada, b_ada.reshape(n_layers, 1, N_ADA * D_MODEL))


def _mod_spec(layer, k, row_fn):
    base = (layer * N_ADA + k) * MOD_ROWS
    return pl.BlockSpec((None, 1, D_MODEL), lambda i: (base + row_fn(i), 0, 0))


def _ffn_kernel(*refs, chunks, final, n_first, layer):
    refs = list(refs)
    o_ref = refs.pop()
    gf_ref = refs.pop() if final else None
    h2_ref = refs.pop(1) if n_first is not None else None
    h_ref, sh_ref, sc_ref, gt_ref, g_ref, w1_ref, w3_ref, w2_ref = refs
    g_ref = g_ref.at[pl.ds(layer, 1)]
    h = h_ref[...]
    if n_first is not None:
        h = jnp.where(pl.program_id(0) < n_first, h, h2_ref[...])
    xb = (_rms(h, g_ref[...]) * (1.0 + sc_ref[...]) + sh_ref[...]).astype(BF16)
    acc = None
    off = 0
    for c in chunks:
        a = _dot(xb, w1_ref[:, off:off + c].astype(BF16))
        b = _dot(xb, w3_ref[:, off:off + c].astype(BF16))
        act = (a * _sigmoid(a) * b).astype(BF16)
        y = _dot(act, w2_ref[off:off + c, :].astype(BF16))
        acc = y if acc is None else acc + y
        off += c
    out = h + (HALF * gt_ref[...]) * acc
    if final:
        out = _rms(out, gf_ref[...])
    o_ref[...] = out


def _ffn_call(hs, modr, layer, kbase, g, w1, w3, w2, *, n_tiles, tm, row_fn, final_g=None, hs2=None):
    chunks = (512,) * (D_FF // 512) + ((D_FF % 512,) if D_FF % 512 else ())
    final = final_g is not None
    n_first = None if hs2 is None else hs.shape[0] // tm
    if hs2 is None:
        in_specs = [pl.BlockSpec((tm, D_MODEL), lambda i: (i, 0))]
        args = [hs]
    else:
        in_specs = [pl.BlockSpec((tm, D_MODEL), lambda i: (jnp.minimum(i, n_first - 1), 0)),
                    pl.BlockSpec((tm, D_MODEL), lambda i: (jnp.maximum(i - n_first, 0), 0))]
        args = [hs, hs2]
    in_specs += [_mod_spec(layer, kbase, row_fn), _mod_spec(layer, kbase + 1, row_fn),
                 _mod_spec(layer, kbase + 2, row_fn),
                 _const_spec(g.shape),
                 _layer_spec((D_MODEL, D_FF), layer), _layer_spec((D_MODEL, D_FF), layer),
                 _layer_spec((D_FF, D_MODEL), layer)]
    args += [modr, modr, modr, g, w1, w3, w2]
    if final:
        in_specs.append(_const_spec((1, D_MODEL)))
        args.append(final_g.reshape(1, D_MODEL))
    return pl.pallas_call(
        functools.partial(_ffn_kernel, chunks=chunks, final=final, n_first=n_first, layer=layer),
        grid=(n_tiles,),
        in_specs=in_specs,
        out_specs=pl.BlockSpec((tm, D_MODEL), lambda i: (i, 0)),
        out_shape=jax.ShapeDtypeStruct((n_tiles * tm, D_MODEL), F32),
        compiler_params=_params(),
        name="ffn",
    )(*args)


def _mix_in_kernel(h_ref, sh_ref, sc_ref, g_ref, win_ref, gq_ref, wqa_ref, gkv_ref,
                   wuk_ref, wuvt_ref, cos_ref, sin_ref, dftc_ref,
                   q_ref, k_ref, vt_ref, vc_ref, gc_ref, gs_ref, *, layer):
    g_ref, gq_ref, gkv_ref = (r.at[pl.ds(layer, 1)] for r in (g_ref, gq_ref, gkv_ref))
    h = h_ref[...]
    u = (_rms(h, g_ref[...]) * (1.0 + sc_ref[...]) + sh_ref[...]).astype(BF16)
    o_cq = 2 * CONV_CH
    o_ckv = o_cq + Q_LORA
    o_kr = o_ckv + KV_LORA
    z = _dot(u, win_ref[:, :o_kr + HEAD_PAD].astype(BF16))
    z_four = _dot(u, win_ref[:, o_kr + QK_ROPE:].astype(BF16))
    vc_ref[...] = z[:, :CONV_CH] * _sigmoid(z[:, CONV_CH:o_cq])
    cos = cos_ref[...]
    sin = sin_ref[...]
    lane = lax.broadcasted_iota(jnp.int32, (1, HEAD_PAD), 1)
    first_half = ((lane - QK_NOPE) % ROPE_AXIS) < (ROPE_AXIS // 2)

    def rot_half(x):
        fwd = pltpu.roll(x, ROPE_AXIS // 2, axis=1)
        bwd = pltpu.roll(x, HEAD_PAD - ROPE_AXIS // 2, axis=1)
        return jnp.where(first_half, -bwd, fwd)

    cqn = _rms(z[:, o_cq:o_ckv], gq_ref[...]).astype(BF16)
    qa = _dot(cqn, wqa_ref[...])
    for hh in range(N_HEADS):
        s = slice(hh * HEAD_PAD, (hh + 1) * HEAD_PAD)
        q_ref[hh] = ((qa[:, s] * cos + rot_half(qa[:, s]) * sin) * Q_SCALE).astype(BF16)
    ckvn = _rms(z[:, o_ckv:o_kr], gkv_ref[...]).astype(BF16)
    kn = _dot(ckvn, wuk_ref[...])
    rope_lanes = jnp.logical_and(lane >= QK_NOPE, lane < QK_NOPE + QK_ROPE)
    zkr = jnp.where(rope_lanes, pltpu.roll(z[:, o_kr:], QK_NOPE, axis=1), 0.0)
    kr = zkr * cos + rot_half(zkr) * sin
    for hh in range(N_HEADS):
        s = slice(hh * HEAD_PAD, (hh + 1) * HEAD_PAD)
        k_ref[hh] = (kn[:, s] + kr).astype(BF16)
    vt_ref[...] = lax.dot_general(wuvt_ref[...], ckvn, NT_DIMS,
                                  preferred_element_type=F32).astype(BF16)
    zf = z_four.astype(BF16)
    for gi in range(FOURIER_GROUPS):
        s = slice(gi * FOURIER_GROUP_CH, (gi + 1) * FOURIER_GROUP_CH)
        r = _dot(zf[:, s], dftc_ref[...])
        gc_ref[:, s] = r[:, :FOURIER_GROUP_CH]
        gs_ref[:, s] = r[:, FOURIER_GROUP_CH:]


def _mix_in_call(hs, modr, layer, g_mix, wl, tabs, dims):
    bsz, seq, lc, tm = dims["B"], dims["S"], dims["Lc"], dims["tm"]
    n_rows = bsz * (seq + lc)
    tpb = seq // tm
    n_lat = bsz * tpb
    n_tiles = n_rows // tm
    row_fn = dims["row_fn"]

    def tab_idx(i):
        return (jnp.where(i < n_lat, i % tpb, tpb), 0)

    def g_idx(i):
        return (jnp.where(i < n_lat, i % tpb, tpb), jnp.where(i < n_lat, i // tpb, i - n_lat))

    in_specs = [pl.BlockSpec((tm, D_MODEL), lambda i: (i, 0)),
                _mod_spec(layer, 3, row_fn), _mod_spec(layer, 4, row_fn),
                _const_spec(g_mix.shape),
                _layer_spec((D_MODEL, MIX_WIDTH), layer),
                _const_spec(wl["g_q"].shape),
                _layer_spec((Q_LORA, N_HEADS * HEAD_PAD), layer),
                _const_spec(wl["g_kv"].shape),
                _layer_spec((KV_LORA, N_HEADS * HEAD_PAD), layer),
                _layer_spec((N_HEADS * V_DIM, KV_LORA), layer),
                pl.BlockSpec((tm, HEAD_PAD), tab_idx), pl.BlockSpec((tm, HEAD_PAD), tab_idx),
                _const_spec((FOURIER_GROUP_CH, 2 * FOURIER_GROUP_CH))]
    out_specs = [pl.BlockSpec((N_HEADS, tm, HEAD_PAD), lambda i: (0, i, 0)),
                 pl.BlockSpec((N_HEADS, tm, HEAD_PAD), lambda i: (0, i, 0)),
                 pl.BlockSpec((N_HEADS * V_DIM, tm), lambda i: (0, i)),
                 pl.BlockSpec((tm, CONV_CH), lambda i: (i, 0)),
                 pl.BlockSpec((tm, FOURIER_WIDTH), g_idx),
                 pl.BlockSpec((tm, FOURIER_WIDTH), g_idx)]
    out_shape = [jax.ShapeDtypeStruct((N_HEADS, n_rows, HEAD_PAD), BF16),
                 jax.ShapeDtypeStruct((N_HEADS, n_rows, HEAD_PAD), BF16),
                 jax.ShapeDtypeStruct((N_HEADS * V_DIM, n_rows), BF16),
                 jax.ShapeDtypeStruct((n_rows, CONV_CH), F32),
                 jax.ShapeDtypeStruct((seq + lc, bsz * FOURIER_WIDTH), F32),
                 jax.ShapeDtypeStruct((seq + lc, bsz * FOURIER_WIDTH), F32)]
    return pl.pallas_call(
        functools.partial(_mix_in_kernel, layer=layer),
        grid=(n_tiles,),
        in_specs=in_specs,
        out_specs=out_specs,
        out_shape=out_shape,
        compiler_params=_params(),
        name="mix_in",
    )(hs, modr, modr, g_mix, wl["w_in"], wl["g_q"], wl["w_uq_a"],
      wl["g_kv"], wl["w_uk"], wl["w_uvt"], tabs["cos"], tabs["sin"], tabs["dftc"])


def _attn_kernel(q_ref, kc_ref, vc_ref, kl_ref, vl_ref, o_ref, *, tk):
    segs = [(kc_ref, vc_ref, 0, kc_ref.shape[0])]
    segs += [(kl_ref, vl_ref, s0, tk) for s0 in range(0, kl_ref.shape[0], tk)]
    n_blk = q_ref.shape[0] // QCOL
    cols = [slice(i * QCOL, (i + 1) * QCOL) for i in range(n_blk)]
    qs = [q_ref[c, :] for c in cols]

    def scores(j, i):
        kref, _, s0, size = segs[j]
        return lax.dot_general(kref[s0:s0 + size, :], qs[i], NT_DIMS,
                               preferred_element_type=F32)

    m = [None] * n_blk
    acc = [None] * n_blk
    s_cur = [scores(0, i) for i in range(n_blk)]
    for j, (_, vref, s0, size) in enumerate(segs):
        v_aug = jnp.concatenate([vref[:, s0:s0 + size], jnp.ones((DENOM_ROWS, size), BF16)], axis=0)
        s_next = [None] * n_blk
        for i in range(n_blk):
            if j + 1 < len(segs):
                s_next[i] = scores(j + 1, i)
            smax = jnp.max(s_cur[i], axis=0, keepdims=True)
            m_new = smax if m[i] is None else jnp.maximum(m[i], smax)
            p = jnp.exp2(s_cur[i] - m_new).astype(BF16)
            pv = _dot(v_aug, p)
            acc[i] = pv if m[i] is None else jnp.exp2(m[i] - m_new) * acc[i] + pv
            m[i] = m_new
        s_cur = s_next
    for i in range(n_blk):
        o_ref[:, cols[i]] = (acc[i][:V_DIM] / acc[i][V_DIM:V_DIM + 1]).astype(BF16)


def _attn_ctx_kernel(q_ref, k_ref, vt_ref, o_ref):
    ones = jnp.ones((DENOM_ROWS, k_ref.shape[1]), BF16)
    for hh in range(N_HEADS):
        rows = slice(hh * V_DIM, (hh + 1) * V_DIM)
        s = lax.dot_general(k_ref[hh], q_ref[hh], NT_DIMS, preferred_element_type=F32)
        p = jnp.exp2(s - jnp.max(s, axis=0, keepdims=True)).astype(BF16)
        pv = _dot(jnp.concatenate([vt_ref[rows, :], ones], axis=0), p)
        o_ref[rows, :] = (pv[:V_DIM] / pv[V_DIM:V_DIM + 1]).astype(BF16)


def _attn_call(q, k, vt, dims):
    bsz, seq, lc, tq = dims["B"], dims["S"], dims["Lc"], dims["tq"]
    ctx_blk0 = (bsz * seq) // lc
    nq = seq // tq
    return pl.pallas_call(
        functools.partial(_attn_kernel, tk=dims["tk"]),
        grid=(bsz, N_HEADS, nq),
        in_specs=[pl.BlockSpec((None, tq, HEAD_PAD), lambda b, h, i: (h, b * nq + i, 0)),
                  pl.BlockSpec((None, lc, HEAD_PAD), lambda b, h, i: (h, ctx_blk0 + b, 0)),
                  pl.BlockSpec((V_DIM, lc), lambda b, h, i: (h, ctx_blk0 + b)),
                  pl.BlockSpec((None, seq, HEAD_PAD), lambda b, h, i: (h, b, 0)),
                  pl.BlockSpec((V_DIM, seq), lambda b, h, i: (h, b))],
        out_specs=pl.BlockSpec((V_DIM, tq), lambda b, h, i: (h, b * nq + i)),
        out_shape=jax.ShapeDtypeStruct((N_HEADS * V_DIM, bsz * seq), BF16),
        compiler_params=_params(3),
        name="attn_lat",
    )(q, k, vt, k, vt)


def _attn_ctx_call(q, k, vt, dims):
    bsz, seq, lc = dims["B"], dims["S"], dims["Lc"]
    ctx_blk0 = (bsz * seq) // lc
    heads = pl.BlockSpec((N_HEADS, lc, HEAD_PAD), lambda b: (0, ctx_blk0 + b, 0))
    return pl.pallas_call(
        _attn_ctx_kernel,
        grid=(bsz,),
        in_specs=[heads, heads, pl.BlockSpec((N_HEADS * V_DIM, lc), lambda b: (0, ctx_blk0 + b))],
        out_specs=pl.BlockSpec((N_HEADS * V_DIM, lc), lambda b: (0, b)),
        out_shape=jax.ShapeDtypeStruct((N_HEADS * V_DIM, bsz * lc), BF16),
        compiler_params=_params(),
        name="attn_ctx",
    )(q, k, vt)


def _fft1_kernel(gc_ref, gs_ref, m_ref, zr_ref, zi_ref):
    n1 = gc_ref.shape[0]
    for j in range(FFT_GROUP):
        x = jnp.concatenate([gc_ref[:, j, :], gs_ref[:, j, :]], axis=0).astype(BF16)
        z = _dot(m_ref[j], x)
        zr_ref[:, j, :] = z[:n1]
        zi_ref[:, j, :] = z[n1:]


def _fft2_kernel(zr_ref, zi_ref, w_ref, f_ref):
    for j in range(FFT_GROUP):
        z = jnp.concatenate([zr_ref[j], zi_ref[j]], axis=0).astype(BF16)
        f_ref[:, j, :] = _dot(w_ref[...], z)


def _fft_tables(seq):
    n1 = int(round(seq ** 0.5))
    n2 = seq // n1
    assert n1 * n2 == seq and n1 % FFT_GROUP == 0 and n2 % FFT_GROUP == 0
    k1 = np.arange(n1)[None, :, None]
    t1 = np.arange(n1)[None, None, :]
    t2 = np.arange(n2)[:, None, None]
    ang = -2.0 * np.pi * (((t1 * k1) % n1) / n1 + ((t2 * k1) % seq) / seq)
    ar = np.cos(ang) / np.sqrt(n1)
    ai = np.sin(ang) / np.sqrt(n1)
    m1 = np.concatenate([np.concatenate([ar, ai], axis=2),
                         np.concatenate([ai, -ar], axis=2)], axis=1)
    k2 = np.arange(n2)[:, None]
    tt = np.arange(n2)[None, :]
    a2 = 2.0 * np.pi * ((k2 * tt) % n2) / n2
    w2 = np.concatenate([np.cos(a2), np.sin(a2)], axis=1) / np.sqrt(n2)
    return n1, n2, jnp.asarray(m1, F32).astype(BF16), jnp.asarray(w2, F32).astype(BF16)


def _fourier_lat_call(gc, gs, dims):
    bsz, seq = dims["B"], dims["S"]
    n1, n2, m1, w2 = _fft_tables(seq)
    ncol = bsz * FOURIER_WIDTH
    assert gc.shape[0] % n2 == 0
    gc3 = gc.reshape(gc.shape[0] // n2, n2, ncol)
    gs3 = gs.reshape(gs.shape[0] // n2, n2, ncol)
    zshape = jax.ShapeDtypeStruct((n1, n2, ncol), F32)
    strided = pl.BlockSpec((n1, FFT_GROUP, ncol), lambda j: (0, j, 0))
    zr, zi = pl.pallas_call(
        _fft1_kernel,
        grid=(n2 // FFT_GROUP,),
        in_specs=[strided, strided,
                  pl.BlockSpec((FFT_GROUP, 2 * n1, 2 * n1), lambda j: (j, 0, 0))],
        out_specs=[strided, strided],
        out_shape=[zshape, zshape],
        compiler_params=_params(),
        name="fft_stage1",
    )(gc3, gs3, m1)
    slab = pl.BlockSpec((FFT_GROUP, n2, ncol), lambda j: (j, 0, 0))
    f3 = pl.pallas_call(
        _fft2_kernel,
        grid=(n1 // FFT_GROUP,),
        in_specs=[slab, slab, _const_spec((n2, 2 * n2))],
        out_specs=pl.BlockSpec((n2, FFT_GROUP, ncol), lambda j: (0, j, 0)),
        out_shape=jax.ShapeDtypeStruct((n2, n1, ncol), F32),
        compiler_params=_params(),
        name="fft_stage2",
    )(zr, zi, w2)
    return f3.reshape(seq, ncol)


def _dft_ctx_kernel(gc_ref, gs_ref, c_ref, s_ref, f_ref):
    f_ref[...] = (_dot(c_ref[...], gc_ref[...].astype(BF16))
                  - _dot(s_ref[...], gs_ref[...].astype(BF16)))


def _fourier_ctx_call(gc, gs, dims):
    bsz, lc = dims["B"], dims["Lc"]
    ncol = bsz * FOURIER_WIDTH
    kt = (np.arange(lc)[:, None] * np.arange(lc)[None, :]) % lc
    ang = 2.0 * np.pi * kt / lc
    cm = jnp.asarray(np.cos(ang) / np.sqrt(lc), F32).astype(BF16)
    sm = jnp.asarray(np.sin(ang) / np.sqrt(lc), F32).astype(BF16)
    ctx_cols = pl.BlockSpec((lc, ncol), lambda j: (dims["S"] // lc, 0))
    return pl.pallas_call(
        _dft_ctx_kernel,
        grid=(1,),
        in_specs=[ctx_cols, ctx_cols, _const_spec((lc, lc)), _const_spec((lc, lc))],
        out_specs=pl.BlockSpec((lc, ncol), lambda j: (0, 0)),
        out_shape=jax.ShapeDtypeStruct((lc, ncol), F32),
        compiler_params=_params(),
        name="dft_ctx",
    )(gc, gs, cm, sm)


def _merge_kernel(*refs, with_ctx, n_lat, tpb, tm, layer):
    refs = list(refs)
    xs_ref = refs.pop()
    xpad_ref = refs.pop()
    o_ref = refs.pop()
    it = iter(refs)
    h_ref, sh_ref, sc_ref, gt_ref, g_ref, wbg_ref, bbg_ref = [next(it) for _ in range(7)]
    vc_ref, vp_ref, vn_ref, wdw_ref, bdw_ref, lng_ref, lnb_ref, wpw_ref, bpw_ref = [
        next(it) for _ in range(9)]
    otl_ref = next(it)
    otc_ref = next(it) if with_ctx else None
    wo_ref = next(it)
    fl_ref = next(it)
    fc_ref = next(it) if with_ctx else None
    wf_ref, bf_ref, wout_ref = [next(it) for _ in range(3)]
    g_ref, bbg_ref, bdw_ref, lng_ref, lnb_ref, bpw_ref, bf_ref = (
        r.at[pl.ds(layer, 1)] for r in (g_ref, bbg_ref, bdw_ref, lng_ref, lnb_ref, bpw_ref, bf_ref))

    i = pl.program_id(0)
    pos = i % tpb
    first = pos == 0
    last = pos == tpb - 1
    if with_ctx:
        is_lat = i < n_lat
        first = jnp.logical_or(first, jnp.logical_not(is_lat))
        last = jnp.logical_or(last, jnp.logical_not(is_lat))

    h = h_ref[...]
    u = (_rms(h, g_ref[...]) * (1.0 + sc_ref[...]) + sh_ref[...]).astype(BF16)

    xpad_ref[0:HALO, :] = jnp.where(first, 0.0, vp_ref[...])
    xpad_ref[HALO:HALO + tm, :] = vc_ref[...]
    xpad_ref[HALO + tm:, :] = jnp.where(last, 0.0, vn_ref[...])
    base = HALO - CONV_K // 2
    n_shift_rows = xs_ref.shape[1]
    for r in range(SUBLANES):
        xs_ref[r] = xpad_ref[pl.ds(r, n_shift_rows), :]

    def conv_lanes(c0):
        acc = None
        for kk in range(CONV_K):
            off = base + kk
            a0 = (off // SUBLANES) * SUBLANES
            t = (xs_ref[off % SUBLANES, a0:a0 + tm, c0:c0 + LANES]
                 * wdw_ref[kk:kk + 1, c0:c0 + LANES])
            acc = t if acc is None else acc + t
        return acc + bdw_ref[:, c0:c0 + LANES]

    ot = otl_ref[...]
    fo = fl_ref[...]
    if with_ctx:
        ot = jnp.where(is_lat, ot, otc_ref[...])
        fo = jnp.where(is_lat, fo, fc_ref[...])
    assert CONV_CH == N_BRANCH * LANES
    gate_pre = []
    conv_parts = []
    for br in range(N_BRANCH):
        cols = slice(br * D_MODEL, (br + 1) * D_MODEL)
        gate_pre.append(_dot(u, wbg_ref[:, cols]) + bbg_ref[:, cols])
        conv_parts.append(conv_lanes(br * LANES))
    y_mla = lax.dot_general(ot, wo_ref[...], TN_DIMS, preferred_element_type=F32)
    y_four = _dot(fo.astype(BF16), wf_ref[...]) + bf_ref[...]
    conv = jnp.concatenate(conv_parts, axis=-1)
    mu = jnp.mean(conv, axis=-1, keepdims=True)
    cen = conv - mu
    var = jnp.mean(cen * cen, axis=-1, keepdims=True)
    ln = cen * lax.rsqrt(var + EPS) * lng_ref[...] + lnb_ref[...]
    y_conv = _dot((ln * _sigmoid(ln)).astype(BF16), wpw_ref[...]) + bpw_ref[...]

    mix = (_sigmoid(gate_pre[0]) * y_conv + _sigmoid(gate_pre[1]) * y_mla
           + _sigmoid(gate_pre[2]) * y_four)
    y = _dot(mix.astype(BF16), wout_ref[...])
    o_ref[...] = h + gt_ref[...] * y


def _merge_call(hs, modr, layer, g_mix, wl, vconv, ot_lat, ot_ctx, f_lat, f_ctx, dims, *, with_ctx):
    bsz, seq, lc, tm = dims["B"], dims["S"], dims["Lc"], dims["tm"]
    tpb = seq // tm
    n_lat = bsz * tpb
    n_tiles = n_lat + (bsz * lc // tm if with_ctx else 0)
    row_fn = dims["row_fn"]
    hpt = tm // HALO
    n_halo = vconv.shape[0] // HALO

    in_specs = [pl.BlockSpec((tm, D_MODEL), lambda i: (i, 0)),
                _mod_spec(layer, 3, row_fn), _mod_spec(layer, 4, row_fn),
                _mod_spec(layer, 5, row_fn),
                _const_spec(g_mix.shape),
                _layer_spec((D_MODEL, N_BRANCH * D_MODEL), layer),
                _const_spec(wl["b_bg"].shape),
                pl.BlockSpec((tm, CONV_CH), lambda i: (i, 0)),
                pl.BlockSpec((HALO, CONV_CH), lambda i: (jnp.maximum(i * hpt - 1, 0), 0)),
                pl.BlockSpec((HALO, CONV_CH), lambda i: (jnp.minimum((i + 1) * hpt, n_halo - 1), 0)),
                _layer_spec((CONV_K + 1, CONV_CH), layer),
                _const_spec(wl["b_dw"].shape), _const_spec(wl["ln_g"].shape),
                _const_spec(wl["ln_b"].shape),
                _layer_spec((CONV_CH, D_MODEL), layer), _const_spec(wl["b_pw"].shape),
                pl.BlockSpec((N_HEADS * V_DIM, tm), lambda i: (0, jnp.minimum(i, n_lat - 1)))]
    args = [hs, modr, modr, modr, g_mix, wl["w_bg"], wl["b_bg"],
            vconv, vconv, vconv, wl["w_dw"], wl["b_dw"], wl["ln_g"], wl["ln_b"], wl["w_pw"],
            wl["b_pw"], ot_lat]
    if with_ctx:
        in_specs.append(pl.BlockSpec((N_HEADS * V_DIM, tm), lambda i: (0, jnp.maximum(i - n_lat, 0))))
        args.append(ot_ctx)
    in_specs.append(_layer_spec((N_HEADS * V_DIM, D_MODEL), layer))
    args.append(wl["w_o"])
    in_specs.append(pl.BlockSpec(
        (tm, FOURIER_WIDTH),
        lambda i: (jnp.where(i < n_lat, i % tpb, 0), jnp.where(i < n_lat, i // tpb, 0))))
    args.append(f_lat)
    if with_ctx:
        in_specs.append(pl.BlockSpec((tm, FOURIER_WIDTH), lambda i: (0, jnp.maximum(i - n_lat, 0))))
        args.append(f_ctx)
    in_specs += [_layer_spec((FOURIER_WIDTH, D_MODEL), layer), _const_spec(wl["b_f"].shape),
                 _layer_spec((D_MODEL, D_MODEL), layer)]
    args += [wl["w_f"], wl["b_f"], wl["w_out"]]
    return pl.pallas_call(
        functools.partial(_merge_kernel, with_ctx=with_ctx, n_lat=n_lat, tpb=tpb, tm=tm, layer=layer),
        grid=(n_tiles,),
        in_specs=in_specs,
        out_specs=pl.BlockSpec((tm, D_MODEL), lambda i: (i, 0)),
        out_shape=jax.ShapeDtypeStruct((n_tiles * tm, D_MODEL), F32),
        scratch_shapes=[pltpu.VMEM((tm + 2 * HALO, CONV_CH), F32),
                        pltpu.VMEM((SUBLANES, tm + 2 * HALO - SUBLANES, CONV_CH), F32)],
        compiler_params=_params(),
        name="merge",
    )(*args)


def _stack_weights(p):
    w_in = p["w_in"]
    depth = w_in.shape[0]
    zeros = lambda *shape: jnp.zeros((depth,) + shape, F32)
    tail = HEAD_PAD - QK_NOPE - QK_ROPE
    wq = p["w_uq"].reshape(depth, Q_LORA, N_HEADS, QK_NOPE + QK_ROPE)
    w_uq_a = jnp.concatenate([wq, zeros(Q_LORA, N_HEADS, tail)], axis=-1)
    wkv = p["w_ukv"].reshape(depth, KV_LORA, N_HEADS, QK_NOPE + V_DIM)
    w_uk = jnp.concatenate([wkv[..., :QK_NOPE], zeros(KV_LORA, N_HEADS, HEAD_PAD - QK_NOPE)], axis=-1)
    w_uvt = wkv[..., QK_NOPE:].reshape(depth, KV_LORA, N_HEADS * V_DIM).transpose(0, 2, 1)
    row = lambda v: v
    return {
        "w_in": w_in,
        "g_q": row(p["g_qnorm"]), "g_kv": row(p["g_kvnorm"]),
        "w_uq_a": w_uq_a.reshape(depth, Q_LORA, N_HEADS * HEAD_PAD).astype(BF16),
        "w_uk": w_uk.reshape(depth, KV_LORA, N_HEADS * HEAD_PAD).astype(BF16),
        "w_uvt": w_uvt.astype(BF16),
        "w_bg": p["w_bgate"].astype(BF16), "b_bg": row(p["b_bgate"]),
        "w_dw": jnp.concatenate([p["w_dw"], zeros(1, CONV_CH)], axis=1),
        "b_dw": row(p["b_dw"]), "ln_g": row(p["ln_g_conv"]), "ln_b": row(p["ln_b_conv"]),
        "w_pw": p["w_pw_conv"].astype(BF16), "b_pw": row(p["b_pw_conv"]),
        "w_o": p["w_o_mla"].astype(BF16),
        "w_f": p["w_fourier"].astype(BF16), "b_f": row(p["b_fourier"]),
        "w_out": p["w_out"].astype(BF16),
    }


def _tables(seq, lc):
    rows = seq // GRID_W
    row = jnp.broadcast_to(jnp.arange(rows, dtype=F32)[:, None], (rows, GRID_W)).reshape(-1)
    col = jnp.broadcast_to(jnp.arange(GRID_W, dtype=F32)[None, :], (rows, GRID_W)).reshape(-1)
    inv = 1.0 / (ROPE_BASE ** (jnp.arange(ROPE_AXIS // 2, dtype=F32) * 2.0 / ROPE_AXIS))
    ar = row[:, None] * inv
    ac = col[:, None] * inv
    ang = jnp.concatenate([ar, ar, ac, ac], axis=-1)
    tail = HEAD_PAD - QK_NOPE - QK_ROPE
    cos = jnp.concatenate([jnp.ones((seq, QK_NOPE), F32), jnp.cos(ang), jnp.ones((seq, tail), F32)], axis=1)
    sin = jnp.concatenate([jnp.zeros((seq, QK_NOPE), F32), jnp.sin(ang), jnp.zeros((seq, tail), F32)], axis=1)
    cos = jnp.concatenate([cos, jnp.ones((lc, HEAD_PAD), F32)], axis=0)
    sin = jnp.concatenate([sin, jnp.zeros((lc, HEAD_PAD), F32)], axis=0)
    c = np.arange(FOURIER_GROUP_CH)
    a = 2.0 * np.pi * ((c[:, None] * c[None, :]) % FOURIER_GROUP_CH) / FOURIER_GROUP_CH
    dftc = np.concatenate([np.cos(a), np.sin(a)], axis=1) / np.sqrt(FOURIER_GROUP_CH)
    return {"cos": cos, "sin": sin, "dftc": jnp.asarray(dftc, F32).astype(BF16)}


def kernel(x, c, ctx, c_ctx, w_ada, b_ada, g_ffn1, w1_ffn1, w3_ffn1, w2_ffn1, g_mix, w_in, w_dw, b_dw, ln_g_conv, ln_b_conv, w_pw_conv, b_pw_conv, g_qnorm, w_uq, g_kvnorm, w_ukv, w_o_mla, w_fourier, b_fourier, w_bgate, b_bgate, w_out, g_ffn2, w1_ffn2, w3_ffn2, w2_ffn2, g_final):
    bsz, seq, _ = x.shape
    lc = ctx.shape[1]
    depth = w_ada.shape[0]
    tm = lc
    tf = 2 * tm
    assert bsz + 1 <= MOD_ROWS and seq % tf == 0 and (bsz * lc) % tf == 0 and tm % HALO == 0
    n_lat_rows = bsz * seq
    p = dict(w_in=w_in, w_dw=w_dw, b_dw=b_dw, ln_g_conv=ln_g_conv, ln_b_conv=ln_b_conv,
             w_pw_conv=w_pw_conv, b_pw_conv=b_pw_conv, g_qnorm=g_qnorm, w_uq=w_uq,
             g_kvnorm=g_kvnorm, w_ukv=w_ukv, w_o_mla=w_o_mla, w_fourier=w_fourier,
             b_fourier=b_fourier, w_bgate=w_bgate, b_bgate=b_bgate, w_out=w_out)

    def make_row_fn(tile):
        n_lat_tiles, tiles_per_seq = n_lat_rows // tile, seq // tile
        return lambda i: jnp.where(i < n_lat_tiles, i // tiles_per_seq, bsz)

    dims = {"B": bsz, "S": seq, "Lc": lc, "tm": tm, "row_fn": make_row_fn(tm),
            "tq": min(4096, seq), "tk": min(256, seq)}
    ffn_row_fn = make_row_fn(tf)
    n_ffn_all = (n_lat_rows + bsz * lc) // tf
    n_ffn_lat = n_lat_rows // tf

    cs = jnp.concatenate([c, c_ctx[None, :], jnp.zeros((MOD_ROWS - bsz - 1, D_MODEL), F32)], axis=0)
    mod = _mod_call(cs, w_ada, b_ada)
    modr = mod.reshape(depth, MOD_ROWS, N_ADA, D_MODEL).transpose(0, 2, 1, 3)
    modr = modr.reshape(depth * N_ADA * MOD_ROWS, 1, D_MODEL)
    tabs = _tables(seq, lc)

    wl = _stack_weights(p)
    g_mix_r = g_mix
    ffn1 = (g_ffn1, w1_ffn1, w3_ffn1, w2_ffn1)
    ffn2 = (g_ffn2, w1_ffn2, w3_ffn2, w2_ffn2)

    hs = None
    for l in range(depth):
        last = l == depth - 1
        if l == 0:
            hs = _ffn_call(x.reshape(n_lat_rows, D_MODEL), modr, l, 0, *ffn1, n_tiles=n_ffn_all, tm=tf,
                           row_fn=ffn_row_fn, hs2=ctx.reshape(bsz * lc, D_MODEL))
        else:
            hs = _ffn_call(hs, modr, l, 0, *ffn1, n_tiles=n_ffn_all, tm=tf, row_fn=ffn_row_fn)
        q, k, vt, vconv, gc, gs = _mix_in_call(hs, modr, l, g_mix_r, wl, tabs, dims)
        ot_lat = _attn_call(q, k, vt, dims)
        f_lat = _fourier_lat_call(gc, gs, dims)
        if last:
            ot_ctx = f_ctx = None
        else:
            ot_ctx = _attn_ctx_call(q, k, vt, dims)
            f_ctx = _fourier_ctx_call(gc, gs, dims)
        hs = _merge_call(hs, modr, l, g_mix_r, wl, vconv, ot_lat, ot_ctx, f_lat, f_ctx, dims,
                         with_ctx=not last)
        hs = _ffn_call(hs, modr, l, 6, *ffn2, n_tiles=n_ffn_lat if last else n_ffn_all, tm=tf,
                       row_fn=ffn_row_fn, final_g=g_final if last else None)
    return hs.reshape(bsz, seq, D_MODEL)
```

```python
import functools

import numpy as np
import jax
import jax.numpy as jnp
from jax import lax
from jax.experimental import pallas as pl
from jax.experimental.pallas import tpu as pltpu

D_MODEL = 1024
D_FF = 2816
N_ADA = 9
CONV_CH = 384
CONV_K = 31
N_HEADS = 8
Q_LORA = 384
KV_LORA = 256
QK_NOPE = 64
QK_ROPE = 32
V_DIM = 64
ROPE_AXIS = QK_ROPE // 2
ROPE_BASE = 10000.0
GRID_W = 64
EPS = 1e-6
HALF = 0.5
ATTN_SCALE = (QK_NOPE + QK_ROPE) ** -0.5
FOURIER_GROUPS = 4
FOURIER_GROUP_CH = 128
FOURIER_WIDTH = FOURIER_GROUPS * FOURIER_GROUP_CH
N_BRANCH = 3

LANES = 128
SUBLANES = 8
HEAD_PAD = 128
MOD_ROWS = 8
HALO = 16
FFT_GROUP = 8
DENOM_ROWS = 16
QCOL = 256
LOG2E = float(np.log2(np.e))
Q_SCALE = ATTN_SCALE * LOG2E
VMEM_LIMIT = 56 * 1024 * 1024
MIX_WIDTH = 2 * CONV_CH + Q_LORA + KV_LORA + QK_ROPE + FOURIER_WIDTH

BF16 = jnp.bfloat16
F32 = jnp.float32
NT_DIMS = (((1,), (1,)), ((), ()))
TN_DIMS = (((0,), (0,)), ((), ()))


def _sigmoid(x):
    return 1.0 / (1.0 + jnp.exp2(x * (-LOG2E)))


def _rms(x, g):
    return x * lax.rsqrt(jnp.mean(x * x, axis=-1, keepdims=True) + EPS) * g


def _dot(a, b):
    return jnp.dot(a, b, preferred_element_type=F32)


def _params(n_axes=1):
    return pltpu.CompilerParams(dimension_semantics=("arbitrary",) * n_axes,
                                vmem_limit_bytes=VMEM_LIMIT)


def _const_spec(shape):
    zeros = (0,) * len(shape)
    return pl.BlockSpec(shape, lambda *_: zeros, pipeline_mode=pl.Buffered(1))


def _layer_spec(shape, layer):
    idx = (layer,) + (0,) * len(shape)
    return pl.BlockSpec((None,) + tuple(shape), lambda *_: idx, pipeline_mode=pl.Buffered(1))


def _mod_kernel(cs_ref, w_ref, b_ref, o_ref):
    cs = cs_ref[...]
    a = (cs * _sigmoid(cs)).astype(BF16)
    o_ref[...] = _dot(a, w_ref[...].astype(BF16)) + b_ref[...]


def _mod_call(cs, w_ada, b_ada):
    n_layers = w_ada.shape[0]
    return pl.pallas_call(
        _mod_kernel,
        grid=(n_layers, N_ADA),
        in_specs=[pl.BlockSpec((MOD_ROWS, D_MODEL), lambda l, k: (0, 0)),
                  pl.BlockSpec((None, D_MODEL, D_MODEL), lambda l, k: (l, 0, k)),
                  pl.BlockSpec((None, 1, D_MODEL), lambda l, k: (l, 0, k))],
        out_specs=pl.BlockSpec((None, MOD_ROWS, D_MODEL), lambda l, k: (l, 0, k)),
        out_shape=jax.ShapeDtypeStruct((n_layers, MOD_ROWS, N_ADA * D_MODEL), F32),
        compiler_params=_params(2),
        name="ada_map",
    )(cs, w_ada, b_ada.reshape(n_layers, 1, N_ADA * D_MODEL))


def _mod_spec(layer, k, row_fn):
    base = (layer * N_ADA + k) * MOD_ROWS
    return pl.BlockSpec((None, 1, D_MODEL), lambda i: (base + row_fn(i), 0, 0))


def _ffn_kernel(*refs, chunks, final, n_first, layer):
    refs = list(refs)
    o_ref = refs.pop()
    gf_ref = refs.pop() if final else None
    h2_ref = refs.pop(1) if n_first is not None else None
    h_ref, sh_ref, sc_ref, gt_ref, g_ref, w1_ref, w3_ref, w2_ref = refs
    g_ref = g_ref.at[pl.ds(layer, 1)]
    h = h_ref[...]
    if n_first is not None:
        h = jnp.where(pl.program_id(0) < n_first, h, h2_ref[...])
    xb = (_rms(h, g_ref[...]) * (1.0 + sc_ref[...]) + sh_ref[...]).astype(BF16)
    acc = None
    off = 0
    for c in chunks:
        a = _dot(xb, w1_ref[:, off:off + c].astype(BF16))
        b = _dot(xb, w3_ref[:, off:off + c].astype(BF16))
        act = (a * _sigmoid(a) * b).astype(BF16)
        y = _dot(act, w2_ref[off:off + c, :].astype(BF16))
        acc = y if acc is None else acc + y
        off += c
    out = h + (HALF * gt_ref[...]) * acc
    if final:
        out = _rms(out, gf_ref[...])
    o_ref[...] = out


def _ffn_call(hs, modr, layer, kbase, g, w1, w3, w2, *, n_tiles, tm, row_fn, final_g=None, hs2=None):
    chunks = (512,) * (D_FF // 512) + ((D_FF % 512,) if D_FF % 512 else ())
    final = final_g is not None
    n_first = None if hs2 is None else hs.shape[0] // tm
    if hs2 is None:
        in_specs = [pl.BlockSpec((tm, D_MODEL), lambda i: (i, 0))]
        args = [hs]
    else:
        in_specs = [pl.BlockSpec((tm, D_MODEL), lambda i: (jnp.minimum(i, n_first - 1), 0)),
                    pl.BlockSpec((tm, D_MODEL), lambda i: (jnp.maximum(i - n_first, 0), 0))]
        args = [hs, hs2]
    in_specs += [_mod_spec(layer, kbase, row_fn), _mod_spec(layer, kbase + 1, row_fn),
                 _mod_spec(layer, kbase + 2, row_fn),
                 _const_spec(g.shape),
                 _layer_spec((D_MODEL, D_FF), layer), _layer_spec((D_MODEL, D_FF), layer),
                 _layer_spec((D_FF, D_MODEL), layer)]
    args += [modr, modr, modr, g, w1, w3, w2]
    if final:
        in_specs.append(_const_spec((1, D_MODEL)))
        args.append(final_g.reshape(1, D_MODEL))
    return pl.pallas_call(
        functools.partial(_ffn_kernel, chunks=chunks, final=final, n_first=n_first, layer=layer),
        grid=(n_tiles,),
        in_specs=in_specs,
        out_specs=pl.BlockSpec((tm, D_MODEL), lambda i: (i, 0)),
        out_shape=jax.ShapeDtypeStruct((n_tiles * tm, D_MODEL), F32),
        compiler_params=_params(),
        name="ffn",
    )(*args)


def _mix_in_kernel(h_ref, sh_ref, sc_ref, g_ref, win_ref, gq_ref, wqa_ref, gkv_ref,
                   wuk_ref, wuvt_ref, cos_ref, sin_ref, dftc_ref,
                   q_ref, k_ref, vt_ref, vc_ref, gc_ref, gs_ref, *, layer):
    g_ref, gq_ref, gkv_ref = (r.at[pl.ds(layer, 1)] for r in (g_ref, gq_ref, gkv_ref))
    h = h_ref[...]
    u = (_rms(h, g_ref[...]) * (1.0 + sc_ref[...]) + sh_ref[...]).astype(BF16)
    o_cq = 2 * CONV_CH
    o_ckv = o_cq + Q_LORA
    o_kr = o_ckv + KV_LORA
    z = lax.dot_general(u, win_ref[:o_kr + HEAD_PAD, :].astype(BF16), NT_DIMS,
                        preferred_element_type=F32)
    z_four = lax.dot_general(u, win_ref[o_kr + QK_ROPE:, :].astype(BF16), NT_DIMS,
                             preferred_element_type=F32)
    vc_ref[...] = z[:, :CONV_CH] * _sigmoid(z[:, CONV_CH:o_cq])
    cos = cos_ref[...]
    sin = sin_ref[...]
    lane = lax.broadcasted_iota(jnp.int32, (1, HEAD_PAD), 1)
    first_half = ((lane - QK_NOPE) % ROPE_AXIS) < (ROPE_AXIS // 2)

    def rot_half(x):
        fwd = pltpu.roll(x, ROPE_AXIS // 2, axis=1)
        bwd = pltpu.roll(x, HEAD_PAD - ROPE_AXIS // 2, axis=1)
        return jnp.where(first_half, -bwd, fwd)

    cqn = _rms(z[:, o_cq:o_ckv], gq_ref[...]).astype(BF16)
    qa = _dot(cqn, wqa_ref[...])
    for hh in range(N_HEADS):
        s = slice(hh * HEAD_PAD, (hh + 1) * HEAD_PAD)
        q_ref[hh] = ((qa[:, s] * cos + rot_half(qa[:, s]) * sin) * Q_SCALE).astype(BF16)
    ckvn = _rms(z[:, o_ckv:o_kr], gkv_ref[...]).astype(BF16)
    kn = _dot(ckvn, wuk_ref[...])
    rope_lanes = jnp.logical_and(lane >= QK_NOPE, lane < QK_NOPE + QK_ROPE)
    zkr = jnp.where(rope_lanes, pltpu.roll(z[:, o_kr:], QK_NOPE, axis=1), 0.0)
    kr = zkr * cos + rot_half(zkr) * sin
    for hh in range(N_HEADS):
        s = slice(hh * HEAD_PAD, (hh + 1) * HEAD_PAD)
        k_ref[hh] = (kn[:, s] + kr).astype(BF16)
    vt_ref[...] = lax.dot_general(wuvt_ref[...], ckvn, NT_DIMS,
                                  preferred_element_type=F32).astype(BF16)
    zf = z_four.astype(BF16)
    for gi in range(FOURIER_GROUPS):
        s = slice(gi * FOURIER_GROUP_CH, (gi + 1) * FOURIER_GROUP_CH)
        r = _dot(zf[:, s], dftc_ref[...])
        gc_ref[:, s] = r[:, :FOURIER_GROUP_CH]
        gs_ref[:, s] = r[:, FOURIER_GROUP_CH:]


def _mix_in_call(hs, modr, layer, g_mix, wl, tabs, dims):
    bsz, seq, lc, tm = dims["B"], dims["S"], dims["Lc"], dims["tm"]
    n_rows = bsz * (seq + lc)
    tpb = seq // tm
    n_lat = bsz * tpb
    n_tiles = n_rows // tm
    row_fn = dims["row_fn"]

    def tab_idx(i):
        return (jnp.where(i < n_lat, i % tpb, tpb), 0)

    def g_idx(i):
        return (jnp.where(i < n_lat, i % tpb, tpb), jnp.where(i < n_lat, i // tpb, i - n_lat))

    in_specs = [pl.BlockSpec((tm, D_MODEL), lambda i: (i, 0)),
                _mod_spec(layer, 3, row_fn), _mod_spec(layer, 4, row_fn),
                _const_spec(g_mix.shape),
                _layer_spec((MIX_WIDTH, D_MODEL), layer),
                _const_spec(wl["g_q"].shape),
                _layer_spec((Q_LORA, N_HEADS * HEAD_PAD), layer),
                _const_spec(wl["g_kv"].shape),
                _layer_spec((KV_LORA, N_HEADS * HEAD_PAD), layer),
                _layer_spec((N_HEADS * V_DIM, KV_LORA), layer),
                pl.BlockSpec((tm, HEAD_PAD), tab_idx), pl.BlockSpec((tm, HEAD_PAD), tab_idx),
                _const_spec((FOURIER_GROUP_CH, 2 * FOURIER_GROUP_CH))]
    out_specs = [pl.BlockSpec((N_HEADS, tm, HEAD_PAD), lambda i: (0, i, 0)),
                 pl.BlockSpec((N_HEADS, tm, HEAD_PAD), lambda i: (0, i, 0)),
                 pl.BlockSpec((N_HEADS * V_DIM, tm), lambda i: (0, i)),
                 pl.BlockSpec((tm, CONV_CH), lambda i: (i, 0)),
                 pl.BlockSpec((tm, FOURIER_WIDTH), g_idx),
                 pl.BlockSpec((tm, FOURIER_WIDTH), g_idx)]
    out_shape = [jax.ShapeDtypeStruct((N_HEADS, n_rows, HEAD_PAD), BF16),
                 jax.ShapeDtypeStruct((N_HEADS, n_rows, HEAD_PAD), BF16),
                 jax.ShapeDtypeStruct((N_HEADS * V_DIM, n_rows), BF16),
                 jax.ShapeDtypeStruct((n_rows, CONV_CH), F32),
                 jax.ShapeDtypeStruct((seq + lc, bsz * FOURIER_WIDTH), F32),
                 jax.ShapeDtypeStruct((seq + lc, bsz * FOURIER_WIDTH), F32)]
    return pl.pallas_call(
        functools.partial(_mix_in_kernel, layer=layer),
        grid=(n_tiles,),
        in_specs=in_specs,
        out_specs=out_specs,
        out_shape=out_shape,
        compiler_params=_params(),
        name="mix_in",
    )(hs, modr, modr, g_mix, wl["w_in"], wl["g_q"], wl["w_uq_a"],
      wl["g_kv"], wl["w_uk"], wl["w_uvt"], tabs["cos"], tabs["sin"], tabs["dftc"])


def _attn_kernel(q_ref, kc_ref, vc_ref, kl_ref, vl_ref, o_ref, *, tk):
    segs = [(kc_ref, vc_ref, 0, kc_ref.shape[0])]
    segs += [(kl_ref, vl_ref, s0, tk) for s0 in range(0, kl_ref.shape[0], tk)]
    n_blk = q_ref.shape[0] // QCOL
    cols = [slice(i * QCOL, (i + 1) * QCOL) for i in range(n_blk)]
    qs = [q_ref[c, :] for c in cols]

    def scores(j, i):
        kref, _, s0, size = segs[j]
        return lax.dot_general(kref[s0:s0 + size, :], qs[i], NT_DIMS,
                               preferred_element_type=F32)

    m = [None] * n_blk
    acc = [None] * n_blk
    s_cur = [scores(0, i) for i in range(n_blk)]
    for j, (_, vref, s0, size) in enumerate(segs):
        v_aug = jnp.concatenate([vref[:, s0:s0 + size], jnp.ones((DENOM_ROWS, size), BF16)], axis=0)
        s_next = [None] * n_blk
        for i in range(n_blk):
            if j + 1 < len(segs):
                s_next[i] = scores(j + 1, i)
            smax = jnp.max(s_cur[i], axis=0, keepdims=True)
            m_new = smax if m[i] is None else jnp.maximum(m[i], smax)
            p = jnp.exp2(s_cur[i] - m_new).astype(BF16)
            pv = _dot(v_aug, p)
            acc[i] = pv if m[i] is None else jnp.exp2(m[i] - m_new) * acc[i] + pv
            m[i] = m_new
        s_cur = s_next
    for i in range(n_blk):
        o_ref[:, cols[i]] = (acc[i][:V_DIM] / acc[i][V_DIM:V_DIM + 1]).astype(BF16)


def _attn_ctx_kernel(q_ref, k_ref, vt_ref, o_ref):
    ones = jnp.ones((DENOM_ROWS, k_ref.shape[1]), BF16)
    for hh in range(N_HEADS):
        rows = slice(hh * V_DIM, (hh + 1) * V_DIM)
        s = lax.dot_general(k_ref[hh], q_ref[hh], NT_DIMS, preferred_element_type=F32)
        p = jnp.exp2(s - jnp.max(s, axis=0, keepdims=True)).astype(BF16)
        pv = _dot(jnp.concatenate([vt_ref[rows, :], ones], axis=0), p)
        o_ref[rows, :] = (pv[:V_DIM] / pv[V_DIM:V_DIM + 1]).astype(BF16)


def _attn_call(q, k, vt, dims):
    bsz, seq, lc, tq = dims["B"], dims["S"], dims["Lc"], dims["tq"]
    ctx_blk0 = (bsz * seq) // lc
    nq = seq // tq
    return pl.pallas_call(
        functools.partial(_attn_kernel, tk=dims["tk"]),
        grid=(bsz, N_HEADS, nq),
        in_specs=[pl.BlockSpec((None, tq, HEAD_PAD), lambda b, h, i: (h, b * nq + i, 0)),
                  pl.BlockSpec((None, lc, HEAD_PAD), lambda b, h, i: (h, ctx_blk0 + b, 0)),
                  pl.BlockSpec((V_DIM, lc), lambda b, h, i: (h, ctx_blk0 + b)),
                  pl.BlockSpec((None, seq, HEAD_PAD), lambda b, h, i: (h, b, 0)),
                  pl.BlockSpec((V_DIM, seq), lambda b, h, i: (h, b))],
        out_specs=pl.BlockSpec((V_DIM, tq), lambda b, h, i: (h, b * nq + i)),
        out_shape=jax.ShapeDtypeStruct((N_HEADS * V_DIM, bsz * seq), BF16),
        compiler_params=_params(3),
        name="attn_lat",
    )(q, k, vt, k, vt)


def _attn_ctx_call(q, k, vt, dims):
    bsz, seq, lc = dims["B"], dims["S"], dims["Lc"]
    ctx_blk0 = (bsz * seq) // lc
    heads = pl.BlockSpec((N_HEADS, lc, HEAD_PAD), lambda b: (0, ctx_blk0 + b, 0))
    return pl.pallas_call(
        _attn_ctx_kernel,
        grid=(bsz,),
        in_specs=[heads, heads, pl.BlockSpec((N_HEADS * V_DIM, lc), lambda b: (0, ctx_blk0 + b))],
        out_specs=pl.BlockSpec((N_HEADS * V_DIM, lc), lambda b: (0, b)),
        out_shape=jax.ShapeDtypeStruct((N_HEADS * V_DIM, bsz * lc), BF16),
        compiler_params=_params(),
        name="attn_ctx",
    )(q, k, vt)


def _fft1_kernel(gc_ref, gs_ref, m_ref, zr_ref, zi_ref):
    n1 = gc_ref.shape[0]
    for j in range(FFT_GROUP):
        x = jnp.concatenate([gc_ref[:, j, :], gs_ref[:, j, :]], axis=0).astype(BF16)
        z = _dot(m_ref[j], x)
        zr_ref[:, j, :] = z[:n1]
        zi_ref[:, j, :] = z[n1:]


def _fft2_kernel(zr_ref, zi_ref, w_ref, f_ref):
    for j in range(FFT_GROUP):
        z = jnp.concatenate([zr_ref[j], zi_ref[j]], axis=0).astype(BF16)
        f_ref[:, j, :] = _dot(w_ref[...], z)


def _fft_tables(seq):
    n1 = int(round(seq ** 0.5))
    n2 = seq // n1
    assert n1 * n2 == seq and n1 % FFT_GROUP == 0 and n2 % FFT_GROUP == 0
    k1 = np.arange(n1)[None, :, None]
    t1 = np.arange(n1)[None, None, :]
    t2 = np.arange(n2)[:, None, None]
    ang = -2.0 * np.pi * (((t1 * k1) % n1) / n1 + ((t2 * k1) % seq) / seq)
    ar = np.cos(ang) / np.sqrt(n1)
    ai = np.sin(ang) / np.sqrt(n1)
    m1 = np.concatenate([np.concatenate([ar, ai], axis=2),
                         np.concatenate([ai, -ar], axis=2)], axis=1)
    k2 = np.arange(n2)[:, None]
    tt = np.arange(n2)[None, :]
    a2 = 2.0 * np.pi * ((k2 * tt) % n2) / n2
    w2 = np.concatenate([np.cos(a2), np.sin(a2)], axis=1) / np.sqrt(n2)
    return n1, n2, jnp.asarray(m1, F32).astype(BF16), jnp.asarray(w2, F32).astype(BF16)


def _fourier_lat_call(gc, gs, dims):
    bsz, seq = dims["B"], dims["S"]
    n1, n2, m1, w2 = _fft_tables(seq)
    ncol = bsz * FOURIER_WIDTH
    assert gc.shape[0] % n2 == 0
    gc3 = gc.reshape(gc.shape[0] // n2, n2, ncol)
    gs3 = gs.reshape(gs.shape[0] // n2, n2, ncol)
    zshape = jax.ShapeDtypeStruct((n1, n2, ncol), F32)
    strided = pl.BlockSpec((n1, FFT_GROUP, ncol), lambda j: (0, j, 0))
    zr, zi = pl.pallas_call(
        _fft1_kernel,
        grid=(n2 // FFT_GROUP,),
        in_specs=[strided, strided,
                  pl.BlockSpec((FFT_GROUP, 2 * n1, 2 * n1), lambda j: (j, 0, 0))],
        out_specs=[strided, strided],
        out_shape=[zshape, zshape],
        compiler_params=_params(),
        name="fft_stage1",
    )(gc3, gs3, m1)
    slab = pl.BlockSpec((FFT_GROUP, n2, ncol), lambda j: (j, 0, 0))
    f3 = pl.pallas_call(
        _fft2_kernel,
        grid=(n1 // FFT_GROUP,),
        in_specs=[slab, slab, _const_spec((n2, 2 * n2))],
        out_specs=pl.BlockSpec((n2, FFT_GROUP, ncol), lambda j: (0, j, 0)),
        out_shape=jax.ShapeDtypeStruct((n2, n1, ncol), F32),
        compiler_params=_params(),
        name="fft_stage2",
    )(zr, zi, w2)
    return f3.reshape(seq, ncol)


def _dft_ctx_kernel(gc_ref, gs_ref, c_ref, s_ref, f_ref):
    f_ref[...] = (_dot(c_ref[...], gc_ref[...].astype(BF16))
                  - _dot(s_ref[...], gs_ref[...].astype(BF16)))


def _fourier_ctx_call(gc, gs, dims):
    bsz, lc = dims["B"], dims["Lc"]
    ncol = bsz * FOURIER_WIDTH
    kt = (np.arange(lc)[:, None] * np.arange(lc)[None, :]) % lc
    ang = 2.0 * np.pi * kt / lc
    cm = jnp.asarray(np.cos(ang) / np.sqrt(lc), F32).astype(BF16)
    sm = jnp.asarray(np.sin(ang) / np.sqrt(lc), F32).astype(BF16)
    ctx_cols = pl.BlockSpec((lc, ncol), lambda j: (dims["S"] // lc, 0))
    return pl.pallas_call(
        _dft_ctx_kernel,
        grid=(1,),
        in_specs=[ctx_cols, ctx_cols, _const_spec((lc, lc)), _const_spec((lc, lc))],
        out_specs=pl.BlockSpec((lc, ncol), lambda j: (0, 0)),
        out_shape=jax.ShapeDtypeStruct((lc, ncol), F32),
        compiler_params=_params(),
        name="dft_ctx",
    )(gc, gs, cm, sm)


def _merge_kernel(*refs, with_ctx, n_lat, tpb, tm, layer):
    refs = list(refs)
    xs_ref = refs.pop()
    xpad_ref = refs.pop()
    o_ref = refs.pop()
    it = iter(refs)
    h_ref, sh_ref, sc_ref, gt_ref, g_ref, wbg_ref, bbg_ref = [next(it) for _ in range(7)]
    vc_ref, vp_ref, vn_ref, wdw_ref, bdw_ref, lng_ref, lnb_ref, wpw_ref, bpw_ref = [
        next(it) for _ in range(9)]
    otl_ref = next(it)
    otc_ref = next(it) if with_ctx else None
    wo_ref = next(it)
    fl_ref = next(it)
    fc_ref = next(it) if with_ctx else None
    wf_ref, bf_ref, wout_ref = [next(it) for _ in range(3)]
    g_ref, bbg_ref, bdw_ref, lng_ref, lnb_ref, bpw_ref, bf_ref = (
        r.at[pl.ds(layer, 1)] for r in (g_ref, bbg_ref, bdw_ref, lng_ref, lnb_ref, bpw_ref, bf_ref))

    i = pl.program_id(0)
    pos = i % tpb
    first = pos == 0
    last = pos == tpb - 1
    if with_ctx:
        is_lat = i < n_lat
        first = jnp.logical_or(first, jnp.logical_not(is_lat))
        last = jnp.logical_or(last, jnp.logical_not(is_lat))

    h = h_ref[...]
    u = (_rms(h, g_ref[...]) * (1.0 + sc_ref[...]) + sh_ref[...]).astype(BF16)

    xpad_ref[0:HALO, :] = jnp.where(first, 0.0, vp_ref[...])
    xpad_ref[HALO:HALO + tm, :] = vc_ref[...]
    xpad_ref[HALO + tm:, :] = jnp.where(last, 0.0, vn_ref[...])
    base = HALO - CONV_K // 2
    n_shift_rows = xs_ref.shape[1]
    for r in range(SUBLANES):
        xs_ref[r] = xpad_ref[pl.ds(r, n_shift_rows), :]

    def conv_lanes(c0):
        acc = None
        for kk in range(CONV_K):
            off = base + kk
            a0 = (off // SUBLANES) * SUBLANES
            t = (xs_ref[off % SUBLANES, a0:a0 + tm, c0:c0 + LANES]
                 * wdw_ref[kk:kk + 1, c0:c0 + LANES])
            acc = t if acc is None else acc + t
        return acc + bdw_ref[:, c0:c0 + LANES]

    ot = otl_ref[...]
    fo = fl_ref[...]
    if with_ctx:
        ot = jnp.where(is_lat, ot, otc_ref[...])
        fo = jnp.where(is_lat, fo, fc_ref[...])
    assert CONV_CH == N_BRANCH * LANES
    gate_pre = []
    conv_parts = []
    for br in range(N_BRANCH):
        cols = slice(br * D_MODEL, (br + 1) * D_MODEL)
        gate_pre.append(_dot(u, wbg_ref[:, cols]) + bbg_ref[:, cols])
        conv_parts.append(conv_lanes(br * LANES))
    y_mla = lax.dot_general(ot, wo_ref[...], TN_DIMS, preferred_element_type=F32)
    y_four = _dot(fo.astype(BF16), wf_ref[...]) + bf_ref[...]
    conv = jnp.concatenate(conv_parts, axis=-1)
    mu = jnp.mean(conv, axis=-1, keepdims=True)
    cen = conv - mu
    var = jnp.mean(cen * cen, axis=-1, keepdims=True)
    ln = cen * lax.rsqrt(var + EPS) * lng_ref[...] + lnb_ref[...]
    y_conv = _dot((ln * _sigmoid(ln)).astype(BF16), wpw_ref[...]) + bpw_ref[...]

    mix = (_sigmoid(gate_pre[0]) * y_conv + _sigmoid(gate_pre[1]) * y_mla
           + _sigmoid(gate_pre[2]) * y_four)
    y = _dot(mix.astype(BF16), wout_ref[...])
    o_ref[...] = h + gt_ref[...] * y


def _merge_call(hs, modr, layer, g_mix, wl, vconv, ot_lat, ot_ctx, f_lat, f_ctx, dims, *, with_ctx):
    bsz, seq, lc, tm = dims["B"], dims["S"], dims["Lc"], dims["tm"]
    tpb = seq // tm
    n_lat = bsz * tpb
    n_tiles = n_lat + (bsz * lc // tm if with_ctx else 0)
    row_fn = dims["row_fn"]
    hpt = tm // HALO
    n_halo = vconv.shape[0] // HALO

    in_specs = [pl.BlockSpec((tm, D_MODEL), lambda i: (i, 0)),
                _mod_spec(layer, 3, row_fn), _mod_spec(layer, 4, row_fn),
                _mod_spec(layer, 5, row_fn),
                _const_spec(g_mix.shape),
                _layer_spec((D_MODEL, N_BRANCH * D_MODEL), layer),
                _const_spec(wl["b_bg"].shape),
                pl.BlockSpec((tm, CONV_CH), lambda i: (i, 0)),
                pl.BlockSpec((HALO, CONV_CH), lambda i: (jnp.maximum(i * hpt - 1, 0), 0)),
                pl.BlockSpec((HALO, CONV_CH), lambda i: (jnp.minimum((i + 1) * hpt, n_halo - 1), 0)),
                _layer_spec((CONV_K + 1, CONV_CH), layer),
                _const_spec(wl["b_dw"].shape), _const_spec(wl["ln_g"].shape),
                _const_spec(wl["ln_b"].shape),
                _layer_spec((CONV_CH, D_MODEL), layer), _const_spec(wl["b_pw"].shape),
                pl.BlockSpec((N_HEADS * V_DIM, tm), lambda i: (0, jnp.minimum(i, n_lat - 1)))]
    args = [hs, modr, modr, modr, g_mix, wl["w_bg"], wl["b_bg"],
            vconv, vconv, vconv, wl["w_dw"], wl["b_dw"], wl["ln_g"], wl["ln_b"], wl["w_pw"],
            wl["b_pw"], ot_lat]
    if with_ctx:
        in_specs.append(pl.BlockSpec((N_HEADS * V_DIM, tm), lambda i: (0, jnp.maximum(i - n_lat, 0))))
        args.append(ot_ctx)
    in_specs.append(_layer_spec((N_HEADS * V_DIM, D_MODEL), layer))
    args.append(wl["w_o"])
    in_specs.append(pl.BlockSpec(
        (tm, FOURIER_WIDTH),
        lambda i: (jnp.where(i < n_lat, i % tpb, 0), jnp.where(i < n_lat, i // tpb, 0))))
    args.append(f_lat)
    if with_ctx:
        in_specs.append(pl.BlockSpec((tm, FOURIER_WIDTH), lambda i: (0, jnp.maximum(i - n_lat, 0))))
        args.append(f_ctx)
    in_specs += [_layer_spec((FOURIER_WIDTH, D_MODEL), layer), _const_spec(wl["b_f"].shape),
                 _layer_spec((D_MODEL, D_MODEL), layer)]
    args += [wl["w_f"], wl["b_f"], wl["w_out"]]
    return pl.pallas_call(
        functools.partial(_merge_kernel, with_ctx=with_ctx, n_lat=n_lat, tpb=tpb, tm=tm, layer=layer),
        grid=(n_tiles,),
        in_specs=in_specs,
        out_specs=pl.BlockSpec((tm, D_MODEL), lambda i: (i, 0)),
        out_shape=jax.ShapeDtypeStruct((n_tiles * tm, D_MODEL), F32),
        scratch_shapes=[pltpu.VMEM((tm + 2 * HALO, CONV_CH), F32),
                        pltpu.VMEM((SUBLANES, tm + 2 * HALO - SUBLANES, CONV_CH), F32)],
        compiler_params=_params(),
        name="merge",
    )(*args)


def _stack_weights(p):
    w_in = p["w_in"]
    depth = w_in.shape[0]
    zeros = lambda *shape: jnp.zeros((depth,) + shape, F32)
    tail = HEAD_PAD - QK_NOPE - QK_ROPE
    wq = p["w_uq"].reshape(depth, Q_LORA, N_HEADS, QK_NOPE + QK_ROPE)
    w_uq_a = jnp.concatenate([wq, zeros(Q_LORA, N_HEADS, tail)], axis=-1)
    wkv = p["w_ukv"].reshape(depth, KV_LORA, N_HEADS, QK_NOPE + V_DIM)
    w_uk = jnp.concatenate([wkv[..., :QK_NOPE], zeros(KV_LORA, N_HEADS, HEAD_PAD - QK_NOPE)], axis=-1)
    w_uvt = wkv[..., QK_NOPE:].reshape(depth, KV_LORA, N_HEADS * V_DIM).transpose(0, 2, 1)
    row = lambda v: v
    return {
        "w_in": jnp.swapaxes(w_in, 1, 2),
        "g_q": row(p["g_qnorm"]), "g_kv": row(p["g_kvnorm"]),
        "w_uq_a": w_uq_a.reshape(depth, Q_LORA, N_HEADS * HEAD_PAD).astype(BF16),
        "w_uk": w_uk.reshape(depth, KV_LORA, N_HEADS * HEAD_PAD).astype(BF16),
        "w_uvt": w_uvt.astype(BF16),
        "w_bg": p["w_bgate"].astype(BF16), "b_bg": row(p["b_bgate"]),
        "w_dw": jnp.concatenate([p["w_dw"], zeros(1, CONV_CH)], axis=1),
        "b_dw": row(p["b_dw"]), "ln_g": row(p["ln_g_conv"]), "ln_b": row(p["ln_b_conv"]),
        "w_pw": p["w_pw_conv"].astype(BF16), "b_pw": row(p["b_pw_conv"]),
        "w_o": p["w_o_mla"].astype(BF16),
        "w_f": p["w_fourier"].astype(BF16), "b_f": row(p["b_fourier"]),
        "w_out": p["w_out"].astype(BF16),
    }


def _tables(seq, lc):
    rows = seq // GRID_W
    row = jnp.broadcast_to(jnp.arange(rows, dtype=F32)[:, None], (rows, GRID_W)).reshape(-1)
    col = jnp.broadcast_to(jnp.arange(GRID_W, dtype=F32)[None, :], (rows, GRID_W)).reshape(-1)
    inv = 1.0 / (ROPE_BASE ** (jnp.arange(ROPE_AXIS // 2, dtype=F32) * 2.0 / ROPE_AXIS))
    ar = row[:, None] * inv
    ac = col[:, None] * inv
    ang = jnp.concatenate([ar, ar, ac, ac], axis=-1)
    tail = HEAD_PAD - QK_NOPE - QK_ROPE
    cos = jnp.concatenate([jnp.ones((seq, QK_NOPE), F32), jnp.cos(ang), jnp.ones((seq, tail), F32)], axis=1)
    sin = jnp.concatenate([jnp.zeros((seq, QK_NOPE), F32), jnp.sin(ang), jnp.zeros((seq, tail), F32)], axis=1)
    cos = jnp.concatenate([cos, jnp.ones((lc, HEAD_PAD), F32)], axis=0)
    sin = jnp.concatenate([sin, jnp.zeros((lc, HEAD_PAD), F32)], axis=0)
    c = np.arange(FOURIER_GROUP_CH)
    a = 2.0 * np.pi * ((c[:, None] * c[None, :]) % FOURIER_GROUP_CH) / FOURIER_GROUP_CH
    dftc = np.concatenate([np.cos(a), np.sin(a)], axis=1) / np.sqrt(FOURIER_GROUP_CH)
    return {"cos": cos, "sin": sin, "dftc": jnp.asarray(dftc, F32).astype(BF16)}


def kernel(x, c, ctx, c_ctx, w_ada, b_ada, g_ffn1, w1_ffn1, w3_ffn1, w2_ffn1, g_mix, w_in, w_dw, b_dw, ln_g_conv, ln_b_conv, w_pw_conv, b_pw_conv, g_qnorm, w_uq, g_kvnorm, w_ukv, w_o_mla, w_fourier, b_fourier, w_bgate, b_bgate, w_out, g_ffn2, w1_ffn2, w3_ffn2, w2_ffn2, g_final):
    bsz, seq, _ = x.shape
    lc = ctx.shape[1]
    depth = w_ada.shape[0]
    tm = lc
    tf = 2 * tm
    assert bsz + 1 <= MOD_ROWS and seq % tf == 0 and (bsz * lc) % tf == 0 and tm % HALO == 0
    n_lat_rows = bsz * seq
    p = dict(w_in=w_in, w_dw=w_dw, b_dw=b_dw, ln_g_conv=ln_g_conv, ln_b_conv=ln_b_conv,
             w_pw_conv=w_pw_conv, b_pw_conv=b_pw_conv, g_qnorm=g_qnorm, w_uq=w_uq,
             g_kvnorm=g_kvnorm, w_ukv=w_ukv, w_o_mla=w_o_mla, w_fourier=w_fourier,
             b_fourier=b_fourier, w_bgate=w_bgate, b_bgate=b_bgate, w_out=w_out)

    def make_row_fn(tile):
        n_lat_tiles, tiles_per_seq = n_lat_rows // tile, seq // tile
        return lambda i: jnp.where(i < n_lat_tiles, i // tiles_per_seq, bsz)

    dims = {"B": bsz, "S": seq, "Lc": lc, "tm": tm, "row_fn": make_row_fn(tm),
            "tq": min(4096, seq), "tk": min(256, seq)}
    ffn_row_fn = make_row_fn(tf)
    n_ffn_all = (n_lat_rows + bsz * lc) // tf
    n_ffn_lat = n_lat_rows // tf

    cs = jnp.concatenate([c, c_ctx[None, :], jnp.zeros((MOD_ROWS - bsz - 1, D_MODEL), F32)], axis=0)
    mod = _mod_call(cs, w_ada, b_ada)
    modr = mod.reshape(depth, MOD_ROWS, N_ADA, D_MODEL).transpose(0, 2, 1, 3)
    modr = modr.reshape(depth * N_ADA * MOD_ROWS, 1, D_MODEL)
    tabs = _tables(seq, lc)

    wl = _stack_weights(p)
    g_mix_r = g_mix
    ffn1 = (g_ffn1, w1_ffn1, w3_ffn1, w2_ffn1)
    ffn2 = (g_ffn2, w1_ffn2, w3_ffn2, w2_ffn2)

    hs = None
    for l in range(depth):
        last = l == depth - 1
        if l == 0:
            hs = _ffn_call(x.reshape(n_lat_rows, D_MODEL), modr, l, 0, *ffn1, n_tiles=n_ffn_all, tm=tf,
                           row_fn=ffn_row_fn, hs2=ctx.reshape(bsz * lc, D_MODEL))
        else:
            hs = _ffn_call(hs, modr, l, 0, *ffn1, n_tiles=n_ffn_all, tm=tf, row_fn=ffn_row_fn)
        q, k, vt, vconv, gc, gs = _mix_in_call(hs, modr, l, g_mix_r, wl, tabs, dims)
        ot_lat = _attn_call(q, k, vt, dims)
        f_lat = _fourier_lat_call(gc, gs, dims)
        if last:
            ot_ctx = f_ctx = None
        else:
            ot_ctx = _attn_ctx_call(q, k, vt, dims)
            f_ctx = _fourier_ctx_call(gc, gs, dims)
        hs = _merge_call(hs, modr, l, g_mix_r, wl, vconv, ot_lat, ot_ctx, f_lat, f_ctx, dims,
                         with_ctx=not last)
        hs = _ffn_call(hs, modr, l, 6, *ffn2, n_tiles=n_ffn_lat if last else n_ffn_all, tm=tf,
                       row_fn=ffn_row_fn, final_g=g_final if last else None)
    return hs.reshape(bsz, seq, D_MODEL)
```

```python
import functools

import numpy as np
import jax
import jax.numpy as jnp
from jax import lax
from jax.experimental import pallas as pl
from jax.experimental.pallas import tpu as pltpu

D_MODEL = 1024
D_FF = 2816
N_ADA = 9
CONV_CH = 384
CONV_K = 31
N_HEADS = 8
Q_LORA = 384
KV_LORA = 256
QK_NOPE = 64
QK_ROPE = 32
V_DIM = 64
ROPE_AXIS = QK_ROPE // 2
ROPE_BASE = 10000.0
GRID_W = 64
EPS = 1e-6
HALF = 0.5
ATTN_SCALE = (QK_NOPE + QK_ROPE) ** -0.5
FOURIER_GROUPS = 4
FOURIER_GROUP_CH = 128
FOURIER_WIDTH = FOURIER_GROUPS * FOURIER_GROUP_CH
N_BRANCH = 3

LANES = 128
SUBLANES = 8
HEAD_PAD = 128
MOD_ROWS = 8
HALO = 16
FFT_GROUP = 8
DENOM_ROWS = 16
QCOL = 256
LOG2E = float(np.log2(np.e))
Q_SCALE = ATTN_SCALE * LOG2E
VMEM_LIMIT = 56 * 1024 * 1024
MIX_WIDTH = 2 * CONV_CH + Q_LORA + KV_LORA + QK_ROPE + FOURIER_WIDTH

BF16 = jnp.bfloat16
F32 = jnp.float32
NT_DIMS = (((1,), (1,)), ((), ()))
TN_DIMS = (((0,), (0,)), ((), ()))


def _sigmoid(x):
    return 1.0 / (1.0 + jnp.exp2(x * (-LOG2E)))


def _rms(x, g):
    return x * lax.rsqrt(jnp.mean(x * x, axis=-1, keepdims=True) + EPS) * g


def _ada_norm(h, g_ref, sc_ref, sh_ref):
    return (_rms(h, g_ref[...] * (1.0 + sc_ref[...])) + sh_ref[...]).astype(BF16)


def _dot(a, b):
    return jnp.dot(a, b, preferred_element_type=F32)


def _params(n_axes=1):
    return pltpu.CompilerParams(dimension_semantics=("arbitrary",) * n_axes,
                                vmem_limit_bytes=VMEM_LIMIT)


def _const_spec(shape):
    zeros = (0,) * len(shape)
    return pl.BlockSpec(shape, lambda *_: zeros, pipeline_mode=pl.Buffered(1))


def _layer_spec(shape, layer):
    idx = (layer,) + (0,) * len(shape)
    return pl.BlockSpec((None,) + tuple(shape), lambda *_: idx, pipeline_mode=pl.Buffered(1))


def _mod_kernel(cs_ref, w_ref, b_ref, o_ref):
    cs = cs_ref[...]
    a = (cs * _sigmoid(cs)).astype(BF16)
    o_ref[...] = _dot(a, w_ref[...].astype(BF16)) + b_ref[...]


def _mod_call(cs, w_ada, b_ada):
    n_layers = w_ada.shape[0]
    return pl.pallas_call(
        _mod_kernel,
        grid=(n_layers, N_ADA),
        in_specs=[pl.BlockSpec((MOD_ROWS, D_MODEL), lambda l, k: (0, 0)),
                  pl.BlockSpec((None, D_MODEL, D_MODEL), lambda l, k: (l, 0, k)),
                  pl.BlockSpec((None, 1, D_MODEL), lambda l, k: (l, 0, k))],
        out_specs=pl.BlockSpec((None, MOD_ROWS, D_MODEL), lambda l, k: (l, 0, k)),
        out_shape=jax.ShapeDtypeStruct((n_layers, MOD_ROWS, N_ADA * D_MODEL), F32),
        compiler_params=_params(2),
        name="ada_map",
    )(cs, w_ada, b_ada.reshape(n_layers, 1, N_ADA * D_MODEL))


def _mod_spec(layer, k, row_fn):
    base = (layer * N_ADA + k) * MOD_ROWS
    return pl.BlockSpec((None, 1, D_MODEL), lambda i: (base + row_fn(i), 0, 0))


def _ffn_kernel(*refs, chunks, final, n_first, layer):
    refs = list(refs)
    o_ref = refs.pop()
    gf_ref = refs.pop() if final else None
    h2_ref = refs.pop(1) if n_first is not None else None
    h_ref, sh_ref, sc_ref, gt_ref, g_ref, w1_ref, w3_ref, w2_ref = refs
    g_ref = g_ref.at[pl.ds(layer, 1)]
    h = h_ref[...]
    if n_first is not None:
        h = jnp.where(pl.program_id(0) < n_first, h, h2_ref[...])
    xb = _ada_norm(h, g_ref, sc_ref, sh_ref)
    acc = None
    off = 0
    for c in chunks:
        a = _dot(xb, w1_ref[:, off:off + c].astype(BF16))
        b = _dot(xb, w3_ref[:, off:off + c].astype(BF16))
        act = (a * _sigmoid(a) * b).astype(BF16)
        y = _dot(act, w2_ref[off:off + c, :].astype(BF16))
        acc = y if acc is None else acc + y
        off += c
    out = h + (HALF * gt_ref[...]) * acc
    if final:
        out = _rms(out, gf_ref[...])
    o_ref[...] = out


def _ffn_call(hs, modr, layer, kbase, g, w1, w3, w2, *, n_tiles, tm, row_fn, final_g=None, hs2=None):
    assert D_FF % QCOL == 0
    chunks = (QCOL,) * (D_FF // QCOL)
    final = final_g is not None
    n_first = None if hs2 is None else hs.shape[0] // tm
    if hs2 is None:
        in_specs = [pl.BlockSpec((tm, D_MODEL), lambda i: (i, 0))]
        args = [hs]
    else:
        in_specs = [pl.BlockSpec((tm, D_MODEL), lambda i: (jnp.minimum(i, n_first - 1), 0)),
                    pl.BlockSpec((tm, D_MODEL), lambda i: (jnp.maximum(i - n_first, 0), 0))]
        args = [hs, hs2]
    in_specs += [_mod_spec(layer, kbase, row_fn), _mod_spec(layer, kbase + 1, row_fn),
                 _mod_spec(layer, kbase + 2, row_fn),
                 _const_spec(g.shape),
                 _layer_spec((D_MODEL, D_FF), layer), _layer_spec((D_MODEL, D_FF), layer),
                 _layer_spec((D_FF, D_MODEL), layer)]
    args += [modr, modr, modr, g, w1, w3, w2]
    if final:
        in_specs.append(_const_spec((1, D_MODEL)))
        args.append(final_g.reshape(1, D_MODEL))
    return pl.pallas_call(
        functools.partial(_ffn_kernel, chunks=chunks, final=final, n_first=n_first, layer=layer),
        grid=(n_tiles,),
        in_specs=in_specs,
        out_specs=pl.BlockSpec((tm, D_MODEL), lambda i: (i, 0)),
        out_shape=jax.ShapeDtypeStruct((n_tiles * tm, D_MODEL), F32),
        compiler_params=_params(),
        name="ffn",
    )(*args)


def _mix_in_kernel(h_ref, sh_ref, sc_ref, g_ref, win_ref, gq_ref, wqa_ref, gkv_ref,
                   wuk_ref, wuvt_ref, cos_ref, sin_ref, dftc_ref,
                   q_ref, k_ref, vt_ref, vc_ref, gc_ref, gs_ref, *, layer):
    g_ref, gq_ref, gkv_ref = (r.at[pl.ds(layer, 1)] for r in (g_ref, gq_ref, gkv_ref))
    h = h_ref[...]
    u = _ada_norm(h, g_ref, sc_ref, sh_ref)
    o_cq = 2 * CONV_CH
    o_ckv = o_cq + Q_LORA
    o_kr = o_ckv + KV_LORA
    z = lax.dot_general(u, win_ref[:o_kr + HEAD_PAD, :].astype(BF16), NT_DIMS,
                        preferred_element_type=F32)
    z_four = lax.dot_general(u, win_ref[o_kr + QK_ROPE:, :].astype(BF16), NT_DIMS,
                             preferred_element_type=F32)
    vc_ref[...] = z[:, :CONV_CH] * _sigmoid(z[:, CONV_CH:o_cq])
    cos = cos_ref[...]
    sin = sin_ref[...]
    lane = lax.broadcasted_iota(jnp.int32, (1, HEAD_PAD), 1)
    first_half = ((lane - QK_NOPE) % ROPE_AXIS) < (ROPE_AXIS // 2)

    def rot_half(x):
        fwd = pltpu.roll(x, ROPE_AXIS // 2, axis=1)
        bwd = pltpu.roll(x, HEAD_PAD - ROPE_AXIS // 2, axis=1)
        return jnp.where(first_half, -bwd, fwd)

    cqn = _rms(z[:, o_cq:o_ckv], gq_ref[...]).astype(BF16)
    qa = _dot(cqn, wqa_ref[...])
    for hh in range(N_HEADS):
        s = slice(hh * HEAD_PAD, (hh + 1) * HEAD_PAD)
        q_ref[hh] = ((qa[:, s] * cos + rot_half(qa[:, s]) * sin) * Q_SCALE).astype(BF16)
    ckvn = _rms(z[:, o_ckv:o_kr], gkv_ref[...]).astype(BF16)
    kn = _dot(ckvn, wuk_ref[...])
    rope_lanes = jnp.logical_and(lane >= QK_NOPE, lane < QK_NOPE + QK_ROPE)
    zkr = jnp.where(rope_lanes, pltpu.roll(z[:, o_kr:], QK_NOPE, axis=1), 0.0)
    kr = zkr * cos + rot_half(zkr) * sin
    for hh in range(N_HEADS):
        s = slice(hh * HEAD_PAD, (hh + 1) * HEAD_PAD)
        k_ref[hh] = (kn[:, s] + kr).astype(BF16)
    vt_ref[...] = lax.dot_general(wuvt_ref[...], ckvn, NT_DIMS,
                                  preferred_element_type=F32).astype(BF16)
    zf = z_four.astype(BF16)
    for gi in range(FOURIER_GROUPS):
        s = slice(gi * FOURIER_GROUP_CH, (gi + 1) * FOURIER_GROUP_CH)
        r = _dot(zf[:, s], dftc_ref[...])
        gc_ref[:, s] = r[:, :FOURIER_GROUP_CH]
        gs_ref[:, s] = r[:, FOURIER_GROUP_CH:]


def _mix_in_call(hs, modr, layer, g_mix, wl, tabs, dims):
    bsz, seq, lc, tm = dims["B"], dims["S"], dims["Lc"], dims["tm"]
    n_rows = bsz * (seq + lc)
    tpb = seq // tm
    n_lat = bsz * tpb
    n_tiles = n_rows // tm
    row_fn = dims["row_fn"]

    def tab_idx(i):
        return (jnp.where(i < n_lat, i % tpb, tpb), 0)

    def g_idx(i):
        return (jnp.where(i < n_lat, i % tpb, tpb), jnp.where(i < n_lat, i // tpb, i - n_lat))

    in_specs = [pl.BlockSpec((tm, D_MODEL), lambda i: (i, 0)),
                _mod_spec(layer, 3, row_fn), _mod_spec(layer, 4, row_fn),
                _const_spec(g_mix.shape),
                _layer_spec((MIX_WIDTH, D_MODEL), layer),
                _const_spec(wl["g_q"].shape),
                _layer_spec((Q_LORA, N_HEADS * HEAD_PAD), layer),
                _const_spec(wl["g_kv"].shape),
                _layer_spec((KV_LORA, N_HEADS * HEAD_PAD), layer),
                _layer_spec((N_HEADS * V_DIM, KV_LORA), layer),
                pl.BlockSpec((tm, HEAD_PAD), tab_idx), pl.BlockSpec((tm, HEAD_PAD), tab_idx),
                _const_spec((FOURIER_GROUP_CH, 2 * FOURIER_GROUP_CH))]
    out_specs = [pl.BlockSpec((N_HEADS, tm, HEAD_PAD), lambda i: (0, i, 0)),
                 pl.BlockSpec((N_HEADS, tm, HEAD_PAD), lambda i: (0, i, 0)),
                 pl.BlockSpec((N_HEADS * V_DIM, tm), lambda i: (0, i)),
                 pl.BlockSpec((tm, CONV_CH), lambda i: (i, 0)),
                 pl.BlockSpec((tm, FOURIER_WIDTH), g_idx),
                 pl.BlockSpec((tm, FOURIER_WIDTH), g_idx)]
    out_shape = [jax.ShapeDtypeStruct((N_HEADS, n_rows, HEAD_PAD), BF16),
                 jax.ShapeDtypeStruct((N_HEADS, n_rows, HEAD_PAD), BF16),
                 jax.ShapeDtypeStruct((N_HEADS * V_DIM, n_rows), BF16),
                 jax.ShapeDtypeStruct((n_rows, CONV_CH), F32),
                 jax.ShapeDtypeStruct((seq + lc, bsz * FOURIER_WIDTH), F32),
                 jax.ShapeDtypeStruct((seq + lc, bsz * FOURIER_WIDTH), F32)]
    return pl.pallas_call(
        functools.partial(_mix_in_kernel, layer=layer),
        grid=(n_tiles,),
        in_specs=in_specs,
        out_specs=out_specs,
        out_shape=out_shape,
        compiler_params=_params(),
        name="mix_in",
    )(hs, modr, modr, g_mix, wl["w_in"], wl["g_q"], wl["w_uq_a"],
      wl["g_kv"], wl["w_uk"], wl["w_uvt"], tabs["cos"], tabs["sin"], tabs["dftc"])


def _attn_kernel(q_ref, kc_ref, vc_ref, kl_ref, vl_ref, o_ref, *, tk):
    segs = [(kc_ref, vc_ref, 0, kc_ref.shape[0])]
    segs += [(kl_ref, vl_ref, s0, tk) for s0 in range(0, kl_ref.shape[0], tk)]
    n_blk = q_ref.shape[0] // QCOL
    cols = [slice(i * QCOL, (i + 1) * QCOL) for i in range(n_blk)]
    qs = [q_ref[c, :] for c in cols]

    def scores(j, i):
        kref, _, s0, size = segs[j]
        return lax.dot_general(kref[s0:s0 + size, :], qs[i], NT_DIMS,
                               preferred_element_type=F32)

    m = [None] * n_blk
    acc = [None] * n_blk
    s_cur = [scores(0, i) for i in range(n_blk)]
    for j, (_, vref, s0, size) in enumerate(segs):
        v_aug = jnp.concatenate([vref[:, s0:s0 + size], jnp.ones((DENOM_ROWS, size), BF16)], axis=0)
        s_next = [None] * n_blk
        for i in range(n_blk):
            if j + 1 < len(segs):
                s_next[i] = scores(j + 1, i)
            smax = jnp.max(s_cur[i], axis=0, keepdims=True)
            m_new = smax if m[i] is None else jnp.maximum(m[i], smax)
            p = jnp.exp2(s_cur[i] - m_new).astype(BF16)
            pv = _dot(v_aug, p)
            acc[i] = pv if m[i] is None else jnp.exp2(m[i] - m_new) * acc[i] + pv
            m[i] = m_new
        s_cur = s_next
    for i in range(n_blk):
        o_ref[:, cols[i]] = (acc[i][:V_DIM] / acc[i][V_DIM:V_DIM + 1]).astype(BF16)


def _attn_ctx_kernel(q_ref, k_ref, vt_ref, o_ref):
    ones = jnp.ones((DENOM_ROWS, k_ref.shape[1]), BF16)
    for hh in range(N_HEADS):
        rows = slice(hh * V_DIM, (hh + 1) * V_DIM)
        s = lax.dot_general(k_ref[hh], q_ref[hh], NT_DIMS, preferred_element_type=F32)
        p = jnp.exp2(s - jnp.max(s, axis=0, keepdims=True)).astype(BF16)
        pv = _dot(jnp.concatenate([vt_ref[rows, :], ones], axis=0), p)
        o_ref[rows, :] = (pv[:V_DIM] / pv[V_DIM:V_DIM + 1]).astype(BF16)


def _attn_call(q, k, vt, dims):
    bsz, seq, lc, tq = dims["B"], dims["S"], dims["Lc"], dims["tq"]
    ctx_blk0 = (bsz * seq) // lc
    nq = seq // tq
    return pl.pallas_call(
        functools.partial(_attn_kernel, tk=dims["tk"]),
        grid=(bsz, N_HEADS, nq),
        in_specs=[pl.BlockSpec((None, tq, HEAD_PAD), lambda b, h, i: (h, b * nq + i, 0)),
                  pl.BlockSpec((None, lc, HEAD_PAD), lambda b, h, i: (h, ctx_blk0 + b, 0)),
                  pl.BlockSpec((V_DIM, lc), lambda b, h, i: (h, ctx_blk0 + b)),
                  pl.BlockSpec((None, seq, HEAD_PAD), lambda b, h, i: (h, b, 0)),
                  pl.BlockSpec((V_DIM, seq), lambda b, h, i: (h, b))],
        out_specs=pl.BlockSpec((V_DIM, tq), lambda b, h, i: (h, b * nq + i)),
        out_shape=jax.ShapeDtypeStruct((N_HEADS * V_DIM, bsz * seq), BF16),
        compiler_params=_params(3),
        name="attn_lat",
    )(q, k, vt, k, vt)


def _attn_ctx_call(q, k, vt, dims):
    bsz, seq, lc = dims["B"], dims["S"], dims["Lc"]
    ctx_blk0 = (bsz * seq) // lc
    heads = pl.BlockSpec((N_HEADS, lc, HEAD_PAD), lambda b: (0, ctx_blk0 + b, 0))
    return pl.pallas_call(
        _attn_ctx_kernel,
        grid=(bsz,),
        in_specs=[heads, heads, pl.BlockSpec((N_HEADS * V_DIM, lc), lambda b: (0, ctx_blk0 + b))],
        out_specs=pl.BlockSpec((N_HEADS * V_DIM, lc), lambda b: (0, b)),
        out_shape=jax.ShapeDtypeStruct((N_HEADS * V_DIM, bsz * lc), BF16),
        compiler_params=_params(),
        name="attn_ctx",
    )(q, k, vt)


def _fft1_kernel(gc_ref, gs_ref, m_ref, zr_ref, zi_ref):
    n1 = gc_ref.shape[0]
    for j in range(FFT_GROUP):
        x = jnp.concatenate([gc_ref[:, j, :], gs_ref[:, j, :]], axis=0).astype(BF16)
        z = _dot(m_ref[j], x)
        zr_ref[:, j, :] = z[:n1]
        zi_ref[:, j, :] = z[n1:]


def _fft2_kernel(zr_ref, zi_ref, w_ref, f_ref):
    for j in range(FFT_GROUP):
        z = jnp.concatenate([zr_ref[j], zi_ref[j]], axis=0).astype(BF16)
        f_ref[:, j, :] = _dot(w_ref[...], z)


def _fft_tables(seq):
    n1 = int(round(seq ** 0.5))
    n2 = seq // n1
    assert n1 * n2 == seq and n1 % FFT_GROUP == 0 and n2 % FFT_GROUP == 0
    k1 = np.arange(n1)[None, :, None]
    t1 = np.arange(n1)[None, None, :]
    t2 = np.arange(n2)[:, None, None]
    ang = -2.0 * np.pi * (((t1 * k1) % n1) / n1 + ((t2 * k1) % seq) / seq)
    ar = np.cos(ang) / np.sqrt(n1)
    ai = np.sin(ang) / np.sqrt(n1)
    m1 = np.concatenate([np.concatenate([ar, ai], axis=2),
                         np.concatenate([ai, -ar], axis=2)], axis=1)
    k2 = np.arange(n2)[:, None]
    tt = np.arange(n2)[None, :]
    a2 = 2.0 * np.pi * ((k2 * tt) % n2) / n2
    w2 = np.concatenate([np.cos(a2), np.sin(a2)], axis=1) / np.sqrt(n2)
    return n1, n2, jnp.asarray(m1, F32).astype(BF16), jnp.asarray(w2, F32).astype(BF16)


def _fourier_lat_call(gc, gs, dims):
    bsz, seq = dims["B"], dims["S"]
    n1, n2, m1, w2 = _fft_tables(seq)
    ncol = bsz * FOURIER_WIDTH
    assert gc.shape[0] % n2 == 0
    gc3 = gc.reshape(gc.shape[0] // n2, n2, ncol)
    gs3 = gs.reshape(gs.shape[0] // n2, n2, ncol)
    zshape = jax.ShapeDtypeStruct((n1, n2, ncol), F32)
    strided = pl.BlockSpec((n1, FFT_GROUP, ncol), lambda j: (0, j, 0))
    zr, zi = pl.pallas_call(
        _fft1_kernel,
        grid=(n2 // FFT_GROUP,),
        in_specs=[strided, strided,
                  pl.BlockSpec((FFT_GROUP, 2 * n1, 2 * n1), lambda j: (j, 0, 0))],
        out_specs=[strided, strided],
        out_shape=[zshape, zshape],
        compiler_params=_params(),
        name="fft_stage1",
    )(gc3, gs3, m1)
    slab = pl.BlockSpec((FFT_GROUP, n2, ncol), lambda j: (j, 0, 0))
    f3 = pl.pallas_call(
        _fft2_kernel,
        grid=(n1 // FFT_GROUP,),
        in_specs=[slab, slab, _const_spec((n2, 2 * n2))],
        out_specs=pl.BlockSpec((n2, FFT_GROUP, ncol), lambda j: (0, j, 0)),
        out_shape=jax.ShapeDtypeStruct((n2, n1, ncol), F32),
        compiler_params=_params(),
        name="fft_stage2",
    )(zr, zi, w2)
    return f3.reshape(seq, ncol)


def _dft_ctx_kernel(gc_ref, gs_ref, c_ref, s_ref, f_ref):
    f_ref[...] = (_dot(c_ref[...], gc_ref[...].astype(BF16))
                  - _dot(s_ref[...], gs_ref[...].astype(BF16)))


def _fourier_ctx_call(gc, gs, dims):
    bsz, lc = dims["B"], dims["Lc"]
    ncol = bsz * FOURIER_WIDTH
    kt = (np.arange(lc)[:, None] * np.arange(lc)[None, :]) % lc
    ang = 2.0 * np.pi * kt / lc
    cm = jnp.asarray(np.cos(ang) / np.sqrt(lc), F32).astype(BF16)
    sm = jnp.asarray(np.sin(ang) / np.sqrt(lc), F32).astype(BF16)
    ctx_cols = pl.BlockSpec((lc, ncol), lambda j: (dims["S"] // lc, 0))
    return pl.pallas_call(
        _dft_ctx_kernel,
        grid=(1,),
        in_specs=[ctx_cols, ctx_cols, _const_spec((lc, lc)), _const_spec((lc, lc))],
        out_specs=pl.BlockSpec((lc, ncol), lambda j: (0, 0)),
        out_shape=jax.ShapeDtypeStruct((lc, ncol), F32),
        compiler_params=_params(),
        name="dft_ctx",
    )(gc, gs, cm, sm)


def _merge_kernel(*refs, with_ctx, n_lat, tpb, tm, layer):
    refs = list(refs)
    xs_ref = refs.pop()
    xpad_ref = refs.pop()
    o_ref = refs.pop()
    it = iter(refs)
    h_ref, sh_ref, sc_ref, gt_ref, g_ref, wbg_ref, bbg_ref = [next(it) for _ in range(7)]
    vc_ref, vp_ref, vn_ref, wdw_ref, bdw_ref, lng_ref, lnb_ref, wpw_ref, bpw_ref = [
        next(it) for _ in range(9)]
    otl_ref = next(it)
    otc_ref = next(it) if with_ctx else None
    wo_ref = next(it)
    fl_ref = next(it)
    fc_ref = next(it) if with_ctx else None
    wf_ref, bf_ref, wout_ref = [next(it) for _ in range(3)]
    g_ref, bbg_ref, bdw_ref, lng_ref, lnb_ref, bpw_ref, bf_ref = (
        r.at[pl.ds(layer, 1)] for r in (g_ref, bbg_ref, bdw_ref, lng_ref, lnb_ref, bpw_ref, bf_ref))

    i = pl.program_id(0)
    pos = i % tpb
    first = pos == 0
    last = pos == tpb - 1
    if with_ctx:
        is_lat = i < n_lat
        first = jnp.logical_or(first, jnp.logical_not(is_lat))
        last = jnp.logical_or(last, jnp.logical_not(is_lat))

    h = h_ref[...]
    u = _ada_norm(h, g_ref, sc_ref, sh_ref)

    xpad_ref[0:HALO, :] = jnp.where(first, 0.0, vp_ref[...])
    xpad_ref[HALO:HALO + tm, :] = vc_ref[...]
    xpad_ref[HALO + tm:, :] = jnp.where(last, 0.0, vn_ref[...])
    base = HALO - CONV_K // 2
    n_shift_rows = xs_ref.shape[1]
    for r in range(SUBLANES):
        xs_ref[r] = xpad_ref[pl.ds(r, n_shift_rows), :]

    def conv_lanes(c0):
        acc = None
        for kk in range(CONV_K):
            off = base + kk
            a0 = (off // SUBLANES) * SUBLANES
            t = (xs_ref[off % SUBLANES, a0:a0 + tm, c0:c0 + LANES]
                 * wdw_ref[kk:kk + 1, c0:c0 + LANES])
            acc = t if acc is None else acc + t
        return acc + bdw_ref[:, c0:c0 + LANES]

    ot = otl_ref[...]
    fo = fl_ref[...]
    if with_ctx:
        ot = jnp.where(is_lat, ot, otc_ref[...])
        fo = jnp.where(is_lat, fo, fc_ref[...])
    assert CONV_CH == N_BRANCH * LANES
    gate_pre = []
    conv_parts = []
    for br in range(N_BRANCH):
        cols = slice(br * D_MODEL, (br + 1) * D_MODEL)
        gate_pre.append(_dot(u, wbg_ref[:, cols]) + bbg_ref[:, cols])
        conv_parts.append(conv_lanes(br * LANES))
    y_mla = lax.dot_general(ot, wo_ref[...], TN_DIMS, preferred_element_type=F32)
    y_four = _dot(fo.astype(BF16), wf_ref[...]) + bf_ref[...]
    conv = jnp.concatenate(conv_parts, axis=-1)
    mu = jnp.mean(conv, axis=-1, keepdims=True)
    cen = conv - mu
    var = jnp.mean(cen * cen, axis=-1, keepdims=True)
    ln = cen * lax.rsqrt(var + EPS) * lng_ref[...] + lnb_ref[...]
    y_conv = _dot((ln * _sigmoid(ln)).astype(BF16), wpw_ref[...]) + bpw_ref[...]

    mix = (_sigmoid(gate_pre[0]) * y_conv + _sigmoid(gate_pre[1]) * y_mla
           + _sigmoid(gate_pre[2]) * y_four)
    y = _dot(mix.astype(BF16), wout_ref[...])
    o_ref[...] = h + gt_ref[...] * y


def _merge_call(hs, modr, layer, g_mix, wl, vconv, ot_lat, ot_ctx, f_lat, f_ctx, dims, *, with_ctx):
    bsz, seq, lc, tm = dims["B"], dims["S"], dims["Lc"], dims["tm"]
    tpb = seq // tm
    n_lat = bsz * tpb
    n_tiles = n_lat + (bsz * lc // tm if with_ctx else 0)
    row_fn = dims["row_fn"]
    hpt = tm // HALO
    n_halo = vconv.shape[0] // HALO

    in_specs = [pl.BlockSpec((tm, D_MODEL), lambda i: (i, 0)),
                _mod_spec(layer, 3, row_fn), _mod_spec(layer, 4, row_fn),
                _mod_spec(layer, 5, row_fn),
                _const_spec(g_mix.shape),
                _layer_spec((D_MODEL, N_BRANCH * D_MODEL), layer),
                _const_spec(wl["b_bg"].shape),
                pl.BlockSpec((tm, CONV_CH), lambda i: (i, 0)),
                pl.BlockSpec((HALO, CONV_CH), lambda i: (jnp.maximum(i * hpt - 1, 0), 0)),
                pl.BlockSpec((HALO, CONV_CH), lambda i: (jnp.minimum((i + 1) * hpt, n_halo - 1), 0)),
                _layer_spec((CONV_K + 1, CONV_CH), layer),
                _const_spec(wl["b_dw"].shape), _const_spec(wl["ln_g"].shape),
                _const_spec(wl["ln_b"].shape),
                _layer_spec((CONV_CH, D_MODEL), layer), _const_spec(wl["b_pw"].shape),
                pl.BlockSpec((N_HEADS * V_DIM, tm), lambda i: (0, jnp.minimum(i, n_lat - 1)))]
    args = [hs, modr, modr, modr, g_mix, wl["w_bg"], wl["b_bg"],
            vconv, vconv, vconv, wl["w_dw"], wl["b_dw"], wl["ln_g"], wl["ln_b"], wl["w_pw"],
            wl["b_pw"], ot_lat]
    if with_ctx:
        in_specs.append(pl.BlockSpec((N_HEADS * V_DIM, tm), lambda i: (0, jnp.maximum(i - n_lat, 0))))
        args.append(ot_ctx)
    in_specs.append(_layer_spec((N_HEADS * V_DIM, D_MODEL), layer))
    args.append(wl["w_o"])
    in_specs.append(pl.BlockSpec(
        (tm, FOURIER_WIDTH),
        lambda i: (jnp.where(i < n_lat, i % tpb, 0), jnp.where(i < n_lat, i // tpb, 0))))
    args.append(f_lat)
    if with_ctx:
        in_specs.append(pl.BlockSpec((tm, FOURIER_WIDTH), lambda i: (0, jnp.maximum(i - n_lat, 0))))
        args.append(f_ctx)
    in_specs += [_layer_spec((FOURIER_WIDTH, D_MODEL), layer), _const_spec(wl["b_f"].shape),
                 _layer_spec((D_MODEL, D_MODEL), layer)]
    args += [wl["w_f"], wl["b_f"], wl["w_out"]]
    return pl.pallas_call(
        functools.partial(_merge_kernel, with_ctx=with_ctx, n_lat=n_lat, tpb=tpb, tm=tm, layer=layer),
        grid=(n_tiles,),
        in_specs=in_specs,
        out_specs=pl.BlockSpec((tm, D_MODEL), lambda i: (i, 0)),
        out_shape=jax.ShapeDtypeStruct((n_tiles * tm, D_MODEL), F32),
        scratch_shapes=[pltpu.VMEM((tm + 2 * HALO, CONV_CH), F32),
                        pltpu.VMEM((SUBLANES, tm + 2 * HALO - SUBLANES, CONV_CH), F32)],
        compiler_params=_params(),
        name="merge",
    )(*args)


def _stack_weights(p):
    w_in = p["w_in"]
    depth = w_in.shape[0]
    zeros = lambda *shape: jnp.zeros((depth,) + shape, F32)
    tail = HEAD_PAD - QK_NOPE - QK_ROPE
    wq = p["w_uq"].reshape(depth, Q_LORA, N_HEADS, QK_NOPE + QK_ROPE)
    w_uq_a = jnp.concatenate([wq, zeros(Q_LORA, N_HEADS, tail)], axis=-1)
    wkv = p["w_ukv"].reshape(depth, KV_LORA, N_HEADS, QK_NOPE + V_DIM)
    w_uk = jnp.concatenate([wkv[..., :QK_NOPE], zeros(KV_LORA, N_HEADS, HEAD_PAD - QK_NOPE)], axis=-1)
    w_uvt = wkv[..., QK_NOPE:].reshape(depth, KV_LORA, N_HEADS * V_DIM).transpose(0, 2, 1)
    row = lambda v: v
    return {
        "w_in": jnp.swapaxes(w_in, 1, 2),
        "g_q": row(p["g_qnorm"]), "g_kv": row(p["g_kvnorm"]),
        "w_uq_a": w_uq_a.reshape(depth, Q_LORA, N_HEADS * HEAD_PAD).astype(BF16),
        "w_uk": w_uk.reshape(depth, KV_LORA, N_HEADS * HEAD_PAD).astype(BF16),
        "w_uvt": w_uvt.astype(BF16),
        "w_bg": p["w_bgate"].astype(BF16), "b_bg": row(p["b_bgate"]),
        "w_dw": jnp.concatenate([p["w_dw"], zeros(1, CONV_CH)], axis=1),
        "b_dw": row(p["b_dw"]), "ln_g": row(p["ln_g_conv"]), "ln_b": row(p["ln_b_conv"]),
        "w_pw": p["w_pw_conv"].astype(BF16), "b_pw": row(p["b_pw_conv"]),
        "w_o": p["w_o_mla"].astype(BF16),
        "w_f": p["w_fourier"].astype(BF16), "b_f": row(p["b_fourier"]),
        "w_out": p["w_out"].astype(BF16),
    }


def _tables(seq, lc):
    rows = seq // GRID_W
    row = jnp.broadcast_to(jnp.arange(rows, dtype=F32)[:, None], (rows, GRID_W)).reshape(-1)
    col = jnp.broadcast_to(jnp.arange(GRID_W, dtype=F32)[None, :], (rows, GRID_W)).reshape(-1)
    inv = 1.0 / (ROPE_BASE ** (jnp.arange(ROPE_AXIS // 2, dtype=F32) * 2.0 / ROPE_AXIS))
    ar = row[:, None] * inv
    ac = col[:, None] * inv
    ang = jnp.concatenate([ar, ar, ac, ac], axis=-1)
    tail = HEAD_PAD - QK_NOPE - QK_ROPE
    cos = jnp.concatenate([jnp.ones((seq, QK_NOPE), F32), jnp.cos(ang), jnp.ones((seq, tail), F32)], axis=1)
    sin = jnp.concatenate([jnp.zeros((seq, QK_NOPE), F32), jnp.sin(ang), jnp.zeros((seq, tail), F32)], axis=1)
    cos = jnp.concatenate([cos, jnp.ones((lc, HEAD_PAD), F32)], axis=0)
    sin = jnp.concatenate([sin, jnp.zeros((lc, HEAD_PAD), F32)], axis=0)
    c = np.arange(FOURIER_GROUP_CH)
    a = 2.0 * np.pi * ((c[:, None] * c[None, :]) % FOURIER_GROUP_CH) / FOURIER_GROUP_CH
    dftc = np.concatenate([np.cos(a), np.sin(a)], axis=1) / np.sqrt(FOURIER_GROUP_CH)
    return {"cos": cos, "sin": sin, "dftc": jnp.asarray(dftc, F32).astype(BF16)}


def kernel(x, c, ctx, c_ctx, w_ada, b_ada, g_ffn1, w1_ffn1, w3_ffn1, w2_ffn1, g_mix, w_in, w_dw, b_dw, ln_g_conv, ln_b_conv, w_pw_conv, b_pw_conv, g_qnorm, w_uq, g_kvnorm, w_ukv, w_o_mla, w_fourier, b_fourier, w_bgate, b_bgate, w_out, g_ffn2, w1_ffn2, w3_ffn2, w2_ffn2, g_final):
    bsz, seq, _ = x.shape
    lc = ctx.shape[1]
    depth = w_ada.shape[0]
    tm = lc
    tf = 2 * tm
    assert bsz + 1 <= MOD_ROWS and seq % tf == 0 and (bsz * lc) % tf == 0 and tm % HALO == 0
    n_lat_rows = bsz * seq
    p = dict(w_in=w_in, w_dw=w_dw, b_dw=b_dw, ln_g_conv=ln_g_conv, ln_b_conv=ln_b_conv,
             w_pw_conv=w_pw_conv, b_pw_conv=b_pw_conv, g_qnorm=g_qnorm, w_uq=w_uq,
             g_kvnorm=g_kvnorm, w_ukv=w_ukv, w_o_mla=w_o_mla, w_fourier=w_fourier,
             b_fourier=b_fourier, w_bgate=w_bgate, b_bgate=b_bgate, w_out=w_out)

    def make_row_fn(tile):
        n_lat_tiles, tiles_per_seq = n_lat_rows // tile, seq // tile
        return lambda i: jnp.where(i < n_lat_tiles, i // tiles_per_seq, bsz)

    dims = {"B": bsz, "S": seq, "Lc": lc, "tm": tm, "row_fn": make_row_fn(tm),
            "tq": min(4096, seq), "tk": min(256, seq)}
    ffn_row_fn = make_row_fn(tf)
    n_ffn_all = (n_lat_rows + bsz * lc) // tf
    n_ffn_lat = n_lat_rows // tf

    cs = jnp.concatenate([c, c_ctx[None, :], jnp.zeros((MOD_ROWS - bsz - 1, D_MODEL), F32)], axis=0)
    mod = _mod_call(cs, w_ada, b_ada)
    modr = mod.reshape(depth, MOD_ROWS, N_ADA, D_MODEL).transpose(0, 2, 1, 3)
    modr = modr.reshape(depth * N_ADA * MOD_ROWS, 1, D_MODEL)
    tabs = _tables(seq, lc)

    wl = _stack_weights(p)
    g_mix_r = g_mix
    ffn1 = (g_ffn1, w1_ffn1, w3_ffn1, w2_ffn1)
    ffn2 = (g_ffn2, w1_ffn2, w3_ffn2, w2_ffn2)

    hs = None
    for l in range(depth):
        last = l == depth - 1
        if l == 0:
            hs = _ffn_call(x.reshape(n_lat_rows, D_MODEL), modr, l, 0, *ffn1, n_tiles=n_ffn_all, tm=tf,
                           row_fn=ffn_row_fn, hs2=ctx.reshape(bsz * lc, D_MODEL))
        else:
            hs = _ffn_call(hs, modr, l, 0, *ffn1, n_tiles=n_ffn_all, tm=tf, row_fn=ffn_row_fn)
        q, k, vt, vconv, gc, gs = _mix_in_call(hs, modr, l, g_mix_r, wl, tabs, dims)
        ot_lat = _attn_call(q, k, vt, dims)
        f_lat = _fourier_lat_call(gc, gs, dims)
        if last:
            ot_ctx = f_ctx = None
        else:
            ot_ctx = _attn_ctx_call(q, k, vt, dims)
            f_ctx = _fourier_ctx_call(gc, gs, dims)
        hs = _merge_call(hs, modr, l, g_mix_r, wl, vconv, ot_lat, ot_ctx, f_lat, f_ctx, dims,
                         with_ctx=not last)
        hs = _ffn_call(hs, modr, l, 6, *ffn2, n_tiles=n_ffn_lat if last else n_ffn_all, tm=tf,
                       row_fn=ffn_row_fn, final_g=g_final if last else None)
    return hs.reshape(bsz, seq, D_MODEL)
```

```python
import functools

import numpy as np
import jax
import jax.numpy as jnp
from jax import lax
from jax.experimental import pallas as pl
from jax.experimental.pallas import tpu as pltpu

D_MODEL = 1024
D_FF = 2816
N_ADA = 9
CONV_CH = 384
CONV_K = 31
N_HEADS = 8
Q_LORA = 384
KV_LORA = 256
QK_NOPE = 64
QK_ROPE = 32
V_DIM = 64
ROPE_AXIS = QK_ROPE // 2
ROPE_BASE = 10000.0
GRID_W = 64
EPS = 1e-6
HALF = 0.5
ATTN_SCALE = (QK_NOPE + QK_ROPE) ** -0.5
FOURIER_GROUPS = 4
FOURIER_GROUP_CH = 128
FOURIER_WIDTH = FOURIER_GROUPS * FOURIER_GROUP_CH
N_BRANCH = 3

LANES = 128
SUBLANES = 8
HEAD_PAD = 128
MOD_ROWS = 8
HALO = 16
FFT_GROUP = 8
DENOM_ROWS = 16
QCOL = 256
SCORE_LOOKAHEAD = 8
LOG2E = float(np.log2(np.e))
Q_SCALE = ATTN_SCALE * LOG2E
VMEM_LIMIT = 56 * 1024 * 1024
MIX_WIDTH = 2 * CONV_CH + Q_LORA + KV_LORA + QK_ROPE + FOURIER_WIDTH

BF16 = jnp.bfloat16
F32 = jnp.float32
NT_DIMS = (((1,), (1,)), ((), ()))
TN_DIMS = (((0,), (0,)), ((), ()))


def _sigmoid(x):
    return 1.0 / (1.0 + jnp.exp2(x * (-LOG2E)))


def _rms(x, g):
    return x * lax.rsqrt(jnp.mean(x * x, axis=-1, keepdims=True) + EPS) * g


def _ada_norm(h, g_ref, sc_ref, sh_ref):
    return (_rms(h, g_ref[...] * (1.0 + sc_ref[...])) + sh_ref[...]).astype(BF16)


def _dot(a, b):
    return jnp.dot(a, b, preferred_element_type=F32)


def _params(n_axes=1):
    return pltpu.CompilerParams(dimension_semantics=("arbitrary",) * n_axes,
                                vmem_limit_bytes=VMEM_LIMIT)


def _const_spec(shape):
    zeros = (0,) * len(shape)
    return pl.BlockSpec(shape, lambda *_: zeros, pipeline_mode=pl.Buffered(1))


def _layer_spec(shape, layer):
    idx = (layer,) + (0,) * len(shape)
    return pl.BlockSpec((None,) + tuple(shape), lambda *_: idx, pipeline_mode=pl.Buffered(1))


def _mod_kernel(cs_ref, w_ref, b_ref, o_ref):
    cs = cs_ref[...]
    a = (cs * _sigmoid(cs)).astype(BF16)
    o_ref[...] = _dot(a, w_ref[...].astype(BF16)) + b_ref[...]


def _mod_call(cs, w_ada, b_ada):
    n_layers = w_ada.shape[0]
    return pl.pallas_call(
        _mod_kernel,
        grid=(n_layers, N_ADA),
        in_specs=[pl.BlockSpec((MOD_ROWS, D_MODEL), lambda l, k: (0, 0)),
                  pl.BlockSpec((None, D_MODEL, D_MODEL), lambda l, k: (l, 0, k)),
                  pl.BlockSpec((None, 1, D_MODEL), lambda l, k: (l, 0, k))],
        out_specs=pl.BlockSpec((None, MOD_ROWS, D_MODEL), lambda l, k: (l, 0, k)),
        out_shape=jax.ShapeDtypeStruct((n_layers, MOD_ROWS, N_ADA * D_MODEL), F32),
        compiler_params=_params(2),
        name="ada_map",
    )(cs, w_ada, b_ada.reshape(n_layers, 1, N_ADA * D_MODEL))


def _mod_spec(layer, k, row_fn):
    base = (layer * N_ADA + k) * MOD_ROWS
    return pl.BlockSpec((None, 1, D_MODEL), lambda i: (base + row_fn(i), 0, 0))


def _ffn_kernel(*refs, chunks, final, n_first, layer):
    refs = list(refs)
    o_ref = refs.pop()
    gf_ref = refs.pop() if final else None
    h2_ref = refs.pop(1) if n_first is not None else None
    h_ref, sh_ref, sc_ref, gt_ref, g_ref, w1_ref, w3_ref, w2_ref = refs
    g_ref = g_ref.at[pl.ds(layer, 1)]
    h = h_ref[...]
    if n_first is not None:
        h = jnp.where(pl.program_id(0) < n_first, h, h2_ref[...])
    xb = _ada_norm(h, g_ref, sc_ref, sh_ref)
    acc = None
    off = 0
    for c in chunks:
        a = _dot(xb, w1_ref[:, off:off + c].astype(BF16))
        b = _dot(xb, w3_ref[:, off:off + c].astype(BF16))
        act = (a * _sigmoid(a) * b).astype(BF16)
        y = _dot(act, w2_ref[off:off + c, :].astype(BF16))
        acc = y if acc is None else acc + y
        off += c
    out = h + (HALF * gt_ref[...]) * acc
    if final:
        out = _rms(out, gf_ref[...])
    o_ref[...] = out


def _ffn_call(hs, modr, layer, kbase, g, w1, w3, w2, *, n_tiles, tm, row_fn, final_g=None, hs2=None):
    assert D_FF % QCOL == 0
    chunks = (QCOL,) * (D_FF // QCOL)
    final = final_g is not None
    n_first = None if hs2 is None else hs.shape[0] // tm
    if hs2 is None:
        in_specs = [pl.BlockSpec((tm, D_MODEL), lambda i: (i, 0))]
        args = [hs]
    else:
        in_specs = [pl.BlockSpec((tm, D_MODEL), lambda i: (jnp.minimum(i, n_first - 1), 0)),
                    pl.BlockSpec((tm, D_MODEL), lambda i: (jnp.maximum(i - n_first, 0), 0))]
        args = [hs, hs2]
    in_specs += [_mod_spec(layer, kbase, row_fn), _mod_spec(layer, kbase + 1, row_fn),
                 _mod_spec(layer, kbase + 2, row_fn),
                 _const_spec(g.shape),
                 _layer_spec((D_MODEL, D_FF), layer), _layer_spec((D_MODEL, D_FF), layer),
                 _layer_spec((D_FF, D_MODEL), layer)]
    args += [modr, modr, modr, g, w1, w3, w2]
    if final:
        in_specs.append(_const_spec((1, D_MODEL)))
        args.append(final_g.reshape(1, D_MODEL))
    return pl.pallas_call(
        functools.partial(_ffn_kernel, chunks=chunks, final=final, n_first=n_first, layer=layer),
        grid=(n_tiles,),
        in_specs=in_specs,
        out_specs=pl.BlockSpec((tm, D_MODEL), lambda i: (i, 0)),
        out_shape=jax.ShapeDtypeStruct((n_tiles * tm, D_MODEL), F32),
        compiler_params=_params(),
        name="ffn",
    )(*args)


def _mix_in_kernel(h_ref, sh_ref, sc_ref, g_ref, win_ref, gq_ref, wqa_ref, gkv_ref,
                   wuk_ref, wuvt_ref, cos_ref, sin_ref, dftc_ref,
                   q_ref, k_ref, vt_ref, vc_ref, gc_ref, gs_ref, *, layer):
    g_ref, gq_ref, gkv_ref = (r.at[pl.ds(layer, 1)] for r in (g_ref, gq_ref, gkv_ref))
    h = h_ref[...]
    u = _ada_norm(h, g_ref, sc_ref, sh_ref)
    o_cq = 2 * CONV_CH
    o_ckv = o_cq + Q_LORA
    o_kr = o_ckv + KV_LORA
    z = lax.dot_general(u, win_ref[:o_kr + HEAD_PAD, :].astype(BF16), NT_DIMS,
                        preferred_element_type=F32)
    z_four = lax.dot_general(u, win_ref[o_kr + QK_ROPE:, :].astype(BF16), NT_DIMS,
                             preferred_element_type=F32)
    vc_ref[...] = z[:, :CONV_CH] * _sigmoid(z[:, CONV_CH:o_cq])
    cos = cos_ref[...]
    sin = sin_ref[...]
    lane = lax.broadcasted_iota(jnp.int32, (1, HEAD_PAD), 1)
    first_half = ((lane - QK_NOPE) % ROPE_AXIS) < (ROPE_AXIS // 2)

    def rot_half(x):
        fwd = pltpu.roll(x, ROPE_AXIS // 2, axis=1)
        bwd = pltpu.roll(x, HEAD_PAD - ROPE_AXIS // 2, axis=1)
        return jnp.where(first_half, -bwd, fwd)

    cqn = _rms(z[:, o_cq:o_ckv], gq_ref[...]).astype(BF16)
    qa = _dot(cqn, wqa_ref[...])
    for hh in range(N_HEADS):
        s = slice(hh * HEAD_PAD, (hh + 1) * HEAD_PAD)
        q_ref[hh] = ((qa[:, s] * cos + rot_half(qa[:, s]) * sin) * Q_SCALE).astype(BF16)
    ckvn = _rms(z[:, o_ckv:o_kr], gkv_ref[...]).astype(BF16)
    kn = _dot(ckvn, wuk_ref[...])
    rope_lanes = jnp.logical_and(lane >= QK_NOPE, lane < QK_NOPE + QK_ROPE)
    zkr = jnp.where(rope_lanes, pltpu.roll(z[:, o_kr:], QK_NOPE, axis=1), 0.0)
    kr = zkr * cos + rot_half(zkr) * sin
    for hh in range(N_HEADS):
        s = slice(hh * HEAD_PAD, (hh + 1) * HEAD_PAD)
        k_ref[hh] = (kn[:, s] + kr).astype(BF16)
    vt_ref[...] = lax.dot_general(wuvt_ref[...], ckvn, NT_DIMS,
                                  preferred_element_type=F32).astype(BF16)
    zf = z_four.astype(BF16)
    for gi in range(FOURIER_GROUPS):
        s = slice(gi * FOURIER_GROUP_CH, (gi + 1) * FOURIER_GROUP_CH)
        r = _dot(zf[:, s], dftc_ref[...])
        gc_ref[:, s] = r[:, :FOURIER_GROUP_CH]
        gs_ref[:, s] = r[:, FOURIER_GROUP_CH:]


def _mix_in_call(hs, modr, layer, g_mix, wl, tabs, dims):
    bsz, seq, lc, tm = dims["B"], dims["S"], dims["Lc"], dims["tm"]
    n_rows = bsz * (seq + lc)
    tpb = seq // tm
    n_lat = bsz * tpb
    n_tiles = n_rows // tm
    row_fn = dims["row_fn"]

    def tab_idx(i):
        return (jnp.where(i < n_lat, i % tpb, tpb), 0)

    def g_idx(i):
        return (jnp.where(i < n_lat, i % tpb, tpb), jnp.where(i < n_lat, i // tpb, i - n_lat))

    in_specs = [pl.BlockSpec((tm, D_MODEL), lambda i: (i, 0)),
                _mod_spec(layer, 3, row_fn), _mod_spec(layer, 4, row_fn),
                _const_spec(g_mix.shape),
                _layer_spec((MIX_WIDTH, D_MODEL), layer),
                _const_spec(wl["g_q"].shape),
                _layer_spec((Q_LORA, N_HEADS * HEAD_PAD), layer),
                _const_spec(wl["g_kv"].shape),
                _layer_spec((KV_LORA, N_HEADS * HEAD_PAD), layer),
                _layer_spec((N_HEADS * V_DIM, KV_LORA), layer),
                pl.BlockSpec((tm, HEAD_PAD), tab_idx), pl.BlockSpec((tm, HEAD_PAD), tab_idx),
                _const_spec((FOURIER_GROUP_CH, 2 * FOURIER_GROUP_CH))]
    out_specs = [pl.BlockSpec((N_HEADS, tm, HEAD_PAD), lambda i: (0, i, 0)),
                 pl.BlockSpec((N_HEADS, tm, HEAD_PAD), lambda i: (0, i, 0)),
                 pl.BlockSpec((N_HEADS * V_DIM, tm), lambda i: (0, i)),
                 pl.BlockSpec((tm, CONV_CH), lambda i: (i, 0)),
                 pl.BlockSpec((tm, FOURIER_WIDTH), g_idx),
                 pl.BlockSpec((tm, FOURIER_WIDTH), g_idx)]
    out_shape = [jax.ShapeDtypeStruct((N_HEADS, n_rows, HEAD_PAD), BF16),
                 jax.ShapeDtypeStruct((N_HEADS, n_rows, HEAD_PAD), BF16),
                 jax.ShapeDtypeStruct((N_HEADS * V_DIM, n_rows), BF16),
                 jax.ShapeDtypeStruct((n_rows, CONV_CH), F32),
                 jax.ShapeDtypeStruct((seq + lc, bsz * FOURIER_WIDTH), F32),
                 jax.ShapeDtypeStruct((seq + lc, bsz * FOURIER_WIDTH), F32)]
    return pl.pallas_call(
        functools.partial(_mix_in_kernel, layer=layer),
        grid=(n_tiles,),
        in_specs=in_specs,
        out_specs=out_specs,
        out_shape=out_shape,
        compiler_params=_params(),
        name="mix_in",
    )(hs, modr, modr, g_mix, wl["w_in"], wl["g_q"], wl["w_uq_a"],
      wl["g_kv"], wl["w_uk"], wl["w_uvt"], tabs["cos"], tabs["sin"], tabs["dftc"])


def _attn_kernel(q_ref, kc_ref, vc_ref, kl_ref, vl_ref, o_ref, *, tk):
    segs = [(kc_ref, vc_ref, 0, kc_ref.shape[0])]
    segs += [(kl_ref, vl_ref, s0, tk) for s0 in range(0, kl_ref.shape[0], tk)]
    n_blk = q_ref.shape[0] // QCOL
    cols = [slice(i * QCOL, (i + 1) * QCOL) for i in range(n_blk)]
    qs = [q_ref[c, :] for c in cols]

    def scores(j, i):
        kref, _, s0, size = segs[j]
        return lax.dot_general(kref[s0:s0 + size, :], qs[i], NT_DIMS,
                               preferred_element_type=F32)

    m = [None] * n_blk
    acc = [None] * n_blk
    order = [(j, i) for j in range(len(segs)) for i in range(n_blk)]
    look = min(SCORE_LOOKAHEAD, len(order))
    queue = [scores(*order[t]) for t in range(look)]
    for t, (j, i) in enumerate(order):
        _, vref, s0, size = segs[j]
        if t + look < len(order):
            queue.append(scores(*order[t + look]))
        s_cur = queue.pop(0)
        v_aug = jnp.concatenate([vref[:, s0:s0 + size], jnp.ones((DENOM_ROWS, size), BF16)], axis=0)
        smax = jnp.max(s_cur, axis=0, keepdims=True)
        m_new = smax if m[i] is None else jnp.maximum(m[i], smax)
        p = jnp.exp2(s_cur - m_new).astype(BF16)
        pv = _dot(v_aug, p)
        acc[i] = pv if m[i] is None else jnp.exp2(m[i] - m_new) * acc[i] + pv
        m[i] = m_new
    for i in range(n_blk):
        o_ref[:, cols[i]] = (acc[i][:V_DIM] / acc[i][V_DIM:V_DIM + 1]).astype(BF16)


def _attn_ctx_kernel(q_ref, k_ref, vt_ref, o_ref):
    ones = jnp.ones((DENOM_ROWS, k_ref.shape[1]), BF16)
    for hh in range(N_HEADS):
        rows = slice(hh * V_DIM, (hh + 1) * V_DIM)
        s = lax.dot_general(k_ref[hh], q_ref[hh], NT_DIMS, preferred_element_type=F32)
        p = jnp.exp2(s - jnp.max(s, axis=0, keepdims=True)).astype(BF16)
        pv = _dot(jnp.concatenate([vt_ref[rows, :], ones], axis=0), p)
        o_ref[rows, :] = (pv[:V_DIM] / pv[V_DIM:V_DIM + 1]).astype(BF16)


def _attn_call(q, k, vt, dims):
    bsz, seq, lc, tq = dims["B"], dims["S"], dims["Lc"], dims["tq"]
    ctx_blk0 = (bsz * seq) // lc
    nq = seq // tq
    return pl.pallas_call(
        functools.partial(_attn_kernel, tk=dims["tk"]),
        grid=(bsz, N_HEADS, nq),
        in_specs=[pl.BlockSpec((None, tq, HEAD_PAD), lambda b, h, i: (h, b * nq + i, 0)),
                  pl.BlockSpec((None, lc, HEAD_PAD), lambda b, h, i: (h, ctx_blk0 + b, 0)),
                  pl.BlockSpec((V_DIM, lc), lambda b, h, i: (h, ctx_blk0 + b)),
                  pl.BlockSpec((None, seq, HEAD_PAD), lambda b, h, i: (h, b, 0)),
                  pl.BlockSpec((V_DIM, seq), lambda b, h, i: (h, b))],
        out_specs=pl.BlockSpec((V_DIM, tq), lambda b, h, i: (h, b * nq + i)),
        out_shape=jax.ShapeDtypeStruct((N_HEADS * V_DIM, bsz * seq), BF16),
        compiler_params=_params(3),
        name="attn_lat",
    )(q, k, vt, k, vt)


def _attn_ctx_call(q, k, vt, dims):
    bsz, seq, lc = dims["B"], dims["S"], dims["Lc"]
    ctx_blk0 = (bsz * seq) // lc
    heads = pl.BlockSpec((N_HEADS, lc, HEAD_PAD), lambda b: (0, ctx_blk0 + b, 0))
    return pl.pallas_call(
        _attn_ctx_kernel,
        grid=(bsz,),
        in_specs=[heads, heads, pl.BlockSpec((N_HEADS * V_DIM, lc), lambda b: (0, ctx_blk0 + b))],
        out_specs=pl.BlockSpec((N_HEADS * V_DIM, lc), lambda b: (0, b)),
        out_shape=jax.ShapeDtypeStruct((N_HEADS * V_DIM, bsz * lc), BF16),
        compiler_params=_params(),
        name="attn_ctx",
    )(q, k, vt)


def _fft1_kernel(gc_ref, gs_ref, m_ref, zr_ref, zi_ref):
    n1 = gc_ref.shape[0]
    for j in range(FFT_GROUP):
        x = jnp.concatenate([gc_ref[:, j, :], gs_ref[:, j, :]], axis=0).astype(BF16)
        z = _dot(m_ref[j], x)
        zr_ref[:, j, :] = z[:n1]
        zi_ref[:, j, :] = z[n1:]


def _fft2_kernel(zr_ref, zi_ref, w_ref, f_ref):
    for j in range(FFT_GROUP):
        z = jnp.concatenate([zr_ref[j], zi_ref[j]], axis=0).astype(BF16)
        f_ref[:, j, :] = _dot(w_ref[...], z)


def _fft_tables(seq):
    n1 = int(round(seq ** 0.5))
    n2 = seq // n1
    assert n1 * n2 == seq and n1 % FFT_GROUP == 0 and n2 % FFT_GROUP == 0
    k1 = np.arange(n1)[None, :, None]
    t1 = np.arange(n1)[None, None, :]
    t2 = np.arange(n2)[:, None, None]
    ang = -2.0 * np.pi * (((t1 * k1) % n1) / n1 + ((t2 * k1) % seq) / seq)
    ar = np.cos(ang) / np.sqrt(n1)
    ai = np.sin(ang) / np.sqrt(n1)
    m1 = np.concatenate([np.concatenate([ar, ai], axis=2),
                         np.concatenate([ai, -ar], axis=2)], axis=1)
    k2 = np.arange(n2)[:, None]
    tt = np.arange(n2)[None, :]
    a2 = 2.0 * np.pi * ((k2 * tt) % n2) / n2
    w2 = np.concatenate([np.cos(a2), np.sin(a2)], axis=1) / np.sqrt(n2)
    return n1, n2, jnp.asarray(m1, F32).astype(BF16), jnp.asarray(w2, F32).astype(BF16)


def _fourier_lat_call(gc, gs, dims):
    bsz, seq = dims["B"], dims["S"]
    n1, n2, m1, w2 = _fft_tables(seq)
    ncol = bsz * FOURIER_WIDTH
    assert gc.shape[0] % n2 == 0
    gc3 = gc.reshape(gc.shape[0] // n2, n2, ncol)
    gs3 = gs.reshape(gs.shape[0] // n2, n2, ncol)
    zshape = jax.ShapeDtypeStruct((n1, n2, ncol), F32)
    strided = pl.BlockSpec((n1, FFT_GROUP, ncol), lambda j: (0, j, 0))
    zr, zi = pl.pallas_call(
        _fft1_kernel,
        grid=(n2 // FFT_GROUP,),
        in_specs=[strided, strided,
                  pl.BlockSpec((FFT_GROUP, 2 * n1, 2 * n1), lambda j: (j, 0, 0))],
        out_specs=[strided, strided],
        out_shape=[zshape, zshape],
        compiler_params=_params(),
        name="fft_stage1",
    )(gc3, gs3, m1)
    slab = pl.BlockSpec((FFT_GROUP, n2, ncol), lambda j: (j, 0, 0))
    f3 = pl.pallas_call(
        _fft2_kernel,
        grid=(n1 // FFT_GROUP,),
        in_specs=[slab, slab, _const_spec((n2, 2 * n2))],
        out_specs=pl.BlockSpec((n2, FFT_GROUP, ncol), lambda j: (0, j, 0)),
        out_shape=jax.ShapeDtypeStruct((n2, n1, ncol), F32),
        compiler_params=_params(),
        name="fft_stage2",
    )(zr, zi, w2)
    return f3.reshape(seq, ncol)


def _dft_ctx_kernel(gc_ref, gs_ref, c_ref, s_ref, f_ref):
    f_ref[...] = (_dot(c_ref[...], gc_ref[...].astype(BF16))
                  - _dot(s_ref[...], gs_ref[...].astype(BF16)))


def _fourier_ctx_call(gc, gs, dims):
    bsz, lc = dims["B"], dims["Lc"]
    ncol = bsz * FOURIER_WIDTH
    kt = (np.arange(lc)[:, None] * np.arange(lc)[None, :]) % lc
    ang = 2.0 * np.pi * kt / lc
    cm = jnp.asarray(np.cos(ang) / np.sqrt(lc), F32).astype(BF16)
    sm = jnp.asarray(np.sin(ang) / np.sqrt(lc), F32).astype(BF16)
    ctx_cols = pl.BlockSpec((lc, ncol), lambda j: (dims["S"] // lc, 0))
    return pl.pallas_call(
        _dft_ctx_kernel,
        grid=(1,),
        in_specs=[ctx_cols, ctx_cols, _const_spec((lc, lc)), _const_spec((lc, lc))],
        out_specs=pl.BlockSpec((lc, ncol), lambda j: (0, 0)),
        out_shape=jax.ShapeDtypeStruct((lc, ncol), F32),
        compiler_params=_params(),
        name="dft_ctx",
    )(gc, gs, cm, sm)


def _merge_kernel(*refs, with_ctx, n_lat, tpb, tm, layer):
    refs = list(refs)
    xs_ref = refs.pop()
    xpad_ref = refs.pop()
    o_ref = refs.pop()
    it = iter(refs)
    h_ref, sh_ref, sc_ref, gt_ref, g_ref, wbg_ref, bbg_ref = [next(it) for _ in range(7)]
    vc_ref, vp_ref, vn_ref, wdw_ref, bdw_ref, lng_ref, lnb_ref, wpw_ref, bpw_ref = [
        next(it) for _ in range(9)]
    otl_ref = next(it)
    otc_ref = next(it) if with_ctx else None
    wo_ref = next(it)
    fl_ref = next(it)
    fc_ref = next(it) if with_ctx else None
    wf_ref, bf_ref, wout_ref = [next(it) for _ in range(3)]
    g_ref, bbg_ref, bdw_ref, lng_ref, lnb_ref, bpw_ref, bf_ref = (
        r.at[pl.ds(layer, 1)] for r in (g_ref, bbg_ref, bdw_ref, lng_ref, lnb_ref, bpw_ref, bf_ref))

    i = pl.program_id(0)
    pos = i % tpb
    first = pos == 0
    last = pos == tpb - 1
    if with_ctx:
        is_lat = i < n_lat
        first = jnp.logical_or(first, jnp.logical_not(is_lat))
        last = jnp.logical_or(last, jnp.logical_not(is_lat))

    h = h_ref[...]
    u = _ada_norm(h, g_ref, sc_ref, sh_ref)

    xpad_ref[0:HALO, :] = jnp.where(first, 0.0, vp_ref[...])
    xpad_ref[HALO:HALO + tm, :] = vc_ref[...]
    xpad_ref[HALO + tm:, :] = jnp.where(last, 0.0, vn_ref[...])
    base = HALO - CONV_K // 2
    n_shift_rows = xs_ref.shape[1]
    for r in range(SUBLANES):
        xs_ref[r] = xpad_ref[pl.ds(r, n_shift_rows), :]

    def conv_lanes(c0):
        acc = None
        for kk in range(CONV_K):
            off = base + kk
            a0 = (off // SUBLANES) * SUBLANES
            t = (xs_ref[off % SUBLANES, a0:a0 + tm, c0:c0 + LANES]
                 * wdw_ref[kk:kk + 1, c0:c0 + LANES])
            acc = t if acc is None else acc + t
        return acc + bdw_ref[:, c0:c0 + LANES]

    ot = otl_ref[...]
    fo = fl_ref[...]
    if with_ctx:
        ot = jnp.where(is_lat, ot, otc_ref[...])
        fo = jnp.where(is_lat, fo, fc_ref[...])
    assert CONV_CH == N_BRANCH * LANES
    gate_pre = []
    conv_parts = []
    for br in range(N_BRANCH):
        cols = slice(br * D_MODEL, (br + 1) * D_MODEL)
        gate_pre.append(_dot(u, wbg_ref[:, cols]) + bbg_ref[:, cols])
        conv_parts.append(conv_lanes(br * LANES))
    y_mla = lax.dot_general(ot, wo_ref[...], TN_DIMS, preferred_element_type=F32)
    y_four = _dot(fo.astype(BF16), wf_ref[...]) + bf_ref[...]
    conv = jnp.concatenate(conv_parts, axis=-1)
    mu = jnp.mean(conv, axis=-1, keepdims=True)
    cen = conv - mu
    var = jnp.mean(cen * cen, axis=-1, keepdims=True)
    ln = cen * lax.rsqrt(var + EPS) * lng_ref[...] + lnb_ref[...]
    y_conv = _dot((ln * _sigmoid(ln)).astype(BF16), wpw_ref[...]) + bpw_ref[...]

    mix = (_sigmoid(gate_pre[0]) * y_conv + _sigmoid(gate_pre[1]) * y_mla
           + _sigmoid(gate_pre[2]) * y_four)
    y = _dot(mix.astype(BF16), wout_ref[...])
    o_ref[...] = h + gt_ref[...] * y


def _merge_call(hs, modr, layer, g_mix, wl, vconv, ot_lat, ot_ctx, f_lat, f_ctx, dims, *, with_ctx):
    bsz, seq, lc, tm = dims["B"], dims["S"], dims["Lc"], dims["tm"]
    tpb = seq // tm
    n_lat = bsz * tpb
    n_tiles = n_lat + (bsz * lc // tm if with_ctx else 0)
    row_fn = dims["row_fn"]
    hpt = tm // HALO
    n_halo = vconv.shape[0] // HALO

    in_specs = [pl.BlockSpec((tm, D_MODEL), lambda i: (i, 0)),
                _mod_spec(layer, 3, row_fn), _mod_spec(layer, 4, row_fn),
                _mod_spec(layer, 5, row_fn),
                _const_spec(g_mix.shape),
                _layer_spec((D_MODEL, N_BRANCH * D_MODEL), layer),
                _const_spec(wl["b_bg"].shape),
                pl.BlockSpec((tm, CONV_CH), lambda i: (i, 0)),
                pl.BlockSpec((HALO, CONV_CH), lambda i: (jnp.maximum(i * hpt - 1, 0), 0)),
                pl.BlockSpec((HALO, CONV_CH), lambda i: (jnp.minimum((i + 1) * hpt, n_halo - 1), 0)),
                _layer_spec((CONV_K + 1, CONV_CH), layer),
                _const_spec(wl["b_dw"].shape), _const_spec(wl["ln_g"].shape),
                _const_spec(wl["ln_b"].shape),
                _layer_spec((CONV_CH, D_MODEL), layer), _const_spec(wl["b_pw"].shape),
                pl.BlockSpec((N_HEADS * V_DIM, tm), lambda i: (0, jnp.minimum(i, n_lat - 1)))]
    args = [hs, modr, modr, modr, g_mix, wl["w_bg"], wl["b_bg"],
            vconv, vconv, vconv, wl["w_dw"], wl["b_dw"], wl["ln_g"], wl["ln_b"], wl["w_pw"],
            wl["b_pw"], ot_lat]
    if with_ctx:
        in_specs.append(pl.BlockSpec((N_HEADS * V_DIM, tm), lambda i: (0, jnp.maximum(i - n_lat, 0))))
        args.append(ot_ctx)
    in_specs.append(_layer_spec((N_HEADS * V_DIM, D_MODEL), layer))
    args.append(wl["w_o"])
    in_specs.append(pl.BlockSpec(
        (tm, FOURIER_WIDTH),
        lambda i: (jnp.where(i < n_lat, i % tpb, 0), jnp.where(i < n_lat, i // tpb, 0))))
    args.append(f_lat)
    if with_ctx:
        in_specs.append(pl.BlockSpec((tm, FOURIER_WIDTH), lambda i: (0, jnp.maximum(i - n_lat, 0))))
        args.append(f_ctx)
    in_specs += [_layer_spec((FOURIER_WIDTH, D_MODEL), layer), _const_spec(wl["b_f"].shape),
                 _layer_spec((D_MODEL, D_MODEL), layer)]
    args += [wl["w_f"], wl["b_f"], wl["w_out"]]
    return pl.pallas_call(
        functools.partial(_merge_kernel, with_ctx=with_ctx, n_lat=n_lat, tpb=tpb, tm=tm, layer=layer),
        grid=(n_tiles,),
        in_specs=in_specs,
        out_specs=pl.BlockSpec((tm, D_MODEL), lambda i: (i, 0)),
        out_shape=jax.ShapeDtypeStruct((n_tiles * tm, D_MODEL), F32),
        scratch_shapes=[pltpu.VMEM((tm + 2 * HALO, CONV_CH), F32),
                        pltpu.VMEM((SUBLANES, tm + 2 * HALO - SUBLANES, CONV_CH), F32)],
        compiler_params=_params(),
        name="merge",
    )(*args)


def _stack_weights(p):
    w_in = p["w_in"]
    depth = w_in.shape[0]
    zeros = lambda *shape: jnp.zeros((depth,) + shape, F32)
    tail = HEAD_PAD - QK_NOPE - QK_ROPE
    wq = p["w_uq"].reshape(depth, Q_LORA, N_HEADS, QK_NOPE + QK_ROPE)
    w_uq_a = jnp.concatenate([wq, zeros(Q_LORA, N_HEADS, tail)], axis=-1)
    wkv = p["w_ukv"].reshape(depth, KV_LORA, N_HEADS, QK_NOPE + V_DIM)
    w_uk = jnp.concatenate([wkv[..., :QK_NOPE], zeros(KV_LORA, N_HEADS, HEAD_PAD - QK_NOPE)], axis=-1)
    w_uvt = wkv[..., QK_NOPE:].reshape(depth, KV_LORA, N_HEADS * V_DIM).transpose(0, 2, 1)
    row = lambda v: v
    return {
        "w_in": jnp.swapaxes(w_in, 1, 2),
        "g_q": row(p["g_qnorm"]), "g_kv": row(p["g_kvnorm"]),
        "w_uq_a": w_uq_a.reshape(depth, Q_LORA, N_HEADS * HEAD_PAD).astype(BF16),
        "w_uk": w_uk.reshape(depth, KV_LORA, N_HEADS * HEAD_PAD).astype(BF16),
        "w_uvt": w_uvt.astype(BF16),
        "w_bg": p["w_bgate"].astype(BF16), "b_bg": row(p["b_bgate"]),
        "w_dw": jnp.concatenate([p["w_dw"], zeros(1, CONV_CH)], axis=1),
        "b_dw": row(p["b_dw"]), "ln_g": row(p["ln_g_conv"]), "ln_b": row(p["ln_b_conv"]),
        "w_pw": p["w_pw_conv"].astype(BF16), "b_pw": row(p["b_pw_conv"]),
        "w_o": p["w_o_mla"].astype(BF16),
        "w_f": p["w_fourier"].astype(BF16), "b_f": row(p["b_fourier"]),
        "w_out": p["w_out"].astype(BF16),
    }


def _tables(seq, lc):
    rows = seq // GRID_W
    row = jnp.broadcast_to(jnp.arange(rows, dtype=F32)[:, None], (rows, GRID_W)).reshape(-1)
    col = jnp.broadcast_to(jnp.arange(GRID_W, dtype=F32)[None, :], (rows, GRID_W)).reshape(-1)
    inv = 1.0 / (ROPE_BASE ** (jnp.arange(ROPE_AXIS // 2, dtype=F32) * 2.0 / ROPE_AXIS))
    ar = row[:, None] * inv
    ac = col[:, None] * inv
    ang = jnp.concatenate([ar, ar, ac, ac], axis=-1)
    tail = HEAD_PAD - QK_NOPE - QK_ROPE
    cos = jnp.concatenate([jnp.ones((seq, QK_NOPE), F32), jnp.cos(ang), jnp.ones((seq, tail), F32)], axis=1)
    sin = jnp.concatenate([jnp.zeros((seq, QK_NOPE), F32), jnp.sin(ang), jnp.zeros((seq, tail), F32)], axis=1)
    cos = jnp.concatenate([cos, jnp.ones((lc, HEAD_PAD), F32)], axis=0)
    sin = jnp.concatenate([sin, jnp.zeros((lc, HEAD_PAD), F32)], axis=0)
    c = np.arange(FOURIER_GROUP_CH)
    a = 2.0 * np.pi * ((c[:, None] * c[None, :]) % FOURIER_GROUP_CH) / FOURIER_GROUP_CH
    dftc = np.concatenate([np.cos(a), np.sin(a)], axis=1) / np.sqrt(FOURIER_GROUP_CH)
    return {"cos": cos, "sin": sin, "dftc": jnp.asarray(dftc, F32).astype(BF16)}


def kernel(x, c, ctx, c_ctx, w_ada, b_ada, g_ffn1, w1_ffn1, w3_ffn1, w2_ffn1, g_mix, w_in, w_dw, b_dw, ln_g_conv, ln_b_conv, w_pw_conv, b_pw_conv, g_qnorm, w_uq, g_kvnorm, w_ukv, w_o_mla, w_fourier, b_fourier, w_bgate, b_bgate, w_out, g_ffn2, w1_ffn2, w3_ffn2, w2_ffn2, g_final):
    bsz, seq, _ = x.shape
    lc = ctx.shape[1]
    depth = w_ada.shape[0]
    tm = lc
    tf = 2 * tm
    assert bsz + 1 <= MOD_ROWS and seq % tf == 0 and (bsz * lc) % tf == 0 and tm % HALO == 0
    n_lat_rows = bsz * seq
    p = dict(w_in=w_in, w_dw=w_dw, b_dw=b_dw, ln_g_conv=ln_g_conv, ln_b_conv=ln_b_conv,
             w_pw_conv=w_pw_conv, b_pw_conv=b_pw_conv, g_qnorm=g_qnorm, w_uq=w_uq,
             g_kvnorm=g_kvnorm, w_ukv=w_ukv, w_o_mla=w_o_mla, w_fourier=w_fourier,
             b_fourier=b_fourier, w_bgate=w_bgate, b_bgate=b_bgate, w_out=w_out)

    def make_row_fn(tile):
        n_lat_tiles, tiles_per_seq = n_lat_rows // tile, seq // tile
        return lambda i: jnp.where(i < n_lat_tiles, i // tiles_per_seq, bsz)

    dims = {"B": bsz, "S": seq, "Lc": lc, "tm": tm, "row_fn": make_row_fn(tm),
            "tq": min(4096, seq), "tk": min(256, seq)}
    ffn_row_fn = make_row_fn(tf)
    n_ffn_all = (n_lat_rows + bsz * lc) // tf
    n_ffn_lat = n_lat_rows // tf

    cs = jnp.concatenate([c, c_ctx[None, :], jnp.zeros((MOD_ROWS - bsz - 1, D_MODEL), F32)], axis=0)
    mod = _mod_call(cs, w_ada, b_ada)
    modr = mod.reshape(depth, MOD_ROWS, N_ADA, D_MODEL).transpose(0, 2, 1, 3)
    modr = modr.reshape(depth * N_ADA * MOD_ROWS, 1, D_MODEL)
    tabs = _tables(seq, lc)

    wl = _stack_weights(p)
    g_mix_r = g_mix
    ffn1 = (g_ffn1, w1_ffn1, w3_ffn1, w2_ffn1)
    ffn2 = (g_ffn2, w1_ffn2, w3_ffn2, w2_ffn2)

    hs = None
    for l in range(depth):
        last = l == depth - 1
        if l == 0:
            hs = _ffn_call(x.reshape(n_lat_rows, D_MODEL), modr, l, 0, *ffn1, n_tiles=n_ffn_all, tm=tf,
                           row_fn=ffn_row_fn, hs2=ctx.reshape(bsz * lc, D_MODEL))
        else:
            hs = _ffn_call(hs, modr, l, 0, *ffn1, n_tiles=n_ffn_all, tm=tf, row_fn=ffn_row_fn)
        q, k, vt, vconv, gc, gs = _mix_in_call(hs, modr, l, g_mix_r, wl, tabs, dims)
        ot_lat = _attn_call(q, k, vt, dims)
        f_lat = _fourier_lat_call(gc, gs, dims)
        if last:
            ot_ctx = f_ctx = None
        else:
            ot_ctx = _attn_ctx_call(q, k, vt, dims)
            f_ctx = _fourier_ctx_call(gc, gs, dims)
        hs = _merge_call(hs, modr, l, g_mix_r, wl, vconv, ot_lat, ot_ctx, f_lat, f_ctx, dims,
                         with_ctx=not last)
        hs = _ffn_call(hs, modr, l, 6, *ffn2, n_tiles=n_ffn_lat if last else n_ffn_all, tm=tf,
                       row_fn=ffn_row_fn, final_g=g_final if last else None)
    return hs.reshape(bsz, seq, D_MODEL)
```

```python
import functools

import numpy as np
import jax
import jax.numpy as jnp
from jax import lax
from jax.experimental import pallas as pl
from jax.experimental.pallas import tpu as pltpu

D_MODEL = 1024
D_FF = 2816
N_ADA = 9
CONV_CH = 384
CONV_K = 31
N_HEADS = 8
Q_LORA = 384
KV_LORA = 256
QK_NOPE = 64
QK_ROPE = 32
V_DIM = 64
ROPE_AXIS = QK_ROPE // 2
ROPE_BASE = 10000.0
GRID_W = 64
EPS = 1e-6
HALF = 0.5
ATTN_SCALE = (QK_NOPE + QK_ROPE) ** -0.5
FOURIER_GROUPS = 4
FOURIER_GROUP_CH = 128
FOURIER_WIDTH = FOURIER_GROUPS * FOURIER_GROUP_CH
N_BRANCH = 3

LANES = 128
SUBLANES = 8
HEAD_PAD = 128
MOD_ROWS = 8
HALO = 16
FFT_GROUP = 8
DENOM_ROWS = 16
QCOL = 256
SCORE_LOOKAHEAD = 6
LOG2E = float(np.log2(np.e))
Q_SCALE = ATTN_SCALE * LOG2E
VMEM_LIMIT = 56 * 1024 * 1024
MIX_WIDTH = 2 * CONV_CH + Q_LORA + KV_LORA + QK_ROPE + FOURIER_WIDTH

BF16 = jnp.bfloat16
F32 = jnp.float32
NT_DIMS = (((1,), (1,)), ((), ()))
TN_DIMS = (((0,), (0,)), ((), ()))


def _sigmoid(x):
    return 1.0 / (1.0 + jnp.exp2(x * (-LOG2E)))


def _rms(x, g):
    return x * lax.rsqrt(jnp.mean(x * x, axis=-1, keepdims=True) + EPS) * g


def _ada_norm(h, g_ref, sc_ref, sh_ref):
    return (_rms(h, g_ref[...] * (1.0 + sc_ref[...])) + sh_ref[...]).astype(BF16)


def _dot(a, b):
    return jnp.dot(a, b, preferred_element_type=F32)


def _params(n_axes=1):
    return pltpu.CompilerParams(dimension_semantics=("arbitrary",) * n_axes,
                                vmem_limit_bytes=VMEM_LIMIT)


def _const_spec(shape):
    zeros = (0,) * len(shape)
    return pl.BlockSpec(shape, lambda *_: zeros, pipeline_mode=pl.Buffered(1))


def _layer_spec(shape, layer):
    idx = (layer,) + (0,) * len(shape)
    return pl.BlockSpec((None,) + tuple(shape), lambda *_: idx, pipeline_mode=pl.Buffered(1))


def _mod_kernel(cs_ref, w_ref, b_ref, o_ref):
    cs = cs_ref[...]
    a = (cs * _sigmoid(cs)).astype(BF16)
    o_ref[...] = _dot(a, w_ref[...].astype(BF16)) + b_ref[...]


def _mod_call(cs, w_ada, b_ada):
    n_layers = w_ada.shape[0]
    return pl.pallas_call(
        _mod_kernel,
        grid=(n_layers, N_ADA),
        in_specs=[pl.BlockSpec((MOD_ROWS, D_MODEL), lambda l, k: (0, 0)),
                  pl.BlockSpec((None, D_MODEL, D_MODEL), lambda l, k: (l, 0, k)),
                  pl.BlockSpec((None, 1, D_MODEL), lambda l, k: (l, 0, k))],
        out_specs=pl.BlockSpec((None, MOD_ROWS, D_MODEL), lambda l, k: (l, 0, k)),
        out_shape=jax.ShapeDtypeStruct((n_layers, MOD_ROWS, N_ADA * D_MODEL), F32),
        compiler_params=_params(2),
        name="ada_map",
    )(cs, w_ada, b_ada.reshape(n_layers, 1, N_ADA * D_MODEL))


def _mod_spec(layer, k, row_fn):
    base = (layer * N_ADA + k) * MOD_ROWS
    return pl.BlockSpec((None, 1, D_MODEL), lambda i: (base + row_fn(i), 0, 0))


def _ffn_kernel(*refs, chunks, final, n_first, layer):
    refs = list(refs)
    o_ref = refs.pop()
    gf_ref = refs.pop() if final else None
    h2_ref = refs.pop(1) if n_first is not None else None
    h_ref, sh_ref, sc_ref, gt_ref, g_ref, w1_ref, w3_ref, w2_ref = refs
    g_ref = g_ref.at[pl.ds(layer, 1)]
    h = h_ref[...]
    if n_first is not None:
        h = jnp.where(pl.program_id(0) < n_first, h, h2_ref[...])
    xb = _ada_norm(h, g_ref, sc_ref, sh_ref)
    acc = None
    off = 0
    for c in chunks:
        a = _dot(xb, w1_ref[:, off:off + c].astype(BF16))
        b = _dot(xb, w3_ref[:, off:off + c].astype(BF16))
        act = (a * _sigmoid(a) * b).astype(BF16)
        y = _dot(act, w2_ref[off:off + c, :].astype(BF16))
        acc = y if acc is None else acc + y
        off += c
    out = h + (HALF * gt_ref[...]) * acc
    if final:
        out = _rms(out, gf_ref[...])
    o_ref[...] = out


def _ffn_call(hs, modr, layer, kbase, g, w1, w3, w2, *, n_tiles, tm, row_fn, final_g=None, hs2=None):
    assert D_FF % QCOL == 0
    chunks = (QCOL,) * (D_FF // QCOL)
    final = final_g is not None
    n_first = None if hs2 is None else hs.shape[0] // tm
    if hs2 is None:
        in_specs = [pl.BlockSpec((tm, D_MODEL), lambda i: (i, 0))]
        args = [hs]
    else:
        in_specs = [pl.BlockSpec((tm, D_MODEL), lambda i: (jnp.minimum(i, n_first - 1), 0)),
                    pl.BlockSpec((tm, D_MODEL), lambda i: (jnp.maximum(i - n_first, 0), 0))]
        args = [hs, hs2]
    in_specs += [_mod_spec(layer, kbase, row_fn), _mod_spec(layer, kbase + 1, row_fn),
                 _mod_spec(layer, kbase + 2, row_fn),
                 _const_spec(g.shape),
                 _layer_spec((D_MODEL, D_FF), layer), _layer_spec((D_MODEL, D_FF), layer),
                 _layer_spec((D_FF, D_MODEL), layer)]
    args += [modr, modr, modr, g, w1, w3, w2]
    if final:
        in_specs.append(_const_spec((1, D_MODEL)))
        args.append(final_g.reshape(1, D_MODEL))
    return pl.pallas_call(
        functools.partial(_ffn_kernel, chunks=chunks, final=final, n_first=n_first, layer=layer),
        grid=(n_tiles,),
        in_specs=in_specs,
        out_specs=pl.BlockSpec((tm, D_MODEL), lambda i: (i, 0)),
        out_shape=jax.ShapeDtypeStruct((n_tiles * tm, D_MODEL), F32),
        compiler_params=_params(),
        name="ffn",
    )(*args)


def _mix_in_kernel(h_ref, sh_ref, sc_ref, g_ref, win_ref, gq_ref, wqa_ref, gkv_ref,
                   wuk_ref, wuvt_ref, cos_ref, sin_ref, dftc_ref,
                   q_ref, k_ref, vt_ref, vc_ref, gc_ref, gs_ref, *, layer):
    g_ref, gq_ref, gkv_ref = (r.at[pl.ds(layer, 1)] for r in (g_ref, gq_ref, gkv_ref))
    h = h_ref[...]
    u = _ada_norm(h, g_ref, sc_ref, sh_ref)
    o_cq = 2 * CONV_CH
    o_ckv = o_cq + Q_LORA
    o_kr = o_ckv + KV_LORA
    z = lax.dot_general(u, win_ref[:o_kr + HEAD_PAD, :].astype(BF16), NT_DIMS,
                        preferred_element_type=F32)
    z_four = lax.dot_general(u, win_ref[o_kr + QK_ROPE:, :].astype(BF16), NT_DIMS,
                             preferred_element_type=F32)
    vc_ref[...] = z[:, :CONV_CH] * _sigmoid(z[:, CONV_CH:o_cq])
    cos = cos_ref[...]
    sin = sin_ref[...]
    lane = lax.broadcasted_iota(jnp.int32, (1, HEAD_PAD), 1)
    first_half = ((lane - QK_NOPE) % ROPE_AXIS) < (ROPE_AXIS // 2)

    def rot_half(x):
        fwd = pltpu.roll(x, ROPE_AXIS // 2, axis=1)
        bwd = pltpu.roll(x, HEAD_PAD - ROPE_AXIS // 2, axis=1)
        return jnp.where(first_half, -bwd, fwd)

    cqn = _rms(z[:, o_cq:o_ckv], gq_ref[...]).astype(BF16)
    qa = _dot(cqn, wqa_ref[...])
    for hh in range(N_HEADS):
        s = slice(hh * HEAD_PAD, (hh + 1) * HEAD_PAD)
        q_ref[hh] = ((qa[:, s] * cos + rot_half(qa[:, s]) * sin) * Q_SCALE).astype(BF16)
    ckvn = _rms(z[:, o_ckv:o_kr], gkv_ref[...]).astype(BF16)
    kn = _dot(ckvn, wuk_ref[...])
    rope_lanes = jnp.logical_and(lane >= QK_NOPE, lane < QK_NOPE + QK_ROPE)
    zkr = jnp.where(rope_lanes, pltpu.roll(z[:, o_kr:], QK_NOPE, axis=1), 0.0)
    kr = zkr * cos + rot_half(zkr) * sin
    for hh in range(N_HEADS):
        s = slice(hh * HEAD_PAD, (hh + 1) * HEAD_PAD)
        k_ref[hh] = (kn[:, s] + kr).astype(BF16)
    vt_ref[...] = lax.dot_general(wuvt_ref[...], ckvn, NT_DIMS,
                                  preferred_element_type=F32).astype(BF16)
    zf = z_four.astype(BF16)
    for gi in range(FOURIER_GROUPS):
        s = slice(gi * FOURIER_GROUP_CH, (gi + 1) * FOURIER_GROUP_CH)
        r = _dot(zf[:, s], dftc_ref[...])
        gc_ref[:, s] = r[:, :FOURIER_GROUP_CH]
        gs_ref[:, s] = r[:, FOURIER_GROUP_CH:]


def _mix_in_call(hs, modr, layer, g_mix, wl, tabs, dims):
    bsz, seq, lc, tm = dims["B"], dims["S"], dims["Lc"], dims["tm"]
    n_rows = bsz * (seq + lc)
    tpb = seq // tm
    n_lat = bsz * tpb
    n_tiles = n_rows // tm
    row_fn = dims["row_fn"]

    def tab_idx(i):
        return (jnp.where(i < n_lat, i % tpb, tpb), 0)

    def g_idx(i):
        return (jnp.where(i < n_lat, i % tpb, tpb), jnp.where(i < n_lat, i // tpb, i - n_lat))

    in_specs = [pl.BlockSpec((tm, D_MODEL), lambda i: (i, 0)),
                _mod_spec(layer, 3, row_fn), _mod_spec(layer, 4, row_fn),
                _const_spec(g_mix.shape),
                _layer_spec((MIX_WIDTH, D_MODEL), layer),
                _const_spec(wl["g_q"].shape),
                _layer_spec((Q_LORA, N_HEADS * HEAD_PAD), layer),
                _const_spec(wl["g_kv"].shape),
                _layer_spec((KV_LORA, N_HEADS * HEAD_PAD), layer),
                _layer_spec((N_HEADS * V_DIM, KV_LORA), layer),
                pl.BlockSpec((tm, HEAD_PAD), tab_idx), pl.BlockSpec((tm, HEAD_PAD), tab_idx),
                _const_spec((FOURIER_GROUP_CH, 2 * FOURIER_GROUP_CH))]
    out_specs = [pl.BlockSpec((N_HEADS, tm, HEAD_PAD), lambda i: (0, i, 0)),
                 pl.BlockSpec((N_HEADS, tm, HEAD_PAD), lambda i: (0, i, 0)),
                 pl.BlockSpec((N_HEADS * V_DIM, tm), lambda i: (0, i)),
                 pl.BlockSpec((tm, CONV_CH), lambda i: (i, 0)),
                 pl.BlockSpec((tm, FOURIER_WIDTH), g_idx),
                 pl.BlockSpec((tm, FOURIER_WIDTH), g_idx)]
    out_shape = [jax.ShapeDtypeStruct((N_HEADS, n_rows, HEAD_PAD), BF16),
                 jax.ShapeDtypeStruct((N_HEADS, n_rows, HEAD_PAD), BF16),
                 jax.ShapeDtypeStruct((N_HEADS * V_DIM, n_rows), BF16),
                 jax.ShapeDtypeStruct((n_rows, CONV_CH), F32),
                 jax.ShapeDtypeStruct((seq + lc, bsz * FOURIER_WIDTH), F32),
                 jax.ShapeDtypeStruct((seq + lc, bsz * FOURIER_WIDTH), F32)]
    return pl.pallas_call(
        functools.partial(_mix_in_kernel, layer=layer),
        grid=(n_tiles,),
        in_specs=in_specs,
        out_specs=out_specs,
        out_shape=out_shape,
        compiler_params=_params(),
        name="mix_in",
    )(hs, modr, modr, g_mix, wl["w_in"], wl["g_q"], wl["w_uq_a"],
      wl["g_kv"], wl["w_uk"], wl["w_uvt"], tabs["cos"], tabs["sin"], tabs["dftc"])


def _attn_kernel(q_ref, kc_ref, vc_ref, kl_ref, vl_ref, o_ref, *, tk):
    segs = [(kc_ref, vc_ref, 0, kc_ref.shape[0])]
    segs += [(kl_ref, vl_ref, s0, tk) for s0 in range(0, kl_ref.shape[0], tk)]
    n_blk = q_ref.shape[0] // QCOL
    cols = [slice(i * QCOL, (i + 1) * QCOL) for i in range(n_blk)]
    qs = [q_ref[c, :] for c in cols]

    def scores(j, i):
        kref, _, s0, size = segs[j]
        return lax.dot_general(kref[s0:s0 + size, :], qs[i], NT_DIMS,
                               preferred_element_type=F32)

    m = [None] * n_blk
    acc = [None] * n_blk
    order = [(j, i) for j in range(len(segs)) for i in range(n_blk)]
    look = min(SCORE_LOOKAHEAD, len(order))
    queue = [scores(*order[t]) for t in range(look)]
    for t, (j, i) in enumerate(order):
        _, vref, s0, size = segs[j]
        if t + look < len(order):
            queue.append(scores(*order[t + look]))
        s_cur = queue.pop(0)
        v_aug = jnp.concatenate([vref[:, s0:s0 + size], jnp.ones((DENOM_ROWS, size), BF16)], axis=0)
        smax = jnp.max(s_cur, axis=0, keepdims=True)
        m_new = smax if m[i] is None else jnp.maximum(m[i], smax)
        p = jnp.exp2(s_cur - m_new).astype(BF16)
        pv = _dot(v_aug, p)
        acc[i] = pv if m[i] is None else jnp.exp2(m[i] - m_new) * acc[i] + pv
        m[i] = m_new
    for i in range(n_blk):
        o_ref[:, cols[i]] = (acc[i][:V_DIM] / acc[i][V_DIM:V_DIM + 1]).astype(BF16)


def _attn_ctx_kernel(q_ref, k_ref, vt_ref, o_ref):
    ones = jnp.ones((DENOM_ROWS, k_ref.shape[1]), BF16)
    for hh in range(N_HEADS):
        rows = slice(hh * V_DIM, (hh + 1) * V_DIM)
        s = lax.dot_general(k_ref[hh], q_ref[hh], NT_DIMS, preferred_element_type=F32)
        p = jnp.exp2(s - jnp.max(s, axis=0, keepdims=True)).astype(BF16)
        pv = _dot(jnp.concatenate([vt_ref[rows, :], ones], axis=0), p)
        o_ref[rows, :] = (pv[:V_DIM] / pv[V_DIM:V_DIM + 1]).astype(BF16)


def _attn_call(q, k, vt, dims):
    bsz, seq, lc, tq = dims["B"], dims["S"], dims["Lc"], dims["tq"]
    ctx_blk0 = (bsz * seq) // lc
    nq = seq // tq
    return pl.pallas_call(
        functools.partial(_attn_kernel, tk=dims["tk"]),
        grid=(bsz, N_HEADS, nq),
        in_specs=[pl.BlockSpec((None, tq, HEAD_PAD), lambda b, h, i: (h, b * nq + i, 0)),
                  pl.BlockSpec((None, lc, HEAD_PAD), lambda b, h, i: (h, ctx_blk0 + b, 0)),
                  pl.BlockSpec((V_DIM, lc), lambda b, h, i: (h, ctx_blk0 + b)),
                  pl.BlockSpec((None, seq, HEAD_PAD), lambda b, h, i: (h, b, 0)),
                  pl.BlockSpec((V_DIM, seq), lambda b, h, i: (h, b))],
        out_specs=pl.BlockSpec((V_DIM, tq), lambda b, h, i: (h, b * nq + i)),
        out_shape=jax.ShapeDtypeStruct((N_HEADS * V_DIM, bsz * seq), BF16),
        compiler_params=_params(3),
        name="attn_lat",
    )(q, k, vt, k, vt)


def _attn_ctx_call(q, k, vt, dims):
    bsz, seq, lc = dims["B"], dims["S"], dims["Lc"]
    ctx_blk0 = (bsz * seq) // lc
    heads = pl.BlockSpec((N_HEADS, lc, HEAD_PAD), lambda b: (0, ctx_blk0 + b, 0))
    return pl.pallas_call(
        _attn_ctx_kernel,
        grid=(bsz,),
        in_specs=[heads, heads, pl.BlockSpec((N_HEADS * V_DIM, lc), lambda b: (0, ctx_blk0 + b))],
        out_specs=pl.BlockSpec((N_HEADS * V_DIM, lc), lambda b: (0, b)),
        out_shape=jax.ShapeDtypeStruct((N_HEADS * V_DIM, bsz * lc), BF16),
        compiler_params=_params(),
        name="attn_ctx",
    )(q, k, vt)


def _fft1_kernel(gc_ref, gs_ref, m_ref, zr_ref, zi_ref):
    n1 = gc_ref.shape[0]
    for j in range(FFT_GROUP):
        x = jnp.concatenate([gc_ref[:, j, :], gs_ref[:, j, :]], axis=0).astype(BF16)
        z = _dot(m_ref[j], x)
        zr_ref[:, j, :] = z[:n1]
        zi_ref[:, j, :] = z[n1:]


def _fft2_kernel(zr_ref, zi_ref, w_ref, f_ref):
    for j in range(FFT_GROUP):
        z = jnp.concatenate([zr_ref[j], zi_ref[j]], axis=0).astype(BF16)
        f_ref[:, j, :] = _dot(w_ref[...], z)


def _fft_tables(seq):
    n1 = int(round(seq ** 0.5))
    n2 = seq // n1
    assert n1 * n2 == seq and n1 % FFT_GROUP == 0 and n2 % FFT_GROUP == 0
    k1 = np.arange(n1)[None, :, None]
    t1 = np.arange(n1)[None, None, :]
    t2 = np.arange(n2)[:, None, None]
    ang = -2.0 * np.pi * (((t1 * k1) % n1) / n1 + ((t2 * k1) % seq) / seq)
    ar = np.cos(ang) / np.sqrt(n1)
    ai = np.sin(ang) / np.sqrt(n1)
    m1 = np.concatenate([np.concatenate([ar, ai], axis=2),
                         np.concatenate([ai, -ar], axis=2)], axis=1)
    k2 = np.arange(n2)[:, None]
    tt = np.arange(n2)[None, :]
    a2 = 2.0 * np.pi * ((k2 * tt) % n2) / n2
    w2 = np.concatenate([np.cos(a2), np.sin(a2)], axis=1) / np.sqrt(n2)
    return n1, n2, jnp.asarray(m1, F32).astype(BF16), jnp.asarray(w2, F32).astype(BF16)


def _fourier_lat_call(gc, gs, dims):
    bsz, seq = dims["B"], dims["S"]
    n1, n2, m1, w2 = _fft_tables(seq)
    ncol = bsz * FOURIER_WIDTH
    assert gc.shape[0] % n2 == 0
    gc3 = gc.reshape(gc.shape[0] // n2, n2, ncol)
    gs3 = gs.reshape(gs.shape[0] // n2, n2, ncol)
    zshape = jax.ShapeDtypeStruct((n1, n2, ncol), F32)
    strided = pl.BlockSpec((n1, FFT_GROUP, ncol), lambda j: (0, j, 0))
    zr, zi = pl.pallas_call(
        _fft1_kernel,
        grid=(n2 // FFT_GROUP,),
        in_specs=[strided, strided,
                  pl.BlockSpec((FFT_GROUP, 2 * n1, 2 * n1), lambda j: (j, 0, 0))],
        out_specs=[strided, strided],
        out_shape=[zshape, zshape],
        compiler_params=_params(),
        name="fft_stage1",
    )(gc3, gs3, m1)
    slab = pl.BlockSpec((FFT_GROUP, n2, ncol), lambda j: (j, 0, 0))
    f3 = pl.pallas_call(
        _fft2_kernel,
        grid=(n1 // FFT_GROUP,),
        in_specs=[slab, slab, _const_spec((n2, 2 * n2))],
        out_specs=pl.BlockSpec((n2, FFT_GROUP, ncol), lambda j: (0, j, 0)),
        out_shape=jax.ShapeDtypeStruct((n2, n1, ncol), F32),
        compiler_params=_params(),
        name="fft_stage2",
    )(zr, zi, w2)
    return f3.reshape(seq, ncol)


def _dft_ctx_kernel(gc_ref, gs_ref, c_ref, s_ref, f_ref):
    f_ref[...] = (_dot(c_ref[...], gc_ref[...].astype(BF16))
                  - _dot(s_ref[...], gs_ref[...].astype(BF16)))


def _fourier_ctx_call(gc, gs, dims):
    bsz, lc = dims["B"], dims["Lc"]
    ncol = bsz * FOURIER_WIDTH
    kt = (np.arange(lc)[:, None] * np.arange(lc)[None, :]) % lc
    ang = 2.0 * np.pi * kt / lc
    cm = jnp.asarray(np.cos(ang) / np.sqrt(lc), F32).astype(BF16)
    sm = jnp.asarray(np.sin(ang) / np.sqrt(lc), F32).astype(BF16)
    ctx_cols = pl.BlockSpec((lc, ncol), lambda j: (dims["S"] // lc, 0))
    return pl.pallas_call(
        _dft_ctx_kernel,
        grid=(1,),
        in_specs=[ctx_cols, ctx_cols, _const_spec((lc, lc)), _const_spec((lc, lc))],
        out_specs=pl.BlockSpec((lc, ncol), lambda j: (0, 0)),
        out_shape=jax.ShapeDtypeStruct((lc, ncol), F32),
        compiler_params=_params(),
        name="dft_ctx",
    )(gc, gs, cm, sm)


def _merge_kernel(*refs, with_ctx, n_lat, tpb, tm, layer):
    refs = list(refs)
    xs_ref = refs.pop()
    xpad_ref = refs.pop()
    o_ref = refs.pop()
    it = iter(refs)
    h_ref, sh_ref, sc_ref, gt_ref, g_ref, wbg_ref, bbg_ref = [next(it) for _ in range(7)]
    vc_ref, vp_ref, vn_ref, wdw_ref, bdw_ref, lng_ref, lnb_ref, wpw_ref, bpw_ref = [
        next(it) for _ in range(9)]
    otl_ref = next(it)
    otc_ref = next(it) if with_ctx else None
    wo_ref = next(it)
    fl_ref = next(it)
    fc_ref = next(it) if with_ctx else None
    wf_ref, bf_ref, wout_ref = [next(it) for _ in range(3)]
    g_ref, bbg_ref, bdw_ref, lng_ref, lnb_ref, bpw_ref, bf_ref = (
        r.at[pl.ds(layer, 1)] for r in (g_ref, bbg_ref, bdw_ref, lng_ref, lnb_ref, bpw_ref, bf_ref))

    i = pl.program_id(0)
    pos = i % tpb
    first = pos == 0
    last = pos == tpb - 1
    if with_ctx:
        is_lat = i < n_lat
        first = jnp.logical_or(first, jnp.logical_not(is_lat))
        last = jnp.logical_or(last, jnp.logical_not(is_lat))

    h = h_ref[...]
    u = _ada_norm(h, g_ref, sc_ref, sh_ref)

    xpad_ref[0:HALO, :] = jnp.where(first, 0.0, vp_ref[...])
    xpad_ref[HALO:HALO + tm, :] = vc_ref[...]
    xpad_ref[HALO + tm:, :] = jnp.where(last, 0.0, vn_ref[...])
    base = HALO - CONV_K // 2
    n_shift_rows = xs_ref.shape[1]
    for r in range(SUBLANES):
        xs_ref[r] = xpad_ref[pl.ds(r, n_shift_rows), :]

    def conv_lanes(c0):
        acc = None
        for kk in range(CONV_K):
            off = base + kk
            a0 = (off // SUBLANES) * SUBLANES
            t = (xs_ref[off % SUBLANES, a0:a0 + tm, c0:c0 + LANES]
                 * wdw_ref[kk:kk + 1, c0:c0 + LANES])
            acc = t if acc is None else acc + t
        return acc + bdw_ref[:, c0:c0 + LANES]

    ot = otl_ref[...]
    fo = fl_ref[...]
    if with_ctx:
        ot = jnp.where(is_lat, ot, otc_ref[...])
        fo = jnp.where(is_lat, fo, fc_ref[...])
    assert CONV_CH == N_BRANCH * LANES
    gate_pre = []
    conv_parts = []
    for br in range(N_BRANCH):
        cols = slice(br * D_MODEL, (br + 1) * D_MODEL)
        gate_pre.append(_dot(u, wbg_ref[:, cols]) + bbg_ref[:, cols])
        conv_parts.append(conv_lanes(br * LANES))
    y_mla = lax.dot_general(ot, wo_ref[...], TN_DIMS, preferred_element_type=F32)
    y_four = _dot(fo.astype(BF16), wf_ref[...]) + bf_ref[...]
    conv = jnp.concatenate(conv_parts, axis=-1)
    mu = jnp.mean(conv, axis=-1, keepdims=True)
    cen = conv - mu
    var = jnp.mean(cen * cen, axis=-1, keepdims=True)
    ln = cen * lax.rsqrt(var + EPS) * lng_ref[...] + lnb_ref[...]
    y_conv = _dot((ln * _sigmoid(ln)).astype(BF16), wpw_ref[...]) + bpw_ref[...]

    mix = (_sigmoid(gate_pre[0]) * y_conv + _sigmoid(gate_pre[1]) * y_mla
           + _sigmoid(gate_pre[2]) * y_four)
    y = _dot(mix.astype(BF16), wout_ref[...])
    o_ref[...] = h + gt_ref[...] * y


def _merge_call(hs, modr, layer, g_mix, wl, vconv, ot_lat, ot_ctx, f_lat, f_ctx, dims, *, with_ctx):
    bsz, seq, lc, tm = dims["B"], dims["S"], dims["Lc"], dims["tm"]
    tpb = seq // tm
    n_lat = bsz * tpb
    n_tiles = n_lat + (bsz * lc // tm if with_ctx else 0)
    row_fn = dims["row_fn"]
    hpt = tm // HALO
    n_halo = vconv.shape[0] // HALO

    in_specs = [pl.BlockSpec((tm, D_MODEL), lambda i: (i, 0)),
                _mod_spec(layer, 3, row_fn), _mod_spec(layer, 4, row_fn),
                _mod_spec(layer, 5, row_fn),
                _const_spec(g_mix.shape),
                _layer_spec((D_MODEL, N_BRANCH * D_MODEL), layer),
                _const_spec(wl["b_bg"].shape),
                pl.BlockSpec((tm, CONV_CH), lambda i: (i, 0)),
                pl.BlockSpec((HALO, CONV_CH), lambda i: (jnp.maximum(i * hpt - 1, 0), 0)),
                pl.BlockSpec((HALO, CONV_CH), lambda i: (jnp.minimum((i + 1) * hpt, n_halo - 1), 0)),
                _layer_spec((CONV_K + 1, CONV_CH), layer),
                _const_spec(wl["b_dw"].shape), _const_spec(wl["ln_g"].shape),
                _const_spec(wl["ln_b"].shape),
                _layer_spec((CONV_CH, D_MODEL), layer), _const_spec(wl["b_pw"].shape),
                pl.BlockSpec((N_HEADS * V_DIM, tm), lambda i: (0, jnp.minimum(i, n_lat - 1)))]
    args = [hs, modr, modr, modr, g_mix, wl["w_bg"], wl["b_bg"],
            vconv, vconv, vconv, wl["w_dw"], wl["b_dw"], wl["ln_g"], wl["ln_b"], wl["w_pw"],
            wl["b_pw"], ot_lat]
    if with_ctx:
        in_specs.append(pl.BlockSpec((N_HEADS * V_DIM, tm), lambda i: (0, jnp.maximum(i - n_lat, 0))))
        args.append(ot_ctx)
    in_specs.append(_layer_spec((N_HEADS * V_DIM, D_MODEL), layer))
    args.append(wl["w_o"])
    in_specs.append(pl.BlockSpec(
        (tm, FOURIER_WIDTH),
        lambda i: (jnp.where(i < n_lat, i % tpb, 0), jnp.where(i < n_lat, i // tpb, 0))))
    args.append(f_lat)
    if with_ctx:
        in_specs.append(pl.BlockSpec((tm, FOURIER_WIDTH), lambda i: (0, jnp.maximum(i - n_lat, 0))))
        args.append(f_ctx)
    in_specs += [_layer_spec((FOURIER_WIDTH, D_MODEL), layer), _const_spec(wl["b_f"].shape),
                 _layer_spec((D_MODEL, D_MODEL), layer)]
    args += [wl["w_f"], wl["b_f"], wl["w_out"]]
    return pl.pallas_call(
        functools.partial(_merge_kernel, with_ctx=with_ctx, n_lat=n_lat, tpb=tpb, tm=tm, layer=layer),
        grid=(n_tiles,),
        in_specs=in_specs,
        out_specs=pl.BlockSpec((tm, D_MODEL), lambda i: (i, 0)),
        out_shape=jax.ShapeDtypeStruct((n_tiles * tm, D_MODEL), F32),
        scratch_shapes=[pltpu.VMEM((tm + 2 * HALO, CONV_CH), F32),
                        pltpu.VMEM((SUBLANES, tm + 2 * HALO - SUBLANES, CONV_CH), F32)],
        compiler_params=_params(),
        name="merge",
    )(*args)


def _stack_weights(p):
    w_in = p["w_in"]
    depth = w_in.shape[0]
    zeros = lambda *shape: jnp.zeros((depth,) + shape, F32)
    tail = HEAD_PAD - QK_NOPE - QK_ROPE
    wq = p["w_uq"].reshape(depth, Q_LORA, N_HEADS, QK_NOPE + QK_ROPE)
    w_uq_a = jnp.concatenate([wq, zeros(Q_LORA, N_HEADS, tail)], axis=-1)
    wkv = p["w_ukv"].reshape(depth, KV_LORA, N_HEADS, QK_NOPE + V_DIM)
    w_uk = jnp.concatenate([wkv[..., :QK_NOPE], zeros(KV_LORA, N_HEADS, HEAD_PAD - QK_NOPE)], axis=-1)
    w_uvt = wkv[..., QK_NOPE:].reshape(depth, KV_LORA, N_HEADS * V_DIM).transpose(0, 2, 1)
    row = lambda v: v
    return {
        "w_in": jnp.swapaxes(w_in, 1, 2),
        "g_q": row(p["g_qnorm"]), "g_kv": row(p["g_kvnorm"]),
        "w_uq_a": w_uq_a.reshape(depth, Q_LORA, N_HEADS * HEAD_PAD).astype(BF16),
        "w_uk": w_uk.reshape(depth, KV_LORA, N_HEADS * HEAD_PAD).astype(BF16),
        "w_uvt": w_uvt.astype(BF16),
        "w_bg": p["w_bgate"].astype(BF16), "b_bg": row(p["b_bgate"]),
        "w_dw": jnp.concatenate([p["w_dw"], zeros(1, CONV_CH)], axis=1),
        "b_dw": row(p["b_dw"]), "ln_g": row(p["ln_g_conv"]), "ln_b": row(p["ln_b_conv"]),
        "w_pw": p["w_pw_conv"].astype(BF16), "b_pw": row(p["b_pw_conv"]),
        "w_o": p["w_o_mla"].astype(BF16),
        "w_f": p["w_fourier"].astype(BF16), "b_f": row(p["b_fourier"]),
        "w_out": p["w_out"].astype(BF16),
    }


def _tables(seq, lc):
    rows = seq // GRID_W
    row = jnp.broadcast_to(jnp.arange(rows, dtype=F32)[:, None], (rows, GRID_W)).reshape(-1)
    col = jnp.broadcast_to(jnp.arange(GRID_W, dtype=F32)[None, :], (rows, GRID_W)).reshape(-1)
    inv = 1.0 / (ROPE_BASE ** (jnp.arange(ROPE_AXIS // 2, dtype=F32) * 2.0 / ROPE_AXIS))
    ar = row[:, None] * inv
    ac = col[:, None] * inv
    ang = jnp.concatenate([ar, ar, ac, ac], axis=-1)
    tail = HEAD_PAD - QK_NOPE - QK_ROPE
    cos = jnp.concatenate([jnp.ones((seq, QK_NOPE), F32), jnp.cos(ang), jnp.ones((seq, tail), F32)], axis=1)
    sin = jnp.concatenate([jnp.zeros((seq, QK_NOPE), F32), jnp.sin(ang), jnp.zeros((seq, tail), F32)], axis=1)
    cos = jnp.concatenate([cos, jnp.ones((lc, HEAD_PAD), F32)], axis=0)
    sin = jnp.concatenate([sin, jnp.zeros((lc, HEAD_PAD), F32)], axis=0)
    c = np.arange(FOURIER_GROUP_CH)
    a = 2.0 * np.pi * ((c[:, None] * c[None, :]) % FOURIER_GROUP_CH) / FOURIER_GROUP_CH
    dftc = np.concatenate([np.cos(a), np.sin(a)], axis=1) / np.sqrt(FOURIER_GROUP_CH)
    return {"cos": cos, "sin": sin, "dftc": jnp.asarray(dftc, F32).astype(BF16)}


def kernel(x, c, ctx, c_ctx, w_ada, b_ada, g_ffn1, w1_ffn1, w3_ffn1, w2_ffn1, g_mix, w_in, w_dw, b_dw, ln_g_conv, ln_b_conv, w_pw_conv, b_pw_conv, g_qnorm, w_uq, g_kvnorm, w_ukv, w_o_mla, w_fourier, b_fourier, w_bgate, b_bgate, w_out, g_ffn2, w1_ffn2, w3_ffn2, w2_ffn2, g_final):
    bsz, seq, _ = x.shape
    lc = ctx.shape[1]
    depth = w_ada.shape[0]
    tm = lc
    tf = 2 * tm
    assert bsz + 1 <= MOD_ROWS and seq % tf == 0 and (bsz * lc) % tf == 0 and tm % HALO == 0
    n_lat_rows = bsz * seq
    p = dict(w_in=w_in, w_dw=w_dw, b_dw=b_dw, ln_g_conv=ln_g_conv, ln_b_conv=ln_b_conv,
             w_pw_conv=w_pw_conv, b_pw_conv=b_pw_conv, g_qnorm=g_qnorm, w_uq=w_uq,
             g_kvnorm=g_kvnorm, w_ukv=w_ukv, w_o_mla=w_o_mla, w_fourier=w_fourier,
             b_fourier=b_fourier, w_bgate=w_bgate, b_bgate=b_bgate, w_out=w_out)

    def make_row_fn(tile):
        n_lat_tiles, tiles_per_seq = n_lat_rows // tile, seq // tile
        return lambda i: jnp.where(i < n_lat_tiles, i // tiles_per_seq, bsz)

    dims = {"B": bsz, "S": seq, "Lc": lc, "tm": tm, "row_fn": make_row_fn(tm),
            "tq": min(4096, seq), "tk": min(256, seq)}
    ffn_row_fn = make_row_fn(tf)
    n_ffn_all = (n_lat_rows + bsz * lc) // tf
    n_ffn_lat = n_lat_rows // tf

    cs = jnp.concatenate([c, c_ctx[None, :], jnp.zeros((MOD_ROWS - bsz - 1, D_MODEL), F32)], axis=0)
    mod = _mod_call(cs, w_ada, b_ada)
    modr = mod.reshape(depth, MOD_ROWS, N_ADA, D_MODEL).transpose(0, 2, 1, 3)
    modr = modr.reshape(depth * N_ADA * MOD_ROWS, 1, D_MODEL)
    tabs = _tables(seq, lc)

    wl = _stack_weights(p)
    g_mix_r = g_mix
    ffn1 = (g_ffn1, w1_ffn1, w3_ffn1, w2_ffn1)
    ffn2 = (g_ffn2, w1_ffn2, w3_ffn2, w2_ffn2)

    hs = None
    for l in range(depth):
        last = l == depth - 1
        if l == 0:
            hs = _ffn_call(x.reshape(n_lat_rows, D_MODEL), modr, l, 0, *ffn1, n_tiles=n_ffn_all, tm=tf,
                           row_fn=ffn_row_fn, hs2=ctx.reshape(bsz * lc, D_MODEL))
        else:
            hs = _ffn_call(hs, modr, l, 0, *ffn1, n_tiles=n_ffn_all, tm=tf, row_fn=ffn_row_fn)
        q, k, vt, vconv, gc, gs = _mix_in_call(hs, modr, l, g_mix_r, wl, tabs, dims)
        ot_lat = _attn_call(q, k, vt, dims)
        f_lat = _fourier_lat_call(gc, gs, dims)
        if last:
            ot_ctx = f_ctx = None
        else:
            ot_ctx = _attn_ctx_call(q, k, vt, dims)
            f_ctx = _fourier_ctx_call(gc, gs, dims)
        hs = _merge_call(hs, modr, l, g_mix_r, wl, vconv, ot_lat, ot_ctx, f_lat, f_ctx, dims,
                         with_ctx=not last)
        hs = _ffn_call(hs, modr, l, 6, *ffn2, n_tiles=n_ffn_lat if last else n_ffn_all, tm=tf,
                       row_fn=ffn_row_fn, final_g=g_final if last else None)
    return hs.reshape(bsz, seq, D_MODEL)
```

```python
import functools

import numpy as np
import jax
import jax.numpy as jnp
from jax import lax
from jax.experimental import pallas as pl
from jax.experimental.pallas import tpu as pltpu

D_MODEL = 1024
D_FF = 2816
N_ADA = 9
CONV_CH = 384
CONV_K = 31
N_HEADS = 8
Q_LORA = 384
KV_LORA = 256
QK_NOPE = 64
QK_ROPE = 32
V_DIM = 64
ROPE_AXIS = QK_ROPE // 2
ROPE_BASE = 10000.0
GRID_W = 64
EPS = 1e-6
HALF = 0.5
ATTN_SCALE = (QK_NOPE + QK_ROPE) ** -0.5
FOURIER_GROUPS = 4
FOURIER_GROUP_CH = 128
FOURIER_WIDTH = FOURIER_GROUPS * FOURIER_GROUP_CH
N_BRANCH = 3

LANES = 128
SUBLANES = 8
HEAD_PAD = 128
MOD_ROWS = 8
HALO = 16
FFT_GROUP = 8
DENOM_ROWS = 16
QCOL = 256
SCORE_LOOKAHEAD = 8
LOG2E = float(np.log2(np.e))
Q_SCALE = ATTN_SCALE * LOG2E
VMEM_LIMIT = 56 * 1024 * 1024
MIX_WIDTH = 2 * CONV_CH + Q_LORA + KV_LORA + QK_ROPE + FOURIER_WIDTH

BF16 = jnp.bfloat16
F32 = jnp.float32
NT_DIMS = (((1,), (1,)), ((), ()))
TN_DIMS = (((0,), (0,)), ((), ()))


def _sigmoid(x):
    return 1.0 / (1.0 + jnp.exp2(x * (-LOG2E)))


def _rms(x, g):
    return x * lax.rsqrt(jnp.mean(x * x, axis=-1, keepdims=True) + EPS) * g


def _ada_norm(h, g_ref, sc_ref, sh_ref):
    return (_rms(h, g_ref[...] * (1.0 + sc_ref[...])) + sh_ref[...]).astype(BF16)


def _dot(a, b):
    return jnp.dot(a, b, preferred_element_type=F32)


def _params(n_axes=1):
    return pltpu.CompilerParams(dimension_semantics=("arbitrary",) * n_axes,
                                vmem_limit_bytes=VMEM_LIMIT)


def _const_spec(shape):
    zeros = (0,) * len(shape)
    return pl.BlockSpec(shape, lambda *_: zeros, pipeline_mode=pl.Buffered(1))


def _layer_spec(shape, layer):
    idx = (layer,) + (0,) * len(shape)
    return pl.BlockSpec((None,) + tuple(shape), lambda *_: idx, pipeline_mode=pl.Buffered(1))


def _mod_kernel(cs_ref, w_ref, b_ref, o_ref):
    cs = cs_ref[...]
    a = (cs * _sigmoid(cs)).astype(BF16)
    o_ref[...] = _dot(a, w_ref[...].astype(BF16)) + b_ref[...]


def _mod_call(cs, w_ada, b_ada):
    n_layers = w_ada.shape[0]
    return pl.pallas_call(
        _mod_kernel,
        grid=(n_layers, N_ADA),
        in_specs=[pl.BlockSpec((MOD_ROWS, D_MODEL), lambda l, k: (0, 0)),
                  pl.BlockSpec((None, D_MODEL, D_MODEL), lambda l, k: (l, 0, k)),
                  pl.BlockSpec((None, 1, D_MODEL), lambda l, k: (l, 0, k))],
        out_specs=pl.BlockSpec((None, MOD_ROWS, D_MODEL), lambda l, k: (l, 0, k)),
        out_shape=jax.ShapeDtypeStruct((n_layers, MOD_ROWS, N_ADA * D_MODEL), F32),
        compiler_params=_params(2),
        name="ada_map",
    )(cs, w_ada, b_ada.reshape(n_layers, 1, N_ADA * D_MODEL))


def _mod_spec(layer, k, row_fn):
    base = (layer * N_ADA + k) * MOD_ROWS
    return pl.BlockSpec((None, 1, D_MODEL), lambda i: (base + row_fn(i), 0, 0))


def _ffn_kernel(*refs, chunks, final, n_first, layer):
    refs = list(refs)
    o_ref = refs.pop()
    gf_ref = refs.pop() if final else None
    h2_ref = refs.pop(1) if n_first is not None else None
    h_ref, sh_ref, sc_ref, gt_ref, g_ref, w1_ref, w3_ref, w2_ref = refs
    g_ref = g_ref.at[pl.ds(layer, 1)]
    h = h_ref[...]
    if n_first is not None:
        h = jnp.where(pl.program_id(0) < n_first, h, h2_ref[...])
    xb = _ada_norm(h, g_ref, sc_ref, sh_ref)
    acc = None
    off = 0
    for c in chunks:
        a = _dot(xb, w1_ref[:, off:off + c].astype(BF16))
        b = _dot(xb, w3_ref[:, off:off + c].astype(BF16))
        act = (a * _sigmoid(a) * b).astype(BF16)
        y = _dot(act, w2_ref[off:off + c, :].astype(BF16))
        acc = y if acc is None else acc + y
        off += c
    out = h + (HALF * gt_ref[...]) * acc
    if final:
        out = _rms(out, gf_ref[...])
    o_ref[...] = out


def _ffn_call(hs, modr, layer, kbase, g, w1, w3, w2, *, n_tiles, tm, row_fn, final_g=None, hs2=None):
    assert D_FF % QCOL == 0
    chunks = (QCOL,) * (D_FF // QCOL)
    final = final_g is not None
    n_first = None if hs2 is None else hs.shape[0] // tm
    if hs2 is None:
        in_specs = [pl.BlockSpec((tm, D_MODEL), lambda i: (i, 0))]
        args = [hs]
    else:
        in_specs = [pl.BlockSpec((tm, D_MODEL), lambda i: (jnp.minimum(i, n_first - 1), 0)),
                    pl.BlockSpec((tm, D_MODEL), lambda i: (jnp.maximum(i - n_first, 0), 0))]
        args = [hs, hs2]
    in_specs += [_mod_spec(layer, kbase, row_fn), _mod_spec(layer, kbase + 1, row_fn),
                 _mod_spec(layer, kbase + 2, row_fn),
                 _const_spec(g.shape),
                 _layer_spec((D_MODEL, D_FF), layer), _layer_spec((D_MODEL, D_FF), layer),
                 _layer_spec((D_FF, D_MODEL), layer)]
    args += [modr, modr, modr, g, w1, w3, w2]
    if final:
        in_specs.append(_const_spec((1, D_MODEL)))
        args.append(final_g.reshape(1, D_MODEL))
    return pl.pallas_call(
        functools.partial(_ffn_kernel, chunks=chunks, final=final, n_first=n_first, layer=layer),
        grid=(n_tiles,),
        in_specs=in_specs,
        out_specs=pl.BlockSpec((tm, D_MODEL), lambda i: (i, 0)),
        out_shape=jax.ShapeDtypeStruct((n_tiles * tm, D_MODEL), F32),
        compiler_params=_params(),
        name="ffn",
    )(*args)


def _mix_in_kernel(h_ref, sh_ref, sc_ref, g_ref, win_ref, gq_ref, wqa_ref, gkv_ref,
                   wuk_ref, wuvt_ref, cos_ref, sin_ref, dftc_ref,
                   q_ref, k_ref, vt_ref, vc_ref, gc_ref, gs_ref, *, layer):
    g_ref, gq_ref, gkv_ref = (r.at[pl.ds(layer, 1)] for r in (g_ref, gq_ref, gkv_ref))
    h = h_ref[...]
    u = _ada_norm(h, g_ref, sc_ref, sh_ref)
    o_cq = 2 * CONV_CH
    o_ckv = o_cq + Q_LORA
    o_kr = o_ckv + KV_LORA
    z = lax.dot_general(u, win_ref[:o_kr + HEAD_PAD, :].astype(BF16), NT_DIMS,
                        preferred_element_type=F32)
    z_four = lax.dot_general(u, win_ref[o_kr + QK_ROPE:, :].astype(BF16), NT_DIMS,
                             preferred_element_type=F32)
    vc_ref[...] = z[:, :CONV_CH] * _sigmoid(z[:, CONV_CH:o_cq])
    cos = cos_ref[...]
    sin = sin_ref[...]
    lane = lax.broadcasted_iota(jnp.int32, (1, HEAD_PAD), 1)
    first_half = ((lane - QK_NOPE) % ROPE_AXIS) < (ROPE_AXIS // 2)

    def rot_half(x):
        fwd = pltpu.roll(x, ROPE_AXIS // 2, axis=1)
        bwd = pltpu.roll(x, HEAD_PAD - ROPE_AXIS // 2, axis=1)
        return jnp.where(first_half, -bwd, fwd)

    cqn = _rms(z[:, o_cq:o_ckv], gq_ref[...]).astype(BF16)
    qa = _dot(cqn, wqa_ref[...])
    for hh in range(N_HEADS):
        s = slice(hh * HEAD_PAD, (hh + 1) * HEAD_PAD)
        q_ref[hh] = ((qa[:, s] * cos + rot_half(qa[:, s]) * sin) * Q_SCALE).astype(BF16)
    ckvn = _rms(z[:, o_ckv:o_kr], gkv_ref[...]).astype(BF16)
    kn = _dot(ckvn, wuk_ref[...])
    rope_lanes = jnp.logical_and(lane >= QK_NOPE, lane < QK_NOPE + QK_ROPE)
    zkr = jnp.where(rope_lanes, pltpu.roll(z[:, o_kr:], QK_NOPE, axis=1), 0.0)
    kr = zkr * cos + rot_half(zkr) * sin
    for hh in range(N_HEADS):
        s = slice(hh * HEAD_PAD, (hh + 1) * HEAD_PAD)
        k_ref[hh] = (kn[:, s] + kr).astype(BF16)
    vt_ref[...] = lax.dot_general(wuvt_ref[...], ckvn, NT_DIMS,
                                  preferred_element_type=F32).astype(BF16)
    zf = z_four.astype(BF16)
    for gi in range(FOURIER_GROUPS):
        s = slice(gi * FOURIER_GROUP_CH, (gi + 1) * FOURIER_GROUP_CH)
        r = _dot(zf[:, s], dftc_ref[...])
        gc_ref[:, s] = r[:, :FOURIER_GROUP_CH]
        gs_ref[:, s] = r[:, FOURIER_GROUP_CH:]


def _mix_in_call(hs, modr, layer, g_mix, wl, tabs, dims):
    bsz, seq, lc, tm = dims["B"], dims["S"], dims["Lc"], dims["tm"]
    n_rows = bsz * (seq + lc)
    tpb = seq // tm
    n_lat = bsz * tpb
    n_tiles = n_rows // tm
    row_fn = dims["row_fn"]

    def tab_idx(i):
        return (jnp.where(i < n_lat, i % tpb, tpb), 0)

    def g_idx(i):
        return (jnp.where(i < n_lat, i % tpb, tpb), jnp.where(i < n_lat, i // tpb, i - n_lat))

    in_specs = [pl.BlockSpec((tm, D_MODEL), lambda i: (i, 0)),
                _mod_spec(layer, 3, row_fn), _mod_spec(layer, 4, row_fn),
                _const_spec(g_mix.shape),
                _layer_spec((MIX_WIDTH, D_MODEL), layer),
                _const_spec(wl["g_q"].shape),
                _layer_spec((Q_LORA, N_HEADS * HEAD_PAD), layer),
                _const_spec(wl["g_kv"].shape),
                _layer_spec((KV_LORA, N_HEADS * HEAD_PAD), layer),
                _layer_spec((N_HEADS * V_DIM, KV_LORA), layer),
                pl.BlockSpec((tm, HEAD_PAD), tab_idx), pl.BlockSpec((tm, HEAD_PAD), tab_idx),
                _const_spec((FOURIER_GROUP_CH, 2 * FOURIER_GROUP_CH))]
    out_specs = [pl.BlockSpec((N_HEADS, tm, HEAD_PAD), lambda i: (0, i, 0)),
                 pl.BlockSpec((N_HEADS, tm, HEAD_PAD), lambda i: (0, i, 0)),
                 pl.BlockSpec((N_HEADS * V_DIM, tm), lambda i: (0, i)),
                 pl.BlockSpec((tm, CONV_CH), lambda i: (i, 0)),
                 pl.BlockSpec((tm, FOURIER_WIDTH), g_idx),
                 pl.BlockSpec((tm, FOURIER_WIDTH), g_idx)]
    out_shape = [jax.ShapeDtypeStruct((N_HEADS, n_rows, HEAD_PAD), BF16),
                 jax.ShapeDtypeStruct((N_HEADS, n_rows, HEAD_PAD), BF16),
                 jax.ShapeDtypeStruct((N_HEADS * V_DIM, n_rows), BF16),
                 jax.ShapeDtypeStruct((n_rows, CONV_CH), F32),
                 jax.ShapeDtypeStruct((seq + lc, bsz * FOURIER_WIDTH), F32),
                 jax.ShapeDtypeStruct((seq + lc, bsz * FOURIER_WIDTH), F32)]
    return pl.pallas_call(
        functools.partial(_mix_in_kernel, layer=layer),
        grid=(n_tiles,),
        in_specs=in_specs,
        out_specs=out_specs,
        out_shape=out_shape,
        compiler_params=_params(),
        name="mix_in",
    )(hs, modr, modr, g_mix, wl["w_in"], wl["g_q"], wl["w_uq_a"],
      wl["g_kv"], wl["w_uk"], wl["w_uvt"], tabs["cos"], tabs["sin"], tabs["dftc"])


def _attn_kernel(q_ref, kc_ref, vc_ref, kl_ref, vl_ref, o_ref, *, tk):
    segs = [(kc_ref, vc_ref, 0, kc_ref.shape[0])]
    segs += [(kl_ref, vl_ref, s0, tk) for s0 in range(0, kl_ref.shape[0], tk)]
    n_blk = q_ref.shape[0] // QCOL
    cols = [slice(i * QCOL, (i + 1) * QCOL) for i in range(n_blk)]
    qs = [q_ref[c, :] for c in cols]

    def scores(j, i):
        kref, _, s0, size = segs[j]
        return lax.dot_general(kref[s0:s0 + size, :], qs[i], NT_DIMS,
                               preferred_element_type=F32)

    m = [None] * n_blk
    acc = [None] * n_blk
    order = [(j, i) for j in range(len(segs)) for i in range(n_blk)]
    look = min(SCORE_LOOKAHEAD, len(order))
    queue = [scores(*order[t]) for t in range(look)]
    for t, (j, i) in enumerate(order):
        _, vref, s0, size = segs[j]
        if t + look < len(order):
            queue.append(scores(*order[t + look]))
        s_cur = queue.pop(0)
        v_aug = jnp.concatenate([vref[:, s0:s0 + size], jnp.ones((DENOM_ROWS, size), BF16)], axis=0)
        smax = jnp.max(s_cur, axis=0, keepdims=True)
        m_new = smax if m[i] is None else jnp.maximum(m[i], smax)
        p = jnp.exp2(s_cur - m_new).astype(BF16)
        pv = _dot(v_aug, p)
        acc[i] = pv if m[i] is None else jnp.exp2(m[i] - m_new) * acc[i] + pv
        m[i] = m_new
    for i in range(n_blk):
        o_ref[:, cols[i]] = (acc[i][:V_DIM] / acc[i][V_DIM:V_DIM + 1]).astype(BF16)


def _attn_ctx_kernel(q_ref, k_ref, vt_ref, o_ref):
    ones = jnp.ones((DENOM_ROWS, k_ref.shape[1]), BF16)
    for hh in range(N_HEADS):
        rows = slice(hh * V_DIM, (hh + 1) * V_DIM)
        s = lax.dot_general(k_ref[hh], q_ref[hh], NT_DIMS, preferred_element_type=F32)
        p = jnp.exp2(s - jnp.max(s, axis=0, keepdims=True)).astype(BF16)
        pv = _dot(jnp.concatenate([vt_ref[rows, :], ones], axis=0), p)
        o_ref[rows, :] = (pv[:V_DIM] / pv[V_DIM:V_DIM + 1]).astype(BF16)


def _attn_call(q, k, vt, dims):
    bsz, seq, lc, tq = dims["B"], dims["S"], dims["Lc"], dims["tq"]
    ctx_blk0 = (bsz * seq) // lc
    nq = seq // tq
    return pl.pallas_call(
        functools.partial(_attn_kernel, tk=dims["tk"]),
        grid=(bsz, N_HEADS, nq),
        in_specs=[pl.BlockSpec((None, tq, HEAD_PAD), lambda b, h, i: (h, b * nq + i, 0)),
                  pl.BlockSpec((None, lc, HEAD_PAD), lambda b, h, i: (h, ctx_blk0 + b, 0)),
                  pl.BlockSpec((V_DIM, lc), lambda b, h, i: (h, ctx_blk0 + b)),
                  pl.BlockSpec((None, seq, HEAD_PAD), lambda b, h, i: (h, b, 0)),
                  pl.BlockSpec((V_DIM, seq), lambda b, h, i: (h, b))],
        out_specs=pl.BlockSpec((V_DIM, tq), lambda b, h, i: (h, b * nq + i)),
        out_shape=jax.ShapeDtypeStruct((N_HEADS * V_DIM, bsz * seq), BF16),
        compiler_params=_params(3),
        name="attn_lat",
    )(q, k, vt, k, vt)


def _attn_ctx_call(q, k, vt, dims):
    bsz, seq, lc = dims["B"], dims["S"], dims["Lc"]
    ctx_blk0 = (bsz * seq) // lc
    heads = pl.BlockSpec((N_HEADS, lc, HEAD_PAD), lambda b: (0, ctx_blk0 + b, 0))
    return pl.pallas_call(
        _attn_ctx_kernel,
        grid=(bsz,),
        in_specs=[heads, heads, pl.BlockSpec((N_HEADS * V_DIM, lc), lambda b: (0, ctx_blk0 + b))],
        out_specs=pl.BlockSpec((N_HEADS * V_DIM, lc), lambda b: (0, b)),
        out_shape=jax.ShapeDtypeStruct((N_HEADS * V_DIM, bsz * lc), BF16),
        compiler_params=_params(),
        name="attn_ctx",
    )(q, k, vt)


def _fft1_kernel(gc_ref, gs_ref, m_ref, zr_ref, zi_ref):
    n1 = gc_ref.shape[0]
    for j in range(FFT_GROUP):
        x = jnp.concatenate([gc_ref[:, j, :], gs_ref[:, j, :]], axis=0).astype(BF16)
        z = _dot(m_ref[j], x)
        zr_ref[:, j, :] = z[:n1]
        zi_ref[:, j, :] = z[n1:]


def _fft2_kernel(zr_ref, zi_ref, w_ref, f_ref):
    for j in range(FFT_GROUP):
        z = jnp.concatenate([zr_ref[j], zi_ref[j]], axis=0).astype(BF16)
        f_ref[:, j, :] = _dot(w_ref[...], z)


def _fft_tables(seq):
    n1 = int(round(seq ** 0.5))
    n2 = seq // n1
    assert n1 * n2 == seq and n1 % FFT_GROUP == 0 and n2 % FFT_GROUP == 0
    k1 = np.arange(n1)[None, :, None]
    t1 = np.arange(n1)[None, None, :]
    t2 = np.arange(n2)[:, None, None]
    ang = -2.0 * np.pi * (((t1 * k1) % n1) / n1 + ((t2 * k1) % seq) / seq)
    ar = np.cos(ang) / np.sqrt(n1)
    ai = np.sin(ang) / np.sqrt(n1)
    m1 = np.concatenate([np.concatenate([ar, ai], axis=2),
                         np.concatenate([ai, -ar], axis=2)], axis=1)
    k2 = np.arange(n2)[:, None]
    tt = np.arange(n2)[None, :]
    a2 = 2.0 * np.pi * ((k2 * tt) % n2) / n2
    w2 = np.concatenate([np.cos(a2), np.sin(a2)], axis=1) / np.sqrt(n2)
    return n1, n2, jnp.asarray(m1, F32).astype(BF16), jnp.asarray(w2, F32).astype(BF16)


def _fourier_lat_call(gc, gs, dims):
    bsz, seq = dims["B"], dims["S"]
    n1, n2, m1, w2 = _fft_tables(seq)
    ncol = bsz * FOURIER_WIDTH
    assert gc.shape[0] % n2 == 0
    gc3 = gc.reshape(gc.shape[0] // n2, n2, ncol)
    gs3 = gs.reshape(gs.shape[0] // n2, n2, ncol)
    zshape = jax.ShapeDtypeStruct((n1, n2, ncol), F32)
    strided = pl.BlockSpec((n1, FFT_GROUP, ncol), lambda j: (0, j, 0))
    zr, zi = pl.pallas_call(
        _fft1_kernel,
        grid=(n2 // FFT_GROUP,),
        in_specs=[strided, strided,
                  pl.BlockSpec((FFT_GROUP, 2 * n1, 2 * n1), lambda j: (j, 0, 0))],
        out_specs=[strided, strided],
        out_shape=[zshape, zshape],
        compiler_params=_params(),
        name="fft_stage1",
    )(gc3, gs3, m1)
    slab = pl.BlockSpec((FFT_GROUP, n2, ncol), lambda j: (j, 0, 0))
    f3 = pl.pallas_call(
        _fft2_kernel,
        grid=(n1 // FFT_GROUP,),
        in_specs=[slab, slab, _const_spec((n2, 2 * n2))],
        out_specs=pl.BlockSpec((n2, FFT_GROUP, ncol), lambda j: (0, j, 0)),
        out_shape=jax.ShapeDtypeStruct((n2, n1, ncol), F32),
        compiler_params=_params(),
        name="fft_stage2",
    )(zr, zi, w2)
    return f3.reshape(seq, ncol)


def _dft_ctx_kernel(gc_ref, gs_ref, c_ref, s_ref, f_ref):
    f_ref[...] = (_dot(c_ref[...], gc_ref[...].astype(BF16))
                  - _dot(s_ref[...], gs_ref[...].astype(BF16)))


def _fourier_ctx_call(gc, gs, dims):
    bsz, lc = dims["B"], dims["Lc"]
    ncol = bsz * FOURIER_WIDTH
    kt = (np.arange(lc)[:, None] * np.arange(lc)[None, :]) % lc
    ang = 2.0 * np.pi * kt / lc
    cm = jnp.asarray(np.cos(ang) / np.sqrt(lc), F32).astype(BF16)
    sm = jnp.asarray(np.sin(ang) / np.sqrt(lc), F32).astype(BF16)
    ctx_cols = pl.BlockSpec((lc, ncol), lambda j: (dims["S"] // lc, 0))
    return pl.pallas_call(
        _dft_ctx_kernel,
        grid=(1,),
        in_specs=[ctx_cols, ctx_cols, _const_spec((lc, lc)), _const_spec((lc, lc))],
        out_specs=pl.BlockSpec((lc, ncol), lambda j: (0, 0)),
        out_shape=jax.ShapeDtypeStruct((lc, ncol), F32),
        compiler_params=_params(),
        name="dft_ctx",
    )(gc, gs, cm, sm)


def _merge_kernel(*refs, with_ctx, n_lat, tpb, tm, layer):
    refs = list(refs)
    xs_ref = refs.pop()
    xpad_ref = refs.pop()
    o_ref = refs.pop()
    it = iter(refs)
    h_ref, sh_ref, sc_ref, gt_ref, g_ref, wbg_ref, bbg_ref = [next(it) for _ in range(7)]
    vc_ref, vp_ref, vn_ref, wdw_ref, bdw_ref, lng_ref, lnb_ref, wpw_ref, bpw_ref = [
        next(it) for _ in range(9)]
    otl_ref = next(it)
    otc_ref = next(it) if with_ctx else None
    wo_ref = next(it)
    fl_ref = next(it)
    fc_ref = next(it) if with_ctx else None
    wf_ref, bf_ref, wout_ref = [next(it) for _ in range(3)]
    g_ref, bbg_ref, bdw_ref, lng_ref, lnb_ref, bpw_ref, bf_ref = (
        r.at[pl.ds(layer, 1)] for r in (g_ref, bbg_ref, bdw_ref, lng_ref, lnb_ref, bpw_ref, bf_ref))

    i = pl.program_id(0)
    pos = i % tpb
    first = pos == 0
    last = pos == tpb - 1
    if with_ctx:
        is_lat = i < n_lat
        first = jnp.logical_or(first, jnp.logical_not(is_lat))
        last = jnp.logical_or(last, jnp.logical_not(is_lat))

    h = h_ref[...]
    u = _ada_norm(h, g_ref, sc_ref, sh_ref)

    xpad_ref[0:HALO, :] = jnp.where(first, 0.0, vp_ref[...])
    xpad_ref[HALO:HALO + tm, :] = vc_ref[...]
    xpad_ref[HALO + tm:, :] = jnp.where(last, 0.0, vn_ref[...])
    base = HALO - CONV_K // 2
    n_shift_rows = xs_ref.shape[1]
    for r in range(SUBLANES):
        xs_ref[r] = xpad_ref[pl.ds(r, n_shift_rows), :]

    def conv_lanes(c0):
        acc = None
        for kk in range(CONV_K):
            off = base + kk
            a0 = (off // SUBLANES) * SUBLANES
            t = (xs_ref[off % SUBLANES, a0:a0 + tm, c0:c0 + LANES]
                 * wdw_ref[kk:kk + 1, c0:c0 + LANES])
            acc = t if acc is None else acc + t
        return acc + bdw_ref[:, c0:c0 + LANES]

    ot = otl_ref[...]
    fo = fl_ref[...]
    if with_ctx:
        ot = jnp.where(is_lat, ot, otc_ref[...])
        fo = jnp.where(is_lat, fo, fc_ref[...])
    assert CONV_CH == N_BRANCH * LANES
    gate_pre = []
    conv_parts = []
    for br in range(N_BRANCH):
        cols = slice(br * D_MODEL, (br + 1) * D_MODEL)
        gate_pre.append(_dot(u, wbg_ref[:, cols]) + bbg_ref[:, cols])
        conv_parts.append(conv_lanes(br * LANES))
    y_mla = lax.dot_general(ot, wo_ref[...], TN_DIMS, preferred_element_type=F32)
    y_four = _dot(fo.astype(BF16), wf_ref[...]) + bf_ref[...]
    conv = jnp.concatenate(conv_parts, axis=-1)
    mu = jnp.mean(conv, axis=-1, keepdims=True)
    cen = conv - mu
    var = jnp.mean(cen * cen, axis=-1, keepdims=True)
    ln = cen * lax.rsqrt(var + EPS) * lng_ref[...] + lnb_ref[...]
    y_conv = _dot((ln * _sigmoid(ln)).astype(BF16), wpw_ref[...]) + bpw_ref[...]

    mix = (_sigmoid(gate_pre[0]) * y_conv + _sigmoid(gate_pre[1]) * y_mla
           + _sigmoid(gate_pre[2]) * y_four)
    y = _dot(mix.astype(BF16), wout_ref[...])
    o_ref[...] = h + gt_ref[...] * y


def _merge_call(hs, modr, layer, g_mix, wl, vconv, ot_lat, ot_ctx, f_lat, f_ctx, dims, *, with_ctx):
    bsz, seq, lc, tm = dims["B"], dims["S"], dims["Lc"], dims["tm"]
    tpb = seq // tm
    n_lat = bsz * tpb
    n_tiles = n_lat + (bsz * lc // tm if with_ctx else 0)
    row_fn = dims["row_fn"]
    hpt = tm // HALO
    n_halo = vconv.shape[0] // HALO

    in_specs = [pl.BlockSpec((tm, D_MODEL), lambda i: (i, 0)),
                _mod_spec(layer, 3, row_fn), _mod_spec(layer, 4, row_fn),
                _mod_spec(layer, 5, row_fn),
                _const_spec(g_mix.shape),
                _layer_spec((D_MODEL, N_BRANCH * D_MODEL), layer),
                _const_spec(wl["b_bg"].shape),
                pl.BlockSpec((tm, CONV_CH), lambda i: (i, 0)),
                pl.BlockSpec((HALO, CONV_CH), lambda i: (jnp.maximum(i * hpt - 1, 0), 0)),
                pl.BlockSpec((HALO, CONV_CH), lambda i: (jnp.minimum((i + 1) * hpt, n_halo - 1), 0)),
                _layer_spec((CONV_K + 1, CONV_CH), layer),
                _const_spec(wl["b_dw"].shape), _const_spec(wl["ln_g"].shape),
                _const_spec(wl["ln_b"].shape),
                _layer_spec((CONV_CH, D_MODEL), layer), _const_spec(wl["b_pw"].shape),
                pl.BlockSpec((N_HEADS * V_DIM, tm), lambda i: (0, jnp.minimum(i, n_lat - 1)))]
    args = [hs, modr, modr, modr, g_mix, wl["w_bg"], wl["b_bg"],
            vconv, vconv, vconv, wl["w_dw"], wl["b_dw"], wl["ln_g"], wl["ln_b"], wl["w_pw"],
            wl["b_pw"], ot_lat]
    if with_ctx:
        in_specs.append(pl.BlockSpec((N_HEADS * V_DIM, tm), lambda i: (0, jnp.maximum(i - n_lat, 0))))
        args.append(ot_ctx)
    in_specs.append(_layer_spec((N_HEADS * V_DIM, D_MODEL), layer))
    args.append(wl["w_o"])
    in_specs.append(pl.BlockSpec(
        (tm, FOURIER_WIDTH),
        lambda i: (jnp.where(i < n_lat, i % tpb, 0), jnp.where(i < n_lat, i // tpb, 0))))
    args.append(f_lat)
    if with_ctx:
        in_specs.append(pl.BlockSpec((tm, FOURIER_WIDTH), lambda i: (0, jnp.maximum(i - n_lat, 0))))
        args.append(f_ctx)
    in_specs += [_layer_spec((FOURIER_WIDTH, D_MODEL), layer), _const_spec(wl["b_f"].shape),
                 _layer_spec((D_MODEL, D_MODEL), layer)]
    args += [wl["w_f"], wl["b_f"], wl["w_out"]]
    return pl.pallas_call(
        functools.partial(_merge_kernel, with_ctx=with_ctx, n_lat=n_lat, tpb=tpb, tm=tm, layer=layer),
        grid=(n_tiles,),
        in_specs=in_specs,
        out_specs=pl.BlockSpec((tm, D_MODEL), lambda i: (i, 0)),
        out_shape=jax.ShapeDtypeStruct((n_tiles * tm, D_MODEL), F32),
        scratch_shapes=[pltpu.VMEM((tm + 2 * HALO, CONV_CH), F32),
                        pltpu.VMEM((SUBLANES, tm + 2 * HALO - SUBLANES, CONV_CH), F32)],
        compiler_params=_params(),
        name="merge",
    )(*args)


def _stack_weights(p):
    w_in = p["w_in"]
    depth = w_in.shape[0]
    zeros = lambda *shape: jnp.zeros((depth,) + shape, F32)
    tail = HEAD_PAD - QK_NOPE - QK_ROPE
    wq = p["w_uq"].reshape(depth, Q_LORA, N_HEADS, QK_NOPE + QK_ROPE)
    w_uq_a = jnp.concatenate([wq, zeros(Q_LORA, N_HEADS, tail)], axis=-1)
    wkv = p["w_ukv"].reshape(depth, KV_LORA, N_HEADS, QK_NOPE + V_DIM)
    w_uk = jnp.concatenate([wkv[..., :QK_NOPE], zeros(KV_LORA, N_HEADS, HEAD_PAD - QK_NOPE)], axis=-1)
    w_uvt = wkv[..., QK_NOPE:].reshape(depth, KV_LORA, N_HEADS * V_DIM).transpose(0, 2, 1)
    row = lambda v: v
    return {
        "w_in": jnp.swapaxes(w_in, 1, 2),
        "g_q": row(p["g_qnorm"]), "g_kv": row(p["g_kvnorm"]),
        "w_uq_a": w_uq_a.reshape(depth, Q_LORA, N_HEADS * HEAD_PAD).astype(BF16),
        "w_uk": w_uk.reshape(depth, KV_LORA, N_HEADS * HEAD_PAD).astype(BF16),
        "w_uvt": w_uvt.astype(BF16),
        "w_bg": p["w_bgate"].astype(BF16), "b_bg": row(p["b_bgate"]),
        "w_dw": jnp.concatenate([p["w_dw"], zeros(1, CONV_CH)], axis=1),
        "b_dw": row(p["b_dw"]), "ln_g": row(p["ln_g_conv"]), "ln_b": row(p["ln_b_conv"]),
        "w_pw": p["w_pw_conv"].astype(BF16), "b_pw": row(p["b_pw_conv"]),
        "w_o": p["w_o_mla"].astype(BF16),
        "w_f": p["w_fourier"].astype(BF16), "b_f": row(p["b_fourier"]),
        "w_out": p["w_out"].astype(BF16),
    }


def _tables(seq, lc):
    rows = seq // GRID_W
    row = jnp.broadcast_to(jnp.arange(rows, dtype=F32)[:, None], (rows, GRID_W)).reshape(-1)
    col = jnp.broadcast_to(jnp.arange(GRID_W, dtype=F32)[None, :], (rows, GRID_W)).reshape(-1)
    inv = 1.0 / (ROPE_BASE ** (jnp.arange(ROPE_AXIS // 2, dtype=F32) * 2.0 / ROPE_AXIS))
    ar = row[:, None] * inv
    ac = col[:, None] * inv
    ang = jnp.concatenate([ar, ar, ac, ac], axis=-1)
    tail = HEAD_PAD - QK_NOPE - QK_ROPE
    cos = jnp.concatenate([jnp.ones((seq, QK_NOPE), F32), jnp.cos(ang), jnp.ones((seq, tail), F32)], axis=1)
    sin = jnp.concatenate([jnp.zeros((seq, QK_NOPE), F32), jnp.sin(ang), jnp.zeros((seq, tail), F32)], axis=1)
    cos = jnp.concatenate([cos, jnp.ones((lc, HEAD_PAD), F32)], axis=0)
    sin = jnp.concatenate([sin, jnp.zeros((lc, HEAD_PAD), F32)], axis=0)
    c = np.arange(FOURIER_GROUP_CH)
    a = 2.0 * np.pi * ((c[:, None] * c[None, :]) % FOURIER_GROUP_CH) / FOURIER_GROUP_CH
    dftc = np.concatenate([np.cos(a), np.sin(a)], axis=1) / np.sqrt(FOURIER_GROUP_CH)
    return {"cos": cos, "sin": sin, "dftc": jnp.asarray(dftc, F32).astype(BF16)}


def kernel(x, c, ctx, c_ctx, w_ada, b_ada, g_ffn1, w1_ffn1, w3_ffn1, w2_ffn1, g_mix, w_in, w_dw, b_dw, ln_g_conv, ln_b_conv, w_pw_conv, b_pw_conv, g_qnorm, w_uq, g_kvnorm, w_ukv, w_o_mla, w_fourier, b_fourier, w_bgate, b_bgate, w_out, g_ffn2, w1_ffn2, w3_ffn2, w2_ffn2, g_final):
    bsz, seq, _ = x.shape
    lc = ctx.shape[1]
    depth = w_ada.shape[0]
    tm = lc
    tf = 2 * tm
    assert bsz + 1 <= MOD_ROWS and seq % tf == 0 and (bsz * lc) % tf == 0 and tm % HALO == 0
    n_lat_rows = bsz * seq
    p = dict(w_in=w_in, w_dw=w_dw, b_dw=b_dw, ln_g_conv=ln_g_conv, ln_b_conv=ln_b_conv,
             w_pw_conv=w_pw_conv, b_pw_conv=b_pw_conv, g_qnorm=g_qnorm, w_uq=w_uq,
             g_kvnorm=g_kvnorm, w_ukv=w_ukv, w_o_mla=w_o_mla, w_fourier=w_fourier,
             b_fourier=b_fourier, w_bgate=w_bgate, b_bgate=b_bgate, w_out=w_out)

    def make_row_fn(tile):
        n_lat_tiles, tiles_per_seq = n_lat_rows // tile, seq // tile
        return lambda i: jnp.where(i < n_lat_tiles, i // tiles_per_seq, bsz)

    dims = {"B": bsz, "S": seq, "Lc": lc, "tm": tm, "row_fn": make_row_fn(tm),
            "tq": min(2048, seq), "tk": min(256, seq)}
    ffn_row_fn = make_row_fn(tf)
    n_ffn_all = (n_lat_rows + bsz * lc) // tf
    n_ffn_lat = n_lat_rows // tf

    cs = jnp.concatenate([c, c_ctx[None, :], jnp.zeros((MOD_ROWS - bsz - 1, D_MODEL), F32)], axis=0)
    mod = _mod_call(cs, w_ada, b_ada)
    modr = mod.reshape(depth, MOD_ROWS, N_ADA, D_MODEL).transpose(0, 2, 1, 3)
    modr = modr.reshape(depth * N_ADA * MOD_ROWS, 1, D_MODEL)
    tabs = _tables(seq, lc)

    wl = _stack_weights(p)
    g_mix_r = g_mix
    ffn1 = (g_ffn1, w1_ffn1, w3_ffn1, w2_ffn1)
    ffn2 = (g_ffn2, w1_ffn2, w3_ffn2, w2_ffn2)

    hs = None
    for l in range(depth):
        last = l == depth - 1
        if l == 0:
            hs = _ffn_call(x.reshape(n_lat_rows, D_MODEL), modr, l, 0, *ffn1, n_tiles=n_ffn_all, tm=tf,
                           row_fn=ffn_row_fn, hs2=ctx.reshape(bsz * lc, D_MODEL))
        else:
            hs = _ffn_call(hs, modr, l, 0, *ffn1, n_tiles=n_ffn_all, tm=tf, row_fn=ffn_row_fn)
        q, k, vt, vconv, gc, gs = _mix_in_call(hs, modr, l, g_mix_r, wl, tabs, dims)
        ot_lat = _attn_call(q, k, vt, dims)
        f_lat = _fourier_lat_call(gc, gs, dims)
        if last:
            ot_ctx = f_ctx = None
        else:
            ot_ctx = _attn_ctx_call(q, k, vt, dims)
            f_ctx = _fourier_ctx_call(gc, gs, dims)
        hs = _merge_call(hs, modr, l, g_mix_r, wl, vconv, ot_lat, ot_ctx, f_lat, f_ctx, dims,
                         with_ctx=not last)
        hs = _ffn_call(hs, modr, l, 6, *ffn2, n_tiles=n_ffn_lat if last else n_ffn_all, tm=tf,
                       row_fn=ffn_row_fn, final_g=g_final if last else None)
    return hs.reshape(bsz, seq, D_MODEL)
```
